```python
import math
import jax
import jax.numpy as jnp
from jax import lax
import numpy as np

D_MODEL = 4096
BATCH = 2
SEQ = 4096
DEPTH = 2

HEAD_DIM = 128
N_HEADS = 8
MIX_WIDTH = N_HEADS * HEAD_DIM
N_MIXERS = 3
DIFF_SUB = HEAD_DIM // 2
ROPE_THETA = 500000.0
ROT_DIM = HEAD_DIM // 4
ROT_DIM_DIFF = DIFF_SUB // 4
MOBA_BLOCK = 256
MOBA_TOPK = 3
MOBA_Q_CHUNK = 32
Q_BLOCK = 128
PEER_HEADS = 8
PEER_NKEYS = 128
PEER_EXPERTS = PEER_NKEYS * PEER_NKEYS
PEER_QDIM = 128
PEER_TOPK = 16
PEER_TOKEN_CHUNK = 64
PLE_DIM = 256
MAX_POS_OFFSET = 4096
NORM_EPS = 1e-6
NEG_INF = -1e30
W_IN_COLS = 3 * N_MIXERS * MIX_WIDTH + N_MIXERS * D_MODEL

kernel_name = "hybrid_moba_diff_stickbreak_peer_block"


def rms_norm(x, gain):
    xf = x.astype(jnp.float32)
    y = xf * lax.rsqrt(jnp.mean(xf * xf, axis=-1, keepdims=True) + NORM_EPS)
    return (y * gain.astype(jnp.float32)).astype(x.dtype)


def rotary_partial(x, positions, rot_dim):
    half = rot_dim // 2
    inv_freq = ROPE_THETA ** (-jnp.arange(half, dtype=jnp.float32) / half)
    ang = positions.astype(jnp.float32)[..., None] * inv_freq
    cos = jnp.cos(ang)[:, :, None, :]
    sin = jnp.sin(ang)[:, :, None, :]
    xr = x[..., :rot_dim].astype(jnp.float32)
    x1, x2 = xr[..., :half], xr[..., half:]
    rot = jnp.concatenate([x1 * cos - x2 * sin, x2 * cos + x1 * sin], axis=-1).astype(x.dtype)
    return jnp.concatenate([rot, x[..., rot_dim:]], axis=-1)


def split_q_blocks(t, c):
    b, h, s, d = t.shape
    return t.reshape(b, h, s // c, c, d).transpose(2, 0, 1, 3, 4)


def merge_q_blocks(y):
    n, b, h, c, d = y.shape
    return y.transpose(1, 2, 0, 3, 4).reshape(b, h, n * c, d)


def moba_attention(q, k, v):
    b, h, s, d = q.shape
    nb = -(-s // MOBA_BLOCK)
    pad = nb * MOBA_BLOCK - s
    kp = jnp.pad(k, ((0, 0), (0, 0), (0, pad), (0, 0)))
    vp = jnp.pad(v, ((0, 0), (0, 0), (0, pad), (0, 0)))
    kb = kp.reshape(b, h, nb, MOBA_BLOCK, d)
    vb = vp.reshape(b, h, nb, MOBA_BLOCK, d)
    kmean = jnp.mean(kb.astype(jnp.float32), axis=3)
    topk = min(MOBA_TOPK, nb)
    scale = d ** -0.5
    n_chunks = s // MOBA_Q_CHUNK
    b_idx = jnp.arange(b)[:, None, None, None]
    h_idx = jnp.arange(h)[None, :, None, None]
    blk_ids = jnp.arange(nb)
    local = jnp.arange(MOBA_BLOCK)

    def chunk(args):
        qi, ci = args
        q_pos = ci * MOBA_Q_CHUNK + jnp.arange(MOBA_Q_CHUNK)
        cur = (ci * MOBA_Q_CHUNK) // MOBA_BLOCK
        gate = jnp.einsum('bhcd,bhnd->bhcn', qi.astype(jnp.float32), kmean)
        gate = jnp.where((blk_ids < cur)[None, None, None, :], gate, NEG_INF)
        _, idx = lax.top_k(gate, topk)
        valid = idx < cur
        ksel = kb[b_idx, h_idx, idx]
        vsel = vb[b_idx, h_idx, idx]
        s_sel = jnp.einsum('bhcd,bhcknd->bhckn', qi, ksel,
                           preferred_element_type=jnp.float32) * scale
        s_sel = jnp.where(valid[..., None], s_sel, NEG_INF)
        s_sel = s_sel.reshape(b, h, MOBA_Q_CHUNK, topk * MOBA_BLOCK)
        k_own = lax.dynamic_slice_in_dim(kp, cur * MOBA_BLOCK, MOBA_BLOCK, axis=2)
        v_own = lax.dynamic_slice_in_dim(vp, cur * MOBA_BLOCK, MOBA_BLOCK, axis=2)
        k_pos = cur * MOBA_BLOCK + local
        s_own = jnp.einsum('bhcd,bhnd->bhcn', qi, k_own,
                           preferred_element_type=jnp.float32) * scale
        s_own = jnp.where((k_pos[None, :] <= q_pos[:, None])[None, None], s_own, NEG_INF)
        probs = jax.nn.softmax(jnp.concatenate([s_sel, s_own], axis=-1), axis=-1).astype(v.dtype)
        p_sel = probs[..., :topk * MOBA_BLOCK].reshape(b, h, MOBA_Q_CHUNK, topk, MOBA_BLOCK)
        p_own = probs[..., topk * MOBA_BLOCK:]
        return (jnp.einsum('bhckn,bhcknd->bhcd', p_sel, vsel)
                + jnp.einsum('bhcn,bhnd->bhcd', p_own, v_own))

    out = lax.map(chunk, (split_q_blocks(q, MOBA_Q_CHUNK), jnp.arange(n_chunks)))
    return merge_q_blocks(out)


def diff_attention(q1, q2, k1, k2, v, lam):
    s = q1.shape[2]
    scale = q1.shape[-1] ** -0.5
    k_pos = jnp.arange(s)

    def block(args):
        a1, a2, bi = args
        q_pos = bi * Q_BLOCK + jnp.arange(Q_BLOCK)
        causal = (k_pos[None, :] <= q_pos[:, None])[None, None]
        s1 = jnp.einsum('bhqd,bhkd->bhqk', a1, k1, preferred_element_type=jnp.float32) * scale
        s2 = jnp.einsum('bhqd,bhkd->bhqk', a2, k2, preferred_element_type=jnp.float32) * scale
        p1 = jax.nn.softmax(jnp.where(causal, s1, NEG_INF), axis=-1)
        p2 = jax.nn.softmax(jnp.where(causal, s2, NEG_INF), axis=-1)
        w = (p1 - lam * p2).astype(v.dtype)
        return jnp.einsum('bhqk,bhkd->bhqd', w, v)

    out = lax.map(block, (split_q_blocks(q1, Q_BLOCK), split_q_blocks(q2, Q_BLOCK),
                          jnp.arange(s // Q_BLOCK)))
    return merge_q_blocks(out)


def stick_breaking_attention(q, k, v):
    s = q.shape[2]
    scale = q.shape[-1] ** -0.5
    k_pos = jnp.arange(s)

    def block(args):
        qb, bi = args
        q_pos = bi * Q_BLOCK + jnp.arange(Q_BLOCK)
        strict = (k_pos[None, :] < q_pos[:, None])[None, None]
        z = jnp.einsum('bhqd,bhkd->bhqk', qb, k, preferred_element_type=jnp.float32) * scale
        log_beta = jax.nn.log_sigmoid(z)
        log_1m = jnp.where(strict, jax.nn.log_sigmoid(-z), 0.0)
        between = lax.cumsum(log_1m, axis=3, reverse=True) - log_1m
        a = jnp.where(strict, jnp.exp(log_beta + between), 0.0).astype(v.dtype)
        return jnp.einsum('bhqk,bhkd->bhqd', a, v)

    out = lax.map(block, (split_q_blocks(q, Q_BLOCK), jnp.arange(s // Q_BLOCK)))
    return merge_q_blocks(out)


def peer_ffn(x, w_q, key_a, key_b, u, v):
    b, s, d = x.shape
    half = PEER_QDIM // 2
    q = (x @ w_q).reshape(b, s, PEER_HEADS, PEER_QDIM).astype(jnp.float32)
    sa = jnp.einsum('bshd,hnd->bshn', q[..., :half], key_a.astype(jnp.float32))
    sb = jnp.einsum('bshd,hnd->bshn', q[..., half:], key_b.astype(jnp.float32))
    va, ia = lax.top_k(sa, PEER_TOPK)
    vb, ib = lax.top_k(sb, PEER_TOPK)
    cand = (va[..., :, None] + vb[..., None, :]).reshape(b, s, PEER_HEADS, PEER_TOPK * PEER_TOPK)
    cand_idx = (ia[..., :, None] * PEER_NKEYS + ib[..., None, :]).reshape(
        b, s, PEER_HEADS, PEER_TOPK * PEER_TOPK)
    top_s, pos = lax.top_k(cand, PEER_TOPK)
    expert = jnp.take_along_axis(cand_idx, pos, axis=-1)
    g = jax.nn.softmax(top_s, axis=-1)
    t = b * s
    n_chunks = t // PEER_TOKEN_CHUNK
    n_sel = PEER_HEADS * PEER_TOPK
    xt = x.reshape(n_chunks, PEER_TOKEN_CHUNK, d)
    et = expert.reshape(n_chunks, PEER_TOKEN_CHUNK, n_sel)
    gt = g.reshape(n_chunks, PEER_TOKEN_CHUNK, n_sel)

    def chunk(args):
        xc, ec, gc = args
        hid = jnp.einsum('td,ted->te', xc, u[ec], preferred_element_type=jnp.float32)
        coef = (gc * jax.nn.gelu(hid, approximate=False)).astype(x.dtype)
        return jnp.einsum('te,ted->td', coef, v[ec])

    out = lax.map(chunk, (xt, et, gt))
    return out.reshape(b, s, d)


def setup_inputs(seed: int = 0) -> dict:
    key = jax.random.key(seed)
    ks = jax.random.split(key, 26)
    f32 = jnp.float32

    def nrm(k, shape, scale):
        return jax.random.normal(k, shape, f32) * scale

    def gain(k, shape):
        return 1.0 + 0.05 * jax.random.normal(k, shape, f32)

    x = nrm(ks[0], (BATCH, SEQ, D_MODEL), 1.0)
    p = nrm(ks[1], (DEPTH, BATCH, SEQ, PLE_DIM), 1.0)
    positions = (jax.random.randint(ks[2], (BATCH, 1), 0, MAX_POS_OFFSET, dtype=jnp.int32)
                 + jnp.arange(SEQ, dtype=jnp.int32)[None, :])
    return {
        'x': x,
        'p': p,
        'positions': positions,
        'norm_mix': gain(ks[3], (DEPTH, D_MODEL)),
        'w_in': nrm(ks[4], (DEPTH, D_MODEL, W_IN_COLS), D_MODEL ** -0.5),
        'qn_moba': gain(ks[5], (DEPTH, HEAD_DIM)),
        'kn_moba': gain(ks[6], (DEPTH, HEAD_DIM)),
        'qn_diff': gain(ks[7], (DEPTH, DIFF_SUB)),
        'kn_diff': gain(ks[8], (DEPTH, DIFF_SUB)),
        'lambda_q1': nrm(ks[9], (DEPTH, DIFF_SUB), 0.1),
        'lambda_k1': nrm(ks[10], (DEPTH, DIFF_SUB), 0.1),
        'lambda_q2': nrm(ks[11], (DEPTH, DIFF_SUB), 0.1),
        'lambda_k2': nrm(ks[12], (DEPTH, DIFF_SUB), 0.1),
        'subln_diff': gain(ks[13], (DEPTH, HEAD_DIM)),
        'w_branch': nrm(ks[14], (DEPTH, N_MIXERS, MIX_WIDTH, D_MODEL), MIX_WIDTH ** -0.5),
        'w_out': nrm(ks[15], (DEPTH, D_MODEL, D_MODEL), D_MODEL ** -0.5),
        'norm_ffn': gain(ks[16], (DEPTH, D_MODEL)),
        'peer_wq': nrm(ks[17], (DEPTH, D_MODEL, PEER_HEADS * PEER_QDIM), D_MODEL ** -0.5),
        'peer_key_a': nrm(ks[18], (DEPTH, PEER_HEADS, PEER_NKEYS, PEER_QDIM // 2), (PEER_QDIM // 2) ** -0.5),
        'peer_key_b': nrm(ks[19], (DEPTH, PEER_HEADS, PEER_NKEYS, PEER_QDIM // 2), (PEER_QDIM // 2) ** -0.5),
        'peer_u': nrm(ks[20], (DEPTH, PEER_EXPERTS, D_MODEL), D_MODEL ** -0.5),
        'peer_v': nrm(ks[21], (DEPTH, PEER_EXPERTS, D_MODEL), PEER_HEADS ** -0.5),
        'norm_ple': gain(ks[22], (DEPTH, D_MODEL)),
        'w_ple_gate': nrm(ks[23], (DEPTH, D_MODEL, D_MODEL), D_MODEL ** -0.5),
        'w_ple_proj': nrm(ks[24], (DEPTH, PLE_DIM, D_MODEL), PLE_DIM ** -0.5),
    }


def reference(x, p, positions, norm_mix, w_in, qn_moba, kn_moba, qn_diff, kn_diff,
              lambda_q1, lambda_k1, lambda_q2, lambda_k2, subln_diff, w_branch, w_out,
              norm_ffn, peer_wq, peer_key_a, peer_key_b, peer_u, peer_v,
              norm_ple, w_ple_gate, w_ple_proj):
    b, s, _ = x.shape
    splits = [MIX_WIDTH * j for j in range(1, 3 * N_MIXERS + 1)] + [
        3 * N_MIXERS * MIX_WIDTH + D_MODEL * j for j in range(1, N_MIXERS)]

    def heads_first(t):
        return jnp.swapaxes(t, 1, 2)

    h = x
    for i in range(DEPTH):
        n1 = rms_norm(h, norm_mix[i])
        proj = n1 @ w_in[i]
        (q_m, k_m, v_m, q_d, k_d, v_d, q_s, k_s, v_s,
         g_m, g_d, g_s) = jnp.split(proj, splits, axis=-1)

        qm = rotary_partial(rms_norm(q_m.reshape(b, s, N_HEADS, HEAD_DIM), qn_moba[i]), positions, ROT_DIM)
        km = rotary_partial(rms_norm(k_m.reshape(b, s, N_HEADS, HEAD_DIM), kn_moba[i]), positions, ROT_DIM)
        vm = v_m.reshape(b, s, N_HEADS, HEAD_DIM)
        o_m = moba_attention(heads_first(qm), heads_first(km), heads_first(vm))
        o_m = heads_first(o_m).reshape(b, s, MIX_WIDTH)

        qd = rotary_partial(rms_norm(q_d.reshape(b, s, 2 * N_HEADS, DIFF_SUB), qn_diff[i]),
                            positions, ROT_DIM_DIFF).reshape(b, s, N_HEADS, 2, DIFF_SUB)
        kd = rotary_partial(rms_norm(k_d.reshape(b, s, 2 * N_HEADS, DIFF_SUB), kn_diff[i]),
                            positions, ROT_DIM_DIFF).reshape(b, s, N_HEADS, 2, DIFF_SUB)
        vd = v_d.reshape(b, s, N_HEADS, HEAD_DIM)
        lam_init = 0.8 - 0.6 * math.exp(-0.3 * i)
        lam = (jnp.exp(jnp.sum((lambda_q1[i] * lambda_k1[i]).astype(jnp.float32)))
               - jnp.exp(jnp.sum((lambda_q2[i] * lambda_k2[i]).astype(jnp.float32))) + lam_init)
        o_d = diff_attention(heads_first(qd[..., 0, :]), heads_first(qd[..., 1, :]),
                             heads_first(kd[..., 0, :]), heads_first(kd[..., 1, :]),
                             heads_first(vd), lam)
        o_d = rms_norm(o_d, subln_diff[i]) * (1.0 - lam_init)
        o_d = heads_first(o_d).reshape(b, s, MIX_WIDTH)

        qs = q_s.reshape(b, s, N_HEADS, HEAD_DIM)
        ks_ = k_s.reshape(b, s, N_HEADS, HEAD_DIM)
        vs = v_s.reshape(b, s, N_HEADS, HEAD_DIM)
        o_s = stick_breaking_attention(heads_first(qs), heads_first(ks_), heads_first(vs))
        o_s = heads_first(o_s).reshape(b, s, MIX_WIDTH)

        merged = (jax.nn.sigmoid(g_m) * (o_m @ w_branch[i, 0])
                  + jax.nn.sigmoid(g_d) * (o_d @ w_branch[i, 1])
                  + jax.nn.sigmoid(g_s) * (o_s @ w_branch[i, 2]))
        h = h + merged @ w_out[i]

        n2 = rms_norm(h, norm_ffn[i])
        h = h + peer_ffn(n2, peer_wq[i], peer_key_a[i], peer_key_b[i], peer_u[i], peer_v[i])

        gate = jax.nn.sigmoid(rms_norm(h, norm_ple[i]) @ w_ple_gate[i])
        h = h + gate * (p[i] @ w_ple_proj[i])
    return h
```

```python
import functools
import math

import jax
import jax.numpy as jnp
from jax import lax
from jax.experimental import pallas as pl
from jax.experimental.pallas import tpu as pltpu

F32 = jnp.float32
BF16 = jnp.bfloat16

LANES = 128
HEAD_DIM = 128
N_HEADS = 8
MIX_WIDTH = N_HEADS * HEAD_DIM
DIFF_SUB = HEAD_DIM // 2
ROPE_THETA = 500000.0
ROT_HALF = HEAD_DIM // 8
ROT_HALF_DIFF = DIFF_SUB // 8
MOBA_BLOCK = 256
MOBA_TOPK = 3
PEER_HEADS = 8
PEER_NKEYS = 128
PEER_TOPK = 16
NORM_EPS = 1e-6
NEG_INF = -1e30
VMEM_LIMIT = 56 * 1024 * 1024

_NT = (((1,), (1,)), ((), ()))


def _params(*sem):
    return pltpu.CompilerParams(dimension_semantics=sem, vmem_limit_bytes=VMEM_LIMIT)


def _rmsnorm_kernel(x_ref, g_ref, o_ref):
    x = x_ref[...]
    ms = jnp.mean(x * x, axis=-1, keepdims=True)
    o_ref[...] = (x * lax.rsqrt(ms + NORM_EPS) * g_ref[...]).astype(o_ref.dtype)


def rmsnorm(x, gain, tm=256):
    t, d = x.shape
    tm = min(tm, t)
    return pl.pallas_call(
        _rmsnorm_kernel,
        grid=(t // tm,),
        in_specs=[pl.BlockSpec((tm, d), lambda i: (i, 0)),
                  pl.BlockSpec((1, d), lambda i: (0, 0))],
        out_specs=pl.BlockSpec((tm, d), lambda i: (i, 0)),
        out_shape=jax.ShapeDtypeStruct((t, d), BF16),
        compiler_params=_params("parallel"),
        name="rmsnorm",
    )(x, gain.reshape(1, d))


def _mm_kernel(*refs, mode):
    if mode == "plain":
        a_ref, b_ref, o_ref, acc_ref = refs
    elif mode == "residual":
        a_ref, b_ref, r_ref, o_ref, acc_ref = refs
    else:
        a_ref, b_ref, r_ref, p_ref, wp_ref, o_ref, acc_ref = refs
    k = pl.program_id(2)

    @pl.when(k == 0)
    def _():
        acc_ref[...] = jnp.zeros_like(acc_ref)

    acc_ref[...] += jnp.dot(a_ref[...], b_ref[...].astype(BF16), preferred_element_type=F32)

    @pl.when(k == pl.num_programs(2) - 1)
    def _():
        acc = acc_ref[...]
        if mode == "plain":
            o_ref[...] = acc.astype(o_ref.dtype)
        elif mode == "residual":
            o_ref[...] = (r_ref[...] + acc).astype(o_ref.dtype)
        else:
            emb = jnp.dot(p_ref[...].astype(BF16), wp_ref[...].astype(BF16),
                          preferred_element_type=F32)
            o_ref[...] = (r_ref[...] + jax.nn.sigmoid(acc) * emb).astype(o_ref.dtype)


def matmul(a, b, *, mode="plain", res=None, p=None, wp=None, out_dtype=F32,
           tm=1024, tn=1024, tk=512):
    m, kdim = a.shape
    n = b.shape[1]
    tm, tn, tk = min(tm, m), min(tn, n), min(tk, kdim)
    in_specs = [pl.BlockSpec((tm, tk), lambda i, j, k: (i, k)),
                pl.BlockSpec((tk, tn), lambda i, j, k: (k, j))]
    args = [a, b]
    if mode in ("residual", "ple"):
        in_specs.append(pl.BlockSpec((tm, tn), lambda i, j, k: (i, j)))
        args.append(res)
    if mode == "ple":
        pd = p.shape[1]
        in_specs += [pl.BlockSpec((tm, pd), lambda i, j, k: (i, 0)),
                     pl.BlockSpec((pd, tn), lambda i, j, k: (0, j))]
        args += [p, wp]
    return pl.pallas_call(
        functools.partial(_mm_kernel, mode=mode),
        grid=(m // tm, n // tn, kdim // tk),
        in_specs=in_specs,
        out_specs=pl.BlockSpec((tm, tn), lambda i, j, k: (i, j)),
        out_shape=jax.ShapeDtypeStruct((m, n), out_dtype),
        scratch_shapes=[pltpu.VMEM((tm, tn), F32)],
        compiler_params=_params("parallel", "parallel", "arbitrary"),
        name="matmul_" + mode,
    )(*args)


def _rope(x, cos, sin, half, period):
    lane = lax.broadcasted_iota(jnp.int32, x.shape, 1)
    first = (lane & (period - 1)) < half
    partner = jnp.where(first, pltpu.roll(x, LANES - half, 1), pltpu.roll(x, half, 1))
    return x * cos + partner * jnp.where(first, -sin, sin)


def _prep_kernel(pos_ref, fm_ref, fd_ref, qng_ref, kng_ref, qdg_ref, kdg_ref,
                 qm_i, km_i, vm_i, qd_i, kd_i, vd_i, qs_i, ks_i, vs_i,
                 qm_o, qmf_o, km_o, vm_o, qd_o, kd_o, vd_o, qs_o, ks_o, vs_o, kmean_o):
    pos = pos_ref[...].astype(F32)
    ang_m = pos * fm_ref[...]
    ang_d = pos * fd_ref[...]
    cos_m, sin_m = jnp.cos(ang_m), jnp.sin(ang_m)
    cos_d, sin_d = jnp.cos(ang_d), jnp.sin(ang_d)
    lane = lax.broadcasted_iota(jnp.int32, ang_m.shape, 1)
    low = lane < DIFF_SUB

    def norm_full(x, g):
        ms = jnp.mean(x * x, axis=-1, keepdims=True)
        return x * lax.rsqrt(ms + NORM_EPS) * g

    def norm_halves(x, g):
        xx = x * x
        lo = jnp.sum(jnp.where(low, xx, 0.0), axis=-1, keepdims=True)
        hi = jnp.sum(jnp.where(low, 0.0, xx), axis=-1, keepdims=True)
        ms = jnp.where(low, lo, hi) * (1.0 / DIFF_SUB)
        return x * lax.rsqrt(ms + NORM_EPS) * g

    for h in range(N_HEADS):
        sl = slice(h * HEAD_DIM, (h + 1) * HEAD_DIM)
        qm = _rope(norm_full(qm_i[:, sl], qng_ref[...]), cos_m, sin_m, ROT_HALF, HEAD_DIM)
        km = _rope(norm_full(km_i[:, sl], kng_ref[...]), cos_m, sin_m, ROT_HALF, HEAD_DIM)
        qmf_o[:, sl] = qm
        qm_o[:, sl] = qm.astype(BF16)
        km_o[:, sl] = km.astype(BF16)
        kmean_o[0, h:h + 1, :] = jnp.mean(km, axis=0, keepdims=True)
        qd = _rope(norm_halves(qd_i[:, sl], qdg_ref[...]), cos_d, sin_d, ROT_HALF_DIFF, DIFF_SUB)
        kd = _rope(norm_halves(kd_i[:, sl], kdg_ref[...]), cos_d, sin_d, ROT_HALF_DIFF, DIFF_SUB)
        qd_o[:, sl] = qd.astype(BF16)
        kd_o[:, sl] = kd.astype(BF16)
    vm_o[...] = vm_i[...].astype(BF16)
    vd_o[...] = vd_i[...].astype(BF16)
    qs_o[...] = qs_i[...].astype(BF16)
    ks_o[...] = ks_i[...].astype(BF16)
    vs_o[...] = vs_i[...].astype(BF16)


def _rope_table(half, period):
    lane = jnp.arange(LANES)
    inv_freq = ROPE_THETA ** (-jnp.arange(half, dtype=F32) / half)
    tab = jnp.where((lane % period) < 2 * half, inv_freq[lane % half], 0.0)
    return tab.reshape(1, LANES).astype(F32)


def prep_qkv(proj, positions, qn_m, kn_m, qn_d, kn_d):
    t = proj.shape[0]
    tm = MOBA_BLOCK
    w = MIX_WIDTH
    row = lambda i: (i, 0)
    col_spec = [pl.BlockSpec((tm, w), functools.partial(lambda i, c: (i, c), c=c)) for c in range(9)]
    vec = pl.BlockSpec((1, LANES), lambda i: (0, 0))
    bf = jax.ShapeDtypeStruct((t, w), BF16)
    out_shape = [bf, jax.ShapeDtypeStruct((t, w), F32)] + [bf] * 8 + [
        jax.ShapeDtypeStruct((t // tm, N_HEADS, HEAD_DIM), F32)]
    out_specs = [pl.BlockSpec((tm, w), row)] * 10 + [
        pl.BlockSpec((1, N_HEADS, HEAD_DIM), lambda i: (i, 0, 0))]
    return pl.pallas_call(
        _prep_kernel,
        grid=(t // tm,),
        in_specs=[pl.BlockSpec((tm, 1), row), vec, vec, vec, vec, vec, vec] + col_spec,
        out_specs=out_specs,
        out_shape=out_shape,
        compiler_params=_params("parallel"),
        name="prep_qkv",
    )(positions.reshape(t, 1), _rope_table(ROT_HALF, HEAD_DIM), _rope_table(ROT_HALF_DIFF, DIFF_SUB),
      qn_m.reshape(1, LANES), kn_m.reshape(1, LANES),
      jnp.tile(qn_d, 2).reshape(1, LANES), jnp.tile(kn_d, 2).reshape(1, LANES),
      *([proj] * 9))


def _softmax_step(s, v, m, l, acc):
    m_new = jnp.maximum(m, jnp.max(s, axis=-1, keepdims=True))
    alpha = jnp.exp(m - m_new)
    p = jnp.exp(s - m_new)
    l = alpha * l + jnp.sum(p, axis=-1, keepdims=True)
    acc = alpha * acc + jnp.dot(p.astype(BF16), v, preferred_element_type=F32)
    return m_new, l, acc


def _softmax_first(s, v):
    m = jnp.max(s, axis=-1, keepdims=True)
    p = jnp.exp(s - m)
    return m, jnp.sum(p, axis=-1, keepdims=True), jnp.dot(p.astype(BF16), v, preferred_element_type=F32)


def _moba_kernel(q_ref, qf_ref, kmean_ref, k_ref, v_ref, o_ref):
    blk = MOBA_BLOCK
    qi = pl.program_id(2)
    scale = HEAD_DIM ** -0.5
    q = q_ref[...]

    gate = lax.dot_general(qf_ref[...], kmean_ref[...], _NT, precision=lax.Precision.HIGHEST,
                           preferred_element_type=F32)
    lane = lax.broadcasted_iota(jnp.int32, gate.shape, 1)
    avail = lane < qi
    sel = jnp.zeros(gate.shape, jnp.bool_)
    for _ in range(MOBA_TOPK):
        gv = jnp.where(avail, gate, -jnp.inf)
        best = jnp.max(gv, axis=-1, keepdims=True)
        first = jnp.min(jnp.where(avail & (gv == best), lane, LANES), axis=-1, keepdims=True)
        take = lane == first
        sel = sel | take
        avail = avail & jnp.logical_not(take)
    penalty = jnp.where(sel, 0.0, NEG_INF / scale).astype(BF16)
    q_aug = jnp.concatenate([q, penalty], axis=1)

    row = lax.broadcasted_iota(jnp.int32, (blk, blk), 0)
    col = lax.broadcasted_iota(jnp.int32, (blk, blk), 1)
    start = pl.multiple_of(qi * blk, blk)
    k_own = k_ref[pl.ds(start, blk), :]
    v_own = v_ref[pl.ds(start, blk), :]
    s_own = lax.dot_general(q, k_own, _NT, preferred_element_type=F32) * scale
    s_own = jnp.where(col <= row, s_own, NEG_INF)
    carry = _softmax_first(s_own, v_own)

    def body(n, carry):
        off = pl.multiple_of(n * blk, blk)
        kb = k_ref[pl.ds(off, blk), :]
        vb = v_ref[pl.ds(off, blk), :]
        onehot = (lax.broadcasted_iota(jnp.int32, (blk, LANES), 1) == n).astype(BF16)
        k_aug = jnp.concatenate([kb, onehot], axis=1)
        s = lax.dot_general(q_aug, k_aug, _NT, preferred_element_type=F32) * scale
        return _softmax_step(s, vb, *carry)

    m, l, acc = lax.fori_loop(0, qi, body, carry)
    o_ref[...] = (acc / l).astype(o_ref.dtype)


def moba_attention(q, qf, kmean_pad, k, v, batch, seq):
    t = q.shape[0]
    blk = MOBA_BLOCK
    nq = seq // blk
    qspec = pl.BlockSpec((blk, HEAD_DIM), lambda b, h, i: (b * nq + i, h))
    kvspec = pl.BlockSpec((seq, HEAD_DIM), lambda b, h, i: (b, h))
    return pl.pallas_call(
        _moba_kernel,
        grid=(batch, N_HEADS, nq),
        in_specs=[qspec, qspec,
                  pl.BlockSpec((None, None, LANES, HEAD_DIM), lambda b, h, i: (b, h, 0, 0)),
                  kvspec, kvspec],
        out_specs=qspec,
        out_shape=jax.ShapeDtypeStruct((t, MIX_WIDTH), BF16),
        compiler_params=_params("parallel", "parallel", "arbitrary"),
        name="moba_attention",
    )(q, qf, kmean_pad, k, v)


def _diff_kernel(lq1_ref, lk1_ref, lq2_ref, lk2_ref, g_ref, q_ref, k_ref, v_ref, o_ref, *, tq, lam_init):
    qi = pl.program_id(2)
    scale = DIFF_SUB ** -0.5
    q = q_ref[...]
    lane = lax.broadcasted_iota(jnp.int32, q.shape, 1)
    zero = jnp.zeros_like(q)
    q1 = jnp.where(lane < DIFF_SUB, q, zero)
    q2 = jnp.where(lane < DIFF_SUB, zero, q)

    row = lax.broadcasted_iota(jnp.int32, (tq, tq), 0)
    col = lax.broadcasted_iota(jnp.int32, (tq, tq), 1)
    start = pl.multiple_of(qi * tq, tq)
    k_own = k_ref[pl.ds(start, tq), :]
    v_own = v_ref[pl.ds(start, tq), :]
    causal = col <= row
    s1 = jnp.where(causal, lax.dot_general(q1, k_own, _NT, preferred_element_type=F32) * scale, NEG_INF)
    s2 = jnp.where(causal, lax.dot_general(q2, k_own, _NT, preferred_element_type=F32) * scale, NEG_INF)
    carry = _softmax_first(s1, v_own) + _softmax_first(s2, v_own)

    def body(n, carry):
        off = pl.multiple_of(n * tq, tq)
        kb = k_ref[pl.ds(off, tq), :]
        vb = v_ref[pl.ds(off, tq), :]
        t1 = lax.dot_general(q1, kb, _NT, preferred_element_type=F32) * scale
        t2 = lax.dot_general(q2, kb, _NT, preferred_element_type=F32) * scale
        return _softmax_step(t1, vb, *carry[:3]) + _softmax_step(t2, vb, *carry[3:])

    _, l1, a1, _, l2, a2 = lax.fori_loop(0, qi, body, carry)
    lam = (jnp.exp(jnp.sum(lq1_ref[...] * lk1_ref[...], axis=-1, keepdims=True))
           - jnp.exp(jnp.sum(lq2_ref[...] * lk2_ref[...], axis=-1, keepdims=True)) + lam_init)
    o = a1 / l1 - lam * (a2 / l2)
    ms = jnp.mean(o * o, axis=-1, keepdims=True)
    o_ref[...] = (o * lax.rsqrt(ms + NORM_EPS) * g_ref[...] * (1.0 - lam_init)).astype(o_ref.dtype)


def diff_attention(q, k, v, lq1, lk1, lq2, lk2, subln, lam_init, batch, seq, tq=256):
    t = q.shape[0]
    tq = min(tq, seq)
    nq = seq // tq
    qspec = pl.BlockSpec((tq, HEAD_DIM), lambda b, h, i: (b * nq + i, h))
    kvspec = pl.BlockSpec((seq, HEAD_DIM), lambda b, h, i: (b, h))
    lspec = pl.BlockSpec((1, DIFF_SUB), lambda b, h, i: (0, 0))
    return pl.pallas_call(
        functools.partial(_diff_kernel, tq=tq, lam_init=lam_init),
        grid=(batch, N_HEADS, nq),
        in_specs=[lspec, lspec, lspec, lspec,
                  pl.BlockSpec((1, HEAD_DIM), lambda b, h, i: (0, 0)), qspec, kvspec, kvspec],
        out_specs=qspec,
        out_shape=jax.ShapeDtypeStruct((t, MIX_WIDTH), BF16),
        compiler_params=_params("parallel", "parallel", "arbitrary"),
        name="diff_attention",
    )(lq1.reshape(1, DIFF_SUB), lk1.reshape(1, DIFF_SUB), lq2.reshape(1, DIFF_SUB),
      lk2.reshape(1, DIFF_SUB), subln.reshape(1, HEAD_DIM), q, k, v)


def _stick_kernel(q_ref, k_ref, v_ref, o_ref, *, tq, tk):
    qi = pl.program_id(2)
    scale = HEAD_DIM ** -0.5
    q = q_ref[...]
    upper = (lax.broadcasted_iota(jnp.int32, (tk, tk), 0)
             > lax.broadcasted_iota(jnp.int32, (tk, tk), 1)).astype(BF16)
    q_pos = qi * tq + lax.broadcasted_iota(jnp.int32, (tq, tk), 0)
    col = lax.broadcasted_iota(jnp.int32, (tq, tk), 1)
    n_tiles = (qi + 1) * (tq // tk)

    def body(it, carry):
        later, acc = carry
        n = n_tiles - 1 - it
        off = pl.multiple_of(n * tk, tk)
        kb = k_ref[pl.ds(off, tk), :]
        vb = v_ref[pl.ds(off, tk), :]
        z = lax.dot_general(q, kb, _NT, preferred_element_type=F32) * scale
        strict = (off + col) < q_pos
        soft = jnp.log1p(jnp.exp(-jnp.abs(z)))
        log_beta = jnp.minimum(z, 0.0) - soft
        log_1m = jnp.where(strict, jnp.minimum(-z, 0.0) - soft, 0.0)
        hi = log_1m.astype(BF16)
        lo = (log_1m - hi.astype(F32)).astype(BF16)
        between = (jnp.dot(hi, upper, preferred_element_type=F32)
                   + jnp.dot(lo, upper, preferred_element_type=F32))
        a = jnp.where(strict, jnp.exp(log_beta + between + later), 0.0)
        acc = acc + jnp.dot(a.astype(BF16), vb, preferred_element_type=F32)
        later = later + jnp.sum(log_1m, axis=-1, keepdims=True)
        return later, acc

    init = (jnp.zeros((tq, 1), F32), jnp.zeros((tq, HEAD_DIM), F32))
    _, acc = lax.fori_loop(0, n_tiles, body, init)
    o_ref[...] = acc.astype(o_ref.dtype)


def stick_attention(q, k, v, batch, seq, tq=512, tk=128):
    t = q.shape[0]
    tq, tk = min(tq, seq), min(tk, seq)
    nq = seq // tq
    qspec = pl.BlockSpec((tq, HEAD_DIM), lambda b, h, i: (b * nq + i, h))
    kvspec = pl.BlockSpec((seq, HEAD_DIM), lambda b, h, i: (b, h))
    return pl.pallas_call(
        functools.partial(_stick_kernel, tq=tq, tk=tk),
        grid=(batch, N_HEADS, nq),
        in_specs=[qspec, kvspec, kvspec],
        out_specs=qspec,
        out_shape=jax.ShapeDtypeStruct((t, MIX_WIDTH), BF16),
        compiler_params=_params("parallel", "parallel", "arbitrary"),
        name="stick_attention",
    )(q, k, v)


def _merge_kernel(om_ref, od_ref, os_ref, w_ref, gm_ref, gd_ref, gs_ref, o_ref):
    acc = jax.nn.sigmoid(gm_ref[...]) * jnp.dot(om_ref[...], w_ref[0].astype(BF16),
                                                preferred_element_type=F32)
    acc += jax.nn.sigmoid(gd_ref[...]) * jnp.dot(od_ref[...], w_ref[1].astype(BF16),
                                                 preferred_element_type=F32)
    acc += jax.nn.sigmoid(gs_ref[...]) * jnp.dot(os_ref[...], w_ref[2].astype(BF16),
                                                 preferred_element_type=F32)
    o_ref[...] = acc.astype(o_ref.dtype)


def merge_branches(o_m, o_d, o_s, w_branch, proj, gate_col, tm=512, tn=512):
    t, kdim = o_m.shape
    d = w_branch.shape[2]
    tm, tn = min(tm, t), min(tn, d)
    ospec = pl.BlockSpec((tm, kdim), lambda i, j: (i, 0))
    gspecs = [pl.BlockSpec((tm, tn), functools.partial(lambda i, j, c: (i, c + j), c=(gate_col + c * d) // tn))
              for c in range(3)]
    return pl.pallas_call(
        _merge_kernel,
        grid=(t // tm, d // tn),
        in_specs=[ospec, ospec, ospec, pl.BlockSpec((3, kdim, tn), lambda i, j: (0, 0, j))] + gspecs,
        out_specs=pl.BlockSpec((tm, tn), lambda i, j: (i, j)),
        out_shape=jax.ShapeDtypeStruct((t, d), BF16),
        compiler_params=_params("parallel", "parallel"),
        name="merge_branches",
    )(o_m, o_d, o_s, w_branch, proj, proj, proj)


def _extract_top(x, rounds):
    rows, n = x.shape
    rid = lax.broadcasted_iota(jnp.int32, (rows, n), 0)
    kid = lax.broadcasted_iota(jnp.int32, (rounds, n), 0)

    def body(k, carry):
        x, rank, vals, idxs = carry
        best = jnp.max(x, axis=0, keepdims=True)
        first = jnp.min(jnp.where(x == best, rid, rows), axis=0, keepdims=True)
        hit = rid == first
        rank = jnp.where(hit, k, rank)
        x = jnp.where(hit, -jnp.inf, x)
        vals = jnp.where(kid == k, best, vals)
        idxs = jnp.where(kid == k, first, idxs)
        return x, rank, vals, idxs

    init = (x, jnp.full((rows, n), rounds, jnp.int32), jnp.zeros((rounds, n), F32),
            jnp.zeros((rounds, n), jnp.int32))
    _, rank, vals, idxs = lax.fori_loop(0, rounds, body, init)
    return vals, rank, idxs


def _peer_score_kernel(q_ref, keys_ref, cb_o, eb_o, na_o, wa_o):
    nk, topk = PEER_NKEYS, PEER_TOPK
    s = lax.dot_general(keys_ref[...], q_ref[...], _NT, precision=lax.Precision.HIGHEST,
                        preferred_element_type=F32)
    sa, sb = s[:nk], s[nk:]
    va, rank_a, _ = _extract_top(sa, topk)
    vb, rank_b, _ = _extract_top(sb, topk)
    cand = jnp.concatenate([va[r:r + 1] + vb for r in range(topk)], axis=0)
    top_s, _, top_p = _extract_top(cand, topk)
    top_r = lax.shift_right_logical(top_p, 4)
    rid = lax.broadcasted_iota(jnp.int32, (topk, s.shape[1]), 0)
    count = jnp.zeros((topk, s.shape[1]), jnp.int32)
    for k in range(topk):
        count = count + (rid == top_r[k:k + 1]).astype(jnp.int32)
    z = jnp.sum(jnp.exp(top_s - top_s[0:1]), axis=0, keepdims=True)
    na = jnp.zeros(sa.shape, jnp.int32)
    for r in range(topk):
        na = jnp.where(rank_a == r, count[r:r + 1], na)
    cb_o[...] = rank_b.astype(F32)
    eb_o[...] = jnp.exp(sb - vb[0:1])
    na_o[...] = na.astype(F32)
    wa_o[...] = jnp.exp(sa - va[0:1]) / z


def peer_scores(q, key_a, key_b, tn=256):
    t = q.shape[0]
    tn = min(tn, t)
    half = key_a.shape[-1]
    zeros = jnp.zeros_like(key_a)
    keys = jnp.concatenate([jnp.concatenate([key_a, zeros], axis=-1),
                            jnp.concatenate([zeros, key_b], axis=-1)], axis=1)
    del half
    out = jax.ShapeDtypeStruct((PEER_HEADS, PEER_NKEYS, t), F32)
    ospec = pl.BlockSpec((None, PEER_NKEYS, tn), lambda i, h: (h, 0, i))
    return pl.pallas_call(
        _peer_score_kernel,
        grid=(t // tn, PEER_HEADS),
        in_specs=[pl.BlockSpec((tn, LANES), lambda i, h: (i, h)),
                  pl.BlockSpec((None, 2 * PEER_NKEYS, LANES), lambda i, h: (h, 0, 0))],
        out_specs=[ospec] * 4,
        out_shape=[out] * 4,
        compiler_params=_params("parallel", "parallel"),
        name="peer_scores",
    )(q, keys)


def _peer_dense_kernel(xt_ref, u_ref, vt_ref, cb_ref, eb_ref, na_ref, wa_ref, o_ref, *, te):
    e = pl.program_id(1)

    @pl.when(e == 0)
    def _():
        o_ref[...] = jnp.zeros_like(o_ref)

    hid = jnp.dot(u_ref[...], xt_ref[...], preferred_element_type=F32)
    act = 0.5 * hid * (1.0 + lax.erf(hid * (2.0 ** -0.5)))
    rows = []
    for di in range(te // PEER_NKEYS):
        i = e * (te // PEER_NKEYS) + di
        w = jnp.zeros((PEER_NKEYS, hid.shape[1]), F32)
        for h in range(PEER_HEADS):
            n_row = na_ref[h, pl.ds(i, 1), :]
            w_row = wa_ref[h, pl.ds(i, 1), :]
            w = w + jnp.where(cb_ref[h] < n_row, eb_ref[h] * w_row, 0.0)
        rows.append(w)
    weight = jnp.concatenate(rows, axis=0) if len(rows) > 1 else rows[0]
    coef = (act * weight).astype(BF16)
    o_ref[...] += jnp.dot(vt_ref[...], coef, preferred_element_type=F32)


def peer_dense(xt, u, vt, cb, eb, na, wa, tn=512, te=256):
    d, t = xt.shape
    n_exp = u.shape[0]
    tn, te = min(tn, t), min(te, n_exp)
    sspec = pl.BlockSpec((PEER_HEADS, PEER_NKEYS, tn), lambda i, e: (0, 0, i))
    return pl.pallas_call(
        functools.partial(_peer_dense_kernel, te=te),
        grid=(t // tn, n_exp // te),
        in_specs=[pl.BlockSpec((d, tn), lambda i, e: (0, i)),
                  pl.BlockSpec((te, d), lambda i, e: (e, 0)),
                  pl.BlockSpec((d, te), lambda i, e: (0, e)),
                  sspec, sspec, sspec, sspec],
        out_specs=pl.BlockSpec((d, tn), lambda i, e: (0, i)),
        out_shape=jax.ShapeDtypeStruct((d, t), F32),
        compiler_params=_params("parallel", "arbitrary"),
        name="peer_dense",
    )(xt, u, vt, cb, eb, na, wa)


def kernel(x, p, positions, norm_mix, w_in, qn_moba, kn_moba, qn_diff, kn_diff, lambda_q1, lambda_k1, lambda_q2, lambda_k2, subln_diff, w_branch, w_out, norm_ffn, peer_wq, peer_key_a, peer_key_b, peer_u, peer_v, norm_ple, w_ple_gate, w_ple_proj):
    b, s, d = x.shape
    t = b * s
    depth = w_in.shape[0]
    nb = s // MOBA_BLOCK
    gate_col = 9 * MIX_WIDTH
    h = x.reshape(t, d)
    for i in range(depth):
        n1 = rmsnorm(h, norm_mix[i])
        proj = matmul(n1, w_in[i])
        (qm, qmf, km, vm, qd, kd, vd, qs, ks, vs, kmean) = prep_qkv(
            proj, positions, qn_moba[i], kn_moba[i], qn_diff[i], kn_diff[i])
        kmean = kmean.reshape(b, nb, N_HEADS, HEAD_DIM).transpose(0, 2, 1, 3)
        kmean = jnp.pad(kmean, ((0, 0), (0, 0), (0, LANES - nb), (0, 0)))
        o_m = moba_attention(qm, qmf, kmean, km, vm, b, s)
        lam_init = 0.8 - 0.6 * math.exp(-0.3 * i)
        o_d = diff_attention(qd, kd, vd, lambda_q1[i], lambda_k1[i], lambda_q2[i], lambda_k2[i],
                             subln_diff[i], lam_init, b, s)
        o_s = stick_attention(qs, ks, vs, b, s)
        merged = merge_branches(o_m, o_d, o_s, w_branch[i], proj, gate_col)
        h = matmul(merged, w_out[i], mode="residual", res=h)
        n2 = rmsnorm(h, norm_ffn[i])
        q = matmul(n2, peer_wq[i])
        cb, eb, na, wa = peer_scores(q, peer_key_a[i], peer_key_b[i])
        ffn_t = peer_dense(n2.T, peer_u[i].astype(BF16), peer_v[i].T.astype(BF16), cb, eb, na, wa)
        h = h + ffn_t.T
        n3 = rmsnorm(h, norm_ple[i])
        h = matmul(n3, w_ple_gate[i], mode="ple", res=h, p=p[i].reshape(t, -1), wp=w_ple_proj[i])
    return h.reshape(b, s, d)
```

```python
import functools
import math

import jax
import jax.numpy as jnp
from jax import lax
from jax.experimental import pallas as pl
from jax.experimental.pallas import tpu as pltpu

F32 = jnp.float32
BF16 = jnp.bfloat16

LANES = 128
HEAD_DIM = 128
N_HEADS = 8
MIX_WIDTH = N_HEADS * HEAD_DIM
DIFF_SUB = HEAD_DIM // 2
ROPE_THETA = 500000.0
ROT_HALF = HEAD_DIM // 8
ROT_HALF_DIFF = DIFF_SUB // 8
MOBA_BLOCK = 256
MOBA_TOPK = 3
PEER_HEADS = 8
PEER_NKEYS = 128
PEER_TOPK = 16
NORM_EPS = 1e-6
NEG_INF = -1e30
VMEM_LIMIT = 56 * 1024 * 1024

_NT = (((1,), (1,)), ((), ()))


def _params(*sem):
    return pltpu.CompilerParams(dimension_semantics=sem, vmem_limit_bytes=VMEM_LIMIT)


def _rmsnorm_kernel(x_ref, g_ref, o_ref):
    x = x_ref[...]
    ms = jnp.mean(x * x, axis=-1, keepdims=True)
    o_ref[...] = (x * lax.rsqrt(ms + NORM_EPS) * g_ref[...]).astype(o_ref.dtype)


def rmsnorm(x, gain, tm=256):
    t, d = x.shape
    tm = min(tm, t)
    return pl.pallas_call(
        _rmsnorm_kernel,
        grid=(t // tm,),
        in_specs=[pl.BlockSpec((tm, d), lambda i: (i, 0)),
                  pl.BlockSpec((1, d), lambda i: (0, 0))],
        out_specs=pl.BlockSpec((tm, d), lambda i: (i, 0)),
        out_shape=jax.ShapeDtypeStruct((t, d), BF16),
        compiler_params=_params("parallel"),
        name="rmsnorm",
    )(x, gain.reshape(1, d))


def _mm_kernel(*refs, mode):
    if mode == "plain":
        a_ref, b_ref, o_ref = refs
    elif mode == "residual":
        a_ref, b_ref, r_ref, o_ref = refs
    else:
        a_ref, b_ref, r_ref, p_ref, wp_ref, o_ref = refs
    acc = jnp.dot(a_ref[...], b_ref[...].astype(BF16), preferred_element_type=F32)
    if mode == "plain":
        o_ref[...] = acc.astype(o_ref.dtype)
    elif mode == "residual":
        o_ref[...] = (r_ref[...] + acc).astype(o_ref.dtype)
    else:
        emb = jnp.dot(p_ref[...].astype(BF16), wp_ref[...].astype(BF16),
                      preferred_element_type=F32)
        o_ref[...] = (r_ref[...] + jax.nn.sigmoid(acc) * emb).astype(o_ref.dtype)


def matmul(a, w, layer, *, mode="plain", res=None, p=None, wp=None, out_dtype=F32, tm=2048, tn=None):
    m, kdim = a.shape
    n = w.shape[2]
    if tn is None:
        tn = 512 if mode == "plain" else 256
    tm, tn = min(tm, m), min(tn, n)
    in_specs = [pl.BlockSpec((tm, kdim), lambda i, j: (i, 0), pipeline_mode=pl.Buffered(1)),
                pl.BlockSpec((None, kdim, tn), lambda i, j: (layer, 0, j))]
    args = [a, w]
    if mode in ("residual", "ple"):
        in_specs.append(pl.BlockSpec((tm, tn), lambda i, j: (i, j)))
        args.append(res)
    if mode == "ple":
        pd = p.shape[2]
        in_specs += [pl.BlockSpec((None, tm, pd), lambda i, j: (layer, i, 0)),
                     pl.BlockSpec((None, pd, tn), lambda i, j: (layer, 0, j))]
        args += [p, wp]
    return pl.pallas_call(
        functools.partial(_mm_kernel, mode=mode),
        grid=(m // tm, n // tn),
        in_specs=in_specs,
        out_specs=pl.BlockSpec((tm, tn), lambda i, j: (i, j)),
        out_shape=jax.ShapeDtypeStruct((m, n), out_dtype),
        compiler_params=_params("parallel", "arbitrary"),
        name="matmul_" + mode,
    )(*args)


def _rope(x, cos, sin, half, period):
    lane = lax.broadcasted_iota(jnp.int32, x.shape, 1)
    first = (lane & (period - 1)) < half
    partner = jnp.where(first, pltpu.roll(x, LANES - half, 1), pltpu.roll(x, half, 1))
    return x * cos + partner * jnp.where(first, -sin, sin)


def _prep_kernel(pos_ref, fm_ref, fd_ref, qng_ref, kng_ref, qdg_ref, kdg_ref,
                 qm_i, km_i, vm_i, qd_i, kd_i, vd_i, qs_i, ks_i, vs_i,
                 qm_o, qmf_o, km_o, vm_o, qd_o, kd_o, vd_o, qs_o, ks_o, vs_o, kmean_o):
    pos = pos_ref[...].astype(F32)
    ang_m = pos * fm_ref[...]
    ang_d = pos * fd_ref[...]
    cos_m, sin_m = jnp.cos(ang_m), jnp.sin(ang_m)
    cos_d, sin_d = jnp.cos(ang_d), jnp.sin(ang_d)
    lane = lax.broadcasted_iota(jnp.int32, ang_m.shape, 1)
    low = lane < DIFF_SUB

    def norm_full(x, g):
        ms = jnp.mean(x * x, axis=-1, keepdims=True)
        return x * lax.rsqrt(ms + NORM_EPS) * g

    def norm_halves(x, g):
        xx = x * x
        lo = jnp.sum(jnp.where(low, xx, 0.0), axis=-1, keepdims=True)
        hi = jnp.sum(jnp.where(low, 0.0, xx), axis=-1, keepdims=True)
        ms = jnp.where(low, lo, hi) * (1.0 / DIFF_SUB)
        return x * lax.rsqrt(ms + NORM_EPS) * g

    for h in range(N_HEADS):
        sl = slice(h * HEAD_DIM, (h + 1) * HEAD_DIM)
        qm = _rope(norm_full(qm_i[:, sl], qng_ref[...]), cos_m, sin_m, ROT_HALF, HEAD_DIM)
        km = _rope(norm_full(km_i[:, sl], kng_ref[...]), cos_m, sin_m, ROT_HALF, HEAD_DIM)
        qmf_o[:, sl] = qm
        qm_o[:, sl] = qm.astype(BF16)
        km_o[:, sl] = km.astype(BF16)
        kmean_o[0, h:h + 1, :] = jnp.mean(km, axis=0, keepdims=True)
        qd = _rope(norm_halves(qd_i[:, sl], qdg_ref[...]), cos_d, sin_d, ROT_HALF_DIFF, DIFF_SUB)
        kd = _rope(norm_halves(kd_i[:, sl], kdg_ref[...]), cos_d, sin_d, ROT_HALF_DIFF, DIFF_SUB)
        qd_o[:, sl] = qd.astype(BF16)
        kd_o[:, sl] = kd.astype(BF16)
    vm_o[...] = vm_i[...].astype(BF16)
    vd_o[...] = vd_i[...].astype(BF16)
    qs_o[...] = qs_i[...].astype(BF16)
    ks_o[...] = ks_i[...].astype(BF16)
    vs_o[...] = vs_i[...].astype(BF16)


def _rope_table(half, period):
    lane = jnp.arange(LANES)
    inv_freq = ROPE_THETA ** (-jnp.arange(half, dtype=F32) / half)
    tab = jnp.where((lane % period) < 2 * half, inv_freq[lane % half], 0.0)
    return tab.reshape(1, LANES).astype(F32)


def prep_qkv(proj, positions, qn_m, kn_m, qn_d, kn_d):
    t = proj.shape[0]
    tm = MOBA_BLOCK
    w = MIX_WIDTH
    row = lambda i: (i, 0)
    col_spec = [pl.BlockSpec((tm, w), functools.partial(lambda i, c: (i, c), c=c)) for c in range(9)]
    vec = pl.BlockSpec((1, LANES), lambda i: (0, 0))
    bf = jax.ShapeDtypeStruct((t, w), BF16)
    out_shape = [bf, jax.ShapeDtypeStruct((t, w), F32)] + [bf] * 8 + [
        jax.ShapeDtypeStruct((t // tm, N_HEADS, HEAD_DIM), F32)]
    out_specs = [pl.BlockSpec((tm, w), row)] * 10 + [
        pl.BlockSpec((1, N_HEADS, HEAD_DIM), lambda i: (i, 0, 0))]
    return pl.pallas_call(
        _prep_kernel,
        grid=(t // tm,),
        in_specs=[pl.BlockSpec((tm, 1), row), vec, vec, vec, vec, vec, vec] + col_spec,
        out_specs=out_specs,
        out_shape=out_shape,
        compiler_params=_params("parallel"),
        name="prep_qkv",
    )(positions.reshape(t, 1), _rope_table(ROT_HALF, HEAD_DIM), _rope_table(ROT_HALF_DIFF, DIFF_SUB),
      qn_m.reshape(1, LANES), kn_m.reshape(1, LANES),
      jnp.tile(qn_d, 2).reshape(1, LANES), jnp.tile(kn_d, 2).reshape(1, LANES),
      *([proj] * 9))


def _softmax_step(s, v, m, l, acc):
    m_new = jnp.maximum(m, jnp.max(s, axis=-1, keepdims=True))
    alpha = jnp.exp(m - m_new)
    p = jnp.exp(s - m_new)
    l = alpha * l + jnp.sum(p, axis=-1, keepdims=True)
    acc = alpha * acc + jnp.dot(p.astype(BF16), v, preferred_element_type=F32)
    return m_new, l, acc


def _softmax_first(s, v):
    m = jnp.max(s, axis=-1, keepdims=True)
    p = jnp.exp(s - m)
    return m, jnp.sum(p, axis=-1, keepdims=True), jnp.dot(p.astype(BF16), v, preferred_element_type=F32)


def _moba_kernel(q_ref, qf_ref, kmean_ref, k_ref, v_ref, o_ref):
    blk = MOBA_BLOCK
    qi = pl.program_id(2)
    scale = HEAD_DIM ** -0.5
    q = q_ref[...]

    gate = lax.dot_general(qf_ref[...], kmean_ref[...], _NT, precision=lax.Precision.HIGHEST,
                           preferred_element_type=F32)
    lane = lax.broadcasted_iota(jnp.int32, gate.shape, 1)
    avail = lane < qi
    sel = jnp.zeros(gate.shape, jnp.bool_)
    for _ in range(MOBA_TOPK):
        gv = jnp.where(avail, gate, -jnp.inf)
        best = jnp.max(gv, axis=-1, keepdims=True)
        first = jnp.min(jnp.where(avail & (gv == best), lane, LANES), axis=-1, keepdims=True)
        take = lane == first
        sel = sel | take
        avail = avail & jnp.logical_not(take)
    penalty = jnp.where(sel, 0.0, NEG_INF / scale).astype(BF16)
    q_aug = jnp.concatenate([q, penalty], axis=1)

    row = lax.broadcasted_iota(jnp.int32, (blk, blk), 0)
    col = lax.broadcasted_iota(jnp.int32, (blk, blk), 1)
    start = pl.multiple_of(qi * blk, blk)
    k_own = k_ref[pl.ds(start, blk), :]
    v_own = v_ref[pl.ds(start, blk), :]
    s_own = lax.dot_general(q, k_own, _NT, preferred_element_type=F32) * scale
    s_own = jnp.where(col <= row, s_own, NEG_INF)
    carry = _softmax_first(s_own, v_own)

    def body(n, carry):
        off = pl.multiple_of(n * blk, blk)
        kb = k_ref[pl.ds(off, blk), :]
        vb = v_ref[pl.ds(off, blk), :]
        onehot = (lax.broadcasted_iota(jnp.int32, (blk, LANES), 1) == n).astype(BF16)
        k_aug = jnp.concatenate([kb, onehot], axis=1)
        s = lax.dot_general(q_aug, k_aug, _NT, preferred_element_type=F32) * scale
        return _softmax_step(s, vb, *carry)

    m, l, acc = lax.fori_loop(0, qi, body, carry)
    o_ref[...] = (acc / l).astype(o_ref.dtype)


def moba_attention(q, qf, kmean_pad, k, v, batch, seq):
    t = q.shape[0]
    blk = MOBA_BLOCK
    nq = seq // blk
    qspec = pl.BlockSpec((blk, HEAD_DIM), lambda b, h, i: (b * nq + i, h))
    kvspec = pl.BlockSpec((seq, HEAD_DIM), lambda b, h, i: (b, h))
    return pl.pallas_call(
        _moba_kernel,
        grid=(batch, N_HEADS, nq),
        in_specs=[qspec, qspec,
                  pl.BlockSpec((None, None, LANES, HEAD_DIM), lambda b, h, i: (b, h, 0, 0)),
                  kvspec, kvspec],
        out_specs=qspec,
        out_shape=jax.ShapeDtypeStruct((t, MIX_WIDTH), BF16),
        compiler_params=_params("parallel", "parallel", "arbitrary"),
        name="moba_attention",
    )(q, qf, kmean_pad, k, v)


def _diff_kernel(lq1_ref, lk1_ref, lq2_ref, lk2_ref, g_ref, q_ref, k_ref, v_ref, o_ref, *, tq, lam_init):
    qi = pl.program_id(2)
    scale = DIFF_SUB ** -0.5
    q = q_ref[...]
    lane = lax.broadcasted_iota(jnp.int32, q.shape, 1)
    zero = jnp.zeros_like(q)
    q1 = jnp.where(lane < DIFF_SUB, q, zero)
    q2 = jnp.where(lane < DIFF_SUB, zero, q)

    row = lax.broadcasted_iota(jnp.int32, (tq, tq), 0)
    col = lax.broadcasted_iota(jnp.int32, (tq, tq), 1)
    start = pl.multiple_of(qi * tq, tq)
    k_own = k_ref[pl.ds(start, tq), :]
    v_own = v_ref[pl.ds(start, tq), :]
    causal = col <= row
    s1 = jnp.where(causal, lax.dot_general(q1, k_own, _NT, preferred_element_type=F32) * scale, NEG_INF)
    s2 = jnp.where(causal, lax.dot_general(q2, k_own, _NT, preferred_element_type=F32) * scale, NEG_INF)
    carry = _softmax_first(s1, v_own) + _softmax_first(s2, v_own)

    def body(n, carry):
        off = pl.multiple_of(n * tq, tq)
        kb = k_ref[pl.ds(off, tq), :]
        vb = v_ref[pl.ds(off, tq), :]
        t1 = lax.dot_general(q1, kb, _NT, preferred_element_type=F32) * scale
        t2 = lax.dot_general(q2, kb, _NT, preferred_element_type=F32) * scale
        return _softmax_step(t1, vb, *carry[:3]) + _softmax_step(t2, vb, *carry[3:])

    _, l1, a1, _, l2, a2 = lax.fori_loop(0, qi, body, carry)
    lam = (jnp.exp(jnp.sum(lq1_ref[...] * lk1_ref[...], axis=-1, keepdims=True))
           - jnp.exp(jnp.sum(lq2_ref[...] * lk2_ref[...], axis=-1, keepdims=True)) + lam_init)
    o = a1 / l1 - lam * (a2 / l2)
    ms = jnp.mean(o * o, axis=-1, keepdims=True)
    o_ref[...] = (o * lax.rsqrt(ms + NORM_EPS) * g_ref[...] * (1.0 - lam_init)).astype(o_ref.dtype)


def diff_attention(q, k, v, lq1, lk1, lq2, lk2, subln, lam_init, batch, seq, tq=256):
    t = q.shape[0]
    tq = min(tq, seq)
    nq = seq // tq
    qspec = pl.BlockSpec((tq, HEAD_DIM), lambda b, h, i: (b * nq + i, h))
    kvspec = pl.BlockSpec((seq, HEAD_DIM), lambda b, h, i: (b, h))
    lspec = pl.BlockSpec((1, DIFF_SUB), lambda b, h, i: (0, 0))
    return pl.pallas_call(
        functools.partial(_diff_kernel, tq=tq, lam_init=lam_init),
        grid=(batch, N_HEADS, nq),
        in_specs=[lspec, lspec, lspec, lspec,
                  pl.BlockSpec((1, HEAD_DIM), lambda b, h, i: (0, 0)), qspec, kvspec, kvspec],
        out_specs=qspec,
        out_shape=jax.ShapeDtypeStruct((t, MIX_WIDTH), BF16),
        compiler_params=_params("parallel", "parallel", "arbitrary"),
        name="diff_attention",
    )(lq1.reshape(1, DIFF_SUB), lk1.reshape(1, DIFF_SUB), lq2.reshape(1, DIFF_SUB),
      lk2.reshape(1, DIFF_SUB), subln.reshape(1, HEAD_DIM), q, k, v)


def _stick_kernel(q_ref, k_ref, v_ref, o_ref, *, tq, tk):
    qi = pl.program_id(2)
    scale = HEAD_DIM ** -0.5
    q = q_ref[...]
    upper = (lax.broadcasted_iota(jnp.int32, (tk, tk), 0)
             > lax.broadcasted_iota(jnp.int32, (tk, tk), 1)).astype(BF16)
    q_pos = qi * tq + lax.broadcasted_iota(jnp.int32, (tq, tk), 0)
    col = lax.broadcasted_iota(jnp.int32, (tq, tk), 1)
    n_tiles = (qi + 1) * (tq // tk)

    def body(it, carry):
        later, acc = carry
        n = n_tiles - 1 - it
        off = pl.multiple_of(n * tk, tk)
        kb = k_ref[pl.ds(off, tk), :]
        vb = v_ref[pl.ds(off, tk), :]
        z = lax.dot_general(q, kb, _NT, preferred_element_type=F32) * scale
        strict = (off + col) < q_pos
        soft = jnp.log1p(jnp.exp(-jnp.abs(z)))
        log_beta = jnp.minimum(z, 0.0) - soft
        log_1m = jnp.where(strict, jnp.minimum(-z, 0.0) - soft, 0.0)
        hi = log_1m.astype(BF16)
        lo = (log_1m - hi.astype(F32)).astype(BF16)
        between = (jnp.dot(hi, upper, preferred_element_type=F32)
                   + jnp.dot(lo, upper, preferred_element_type=F32))
        a = jnp.where(strict, jnp.exp(log_beta + between + later), 0.0)
        acc = acc + jnp.dot(a.astype(BF16), vb, preferred_element_type=F32)
        later = later + jnp.sum(log_1m, axis=-1, keepdims=True)
        return later, acc

    init = (jnp.zeros((tq, 1), F32), jnp.zeros((tq, HEAD_DIM), F32))
    _, acc = lax.fori_loop(0, n_tiles, body, init)
    o_ref[...] = acc.astype(o_ref.dtype)


def stick_attention(q, k, v, batch, seq, tq=512, tk=128):
    t = q.shape[0]
    tq, tk = min(tq, seq), min(tk, seq)
    nq = seq // tq
    qspec = pl.BlockSpec((tq, HEAD_DIM), lambda b, h, i: (b * nq + i, h))
    kvspec = pl.BlockSpec((seq, HEAD_DIM), lambda b, h, i: (b, h))
    return pl.pallas_call(
        functools.partial(_stick_kernel, tq=tq, tk=tk),
        grid=(batch, N_HEADS, nq),
        in_specs=[qspec, kvspec, kvspec],
        out_specs=qspec,
        out_shape=jax.ShapeDtypeStruct((t, MIX_WIDTH), BF16),
        compiler_params=_params("parallel", "parallel", "arbitrary"),
        name="stick_attention",
    )(q, k, v)


def _merge_kernel(om_ref, od_ref, os_ref, w_ref, gm_ref, gd_ref, gs_ref, o_ref):
    acc = jax.nn.sigmoid(gm_ref[...]) * jnp.dot(om_ref[...], w_ref[0].astype(BF16),
                                                preferred_element_type=F32)
    acc += jax.nn.sigmoid(gd_ref[...]) * jnp.dot(od_ref[...], w_ref[1].astype(BF16),
                                                 preferred_element_type=F32)
    acc += jax.nn.sigmoid(gs_ref[...]) * jnp.dot(os_ref[...], w_ref[2].astype(BF16),
                                                 preferred_element_type=F32)
    o_ref[...] = acc.astype(o_ref.dtype)


def merge_branches(o_m, o_d, o_s, w_branch, layer, proj, gate_col, tm=1024, tn=512):
    t, kdim = o_m.shape
    d = w_branch.shape[3]
    tm, tn = min(tm, t), min(tn, d)
    ospec = pl.BlockSpec((tm, kdim), lambda i, j: (i, 0))
    gspecs = [pl.BlockSpec((tm, tn), functools.partial(lambda i, j, c: (i, c + j), c=(gate_col + c * d) // tn))
              for c in range(3)]
    return pl.pallas_call(
        _merge_kernel,
        grid=(t // tm, d // tn),
        in_specs=[ospec, ospec, ospec,
                  pl.BlockSpec((None, 3, kdim, tn), lambda i, j: (layer, 0, 0, j))] + gspecs,
        out_specs=pl.BlockSpec((tm, tn), lambda i, j: (i, j)),
        out_shape=jax.ShapeDtypeStruct((t, d), BF16),
        compiler_params=_params("parallel", "parallel"),
        name="merge_branches",
    )(o_m, o_d, o_s, w_branch, proj, proj, proj)


def _extract_top(x, rounds):
    rows, n = x.shape
    rid = lax.broadcasted_iota(jnp.int32, (rows, n), 0)
    kid = lax.broadcasted_iota(jnp.int32, (rounds, n), 0)

    def body(k, carry):
        x, rank, vals, idxs = carry
        best = jnp.max(x, axis=0, keepdims=True)
        first = jnp.min(jnp.where(x == best, rid, rows), axis=0, keepdims=True)
        hit = rid == first
        rank = jnp.where(hit, k, rank)
        x = jnp.where(hit, -jnp.inf, x)
        vals = jnp.where(kid == k, best, vals)
        idxs = jnp.where(kid == k, first, idxs)
        return x, rank, vals, idxs

    init = (x, jnp.full((rows, n), rounds, jnp.int32), jnp.zeros((rounds, n), F32),
            jnp.zeros((rounds, n), jnp.int32))
    _, rank, vals, idxs = lax.fori_loop(0, rounds, body, init)
    return vals, rank, idxs


def _peer_score_kernel(q_ref, keys_ref, cb_o, eb_o, na_o, wa_o):
    nk, topk = PEER_NKEYS, PEER_TOPK
    s = lax.dot_general(keys_ref[...], q_ref[...], _NT, precision=lax.Precision.HIGHEST,
                        preferred_element_type=F32)
    sa, sb = s[:nk], s[nk:]
    va, rank_a, _ = _extract_top(sa, topk)
    vb, rank_b, _ = _extract_top(sb, topk)
    cand = jnp.concatenate([va[r:r + 1] + vb for r in range(topk)], axis=0)
    top_s, _, top_p = _extract_top(cand, topk)
    top_r = lax.shift_right_logical(top_p, 4)
    rid = lax.broadcasted_iota(jnp.int32, (topk, s.shape[1]), 0)
    count = jnp.zeros((topk, s.shape[1]), jnp.int32)
    for k in range(topk):
        count = count + (rid == top_r[k:k + 1]).astype(jnp.int32)
    z = jnp.sum(jnp.exp(top_s - top_s[0:1]), axis=0, keepdims=True)
    na = jnp.zeros(sa.shape, jnp.int32)
    for r in range(topk):
        na = jnp.where(rank_a == r, count[r:r + 1], na)
    cb_o[...] = rank_b.astype(F32)
    eb_o[...] = jnp.exp(sb - vb[0:1])
    na_o[...] = na.astype(F32)
    wa_o[...] = jnp.exp(sa - va[0:1]) / z


def peer_scores(q, key_a, key_b, tn=256):
    t = q.shape[0]
    tn = min(tn, t)
    half = key_a.shape[-1]
    zeros = jnp.zeros_like(key_a)
    keys = jnp.concatenate([jnp.concatenate([key_a, zeros], axis=-1),
                            jnp.concatenate([zeros, key_b], axis=-1)], axis=1)
    del half
    out = jax.ShapeDtypeStruct((PEER_HEADS, PEER_NKEYS, t), F32)
    ospec = pl.BlockSpec((None, PEER_NKEYS, tn), lambda i, h: (h, 0, i))
    return pl.pallas_call(
        _peer_score_kernel,
        grid=(t // tn, PEER_HEADS),
        in_specs=[pl.BlockSpec((tn, LANES), lambda i, h: (i, h)),
                  pl.BlockSpec((None, 2 * PEER_NKEYS, LANES), lambda i, h: (h, 0, 0))],
        out_specs=[ospec] * 4,
        out_shape=[out] * 4,
        compiler_params=_params("parallel", "parallel"),
        name="peer_scores",
    )(q, keys)


def _peer_dense_kernel(xt_ref, u_ref, vt_ref, cb_ref, eb_ref, na_ref, wa_ref, o_ref,
                       coef_even, coef_odd, *, te):
    e = pl.program_id(1)
    last = pl.num_programs(1) - 2
    tn = o_ref.shape[1]
    half = tn // 2

    @pl.when(e == 0)
    def _():
        o_ref[...] = jnp.zeros_like(o_ref)
        coef_odd[...] = jnp.zeros_like(coef_odd)

    def gate_into_coef(hid, lane0, coef_out):
        chunk = 16
        for di in range(te // PEER_NKEYS):
            i = jnp.minimum(e, last) * (te // PEER_NKEYS) + di
            for sub in range(0, hid.shape[1], LANES):
                lanes = slice(lane0 + sub, lane0 + sub + LANES)
                n_rows = [na_ref[i, h:h + 1, lanes] for h in range(PEER_HEADS)]
                w_rows = [wa_ref[i, h:h + 1, lanes] for h in range(PEER_HEADS)]
                for r in range(0, PEER_NKEYS, chunk):
                    rs = slice(r, r + chunk)
                    w = None
                    for h in range(PEER_HEADS):
                        term = jnp.where(cb_ref[h, rs, lanes] < n_rows[h],
                                         eb_ref[h, rs, lanes] * w_rows[h], 0.0)
                        w = term if w is None else w + term
                    row0 = di * PEER_NKEYS + r
                    x = hid[row0:row0 + chunk, sub:sub + LANES]
                    act = 0.5 * x * (1.0 + lax.erf(x * (2.0 ** -0.5)))
                    coef_out[row0:row0 + chunk, lanes] = (act * w).astype(BF16)

    def step(coef_out, coef_in):
        lo, hi = slice(0, half), slice(half, tn)
        hid_lo = jnp.dot(u_ref[...], xt_ref[:, lo], preferred_element_type=F32)
        o_ref[:, lo] += jnp.dot(vt_ref[...], coef_in[:, lo], preferred_element_type=F32)
        hid_hi = jnp.dot(u_ref[...], xt_ref[:, hi], preferred_element_type=F32)
        gate_into_coef(hid_lo, 0, coef_out)
        o_ref[:, hi] += jnp.dot(vt_ref[...], coef_in[:, hi], preferred_element_type=F32)
        gate_into_coef(hid_hi, half, coef_out)

    @pl.when((e & 1) == 0)
    def _():
        step(coef_even, coef_odd)

    @pl.when((e & 1) == 1)
    def _():
        step(coef_odd, coef_even)


def peer_dense(xt, u, vt, cb, eb, na, wa, tn=512, te=256):
    d, t = xt.shape
    n_exp = u.shape[0]
    tn, te = min(tn, t), min(te, n_exp)
    n_e = n_exp // te
    sspec = pl.BlockSpec((PEER_HEADS, PEER_NKEYS, tn), lambda i, e: (0, 0, i))
    aspec = pl.BlockSpec((PEER_NKEYS, PEER_HEADS, tn), lambda i, e: (0, 0, i))
    na, wa = na.transpose(1, 0, 2), wa.transpose(1, 0, 2)
    return pl.pallas_call(
        functools.partial(_peer_dense_kernel, te=te),
        grid=(t // tn, n_e + 1),
        in_specs=[pl.BlockSpec((d, tn), lambda i, e: (0, i)),
                  pl.BlockSpec((te, d), lambda i, e: (jnp.minimum(e, n_e - 1), 0)),
                  pl.BlockSpec((d, te), lambda i, e: (0, jnp.maximum(e - 1, 0))),
                  sspec, sspec, aspec, aspec],
        out_specs=pl.BlockSpec((d, tn), lambda i, e: (0, i)),
        out_shape=jax.ShapeDtypeStruct((d, t), F32),
        scratch_shapes=[pltpu.VMEM((te, tn), BF16), pltpu.VMEM((te, tn), BF16)],
        compiler_params=_params("parallel", "arbitrary"),
        name="peer_dense",
    )(xt, u, vt, cb, eb, na, wa)


def kernel(x, p, positions, norm_mix, w_in, qn_moba, kn_moba, qn_diff, kn_diff, lambda_q1, lambda_k1, lambda_q2, lambda_k2, subln_diff, w_branch, w_out, norm_ffn, peer_wq, peer_key_a, peer_key_b, peer_u, peer_v, norm_ple, w_ple_gate, w_ple_proj):
    b, s, d = x.shape
    t = b * s
    depth = w_in.shape[0]
    nb = s // MOBA_BLOCK
    gate_col = 9 * MIX_WIDTH
    h = x.reshape(t, d)
    p_flat = p.reshape(depth, t, -1)
    for i in range(depth):
        n1 = rmsnorm(h, norm_mix[i])
        proj = matmul(n1, w_in, i)
        (qm, qmf, km, vm, qd, kd, vd, qs, ks, vs, kmean) = prep_qkv(
            proj, positions, qn_moba[i], kn_moba[i], qn_diff[i], kn_diff[i])
        kmean = kmean.reshape(b, nb, N_HEADS, HEAD_DIM).transpose(0, 2, 1, 3)
        kmean = jnp.pad(kmean, ((0, 0), (0, 0), (0, LANES - nb), (0, 0)))
        o_m = moba_attention(qm, qmf, kmean, km, vm, b, s)
        lam_init = 0.8 - 0.6 * math.exp(-0.3 * i)
        o_d = diff_attention(qd, kd, vd, lambda_q1[i], lambda_k1[i], lambda_q2[i], lambda_k2[i],
                             subln_diff[i], lam_init, b, s)
        o_s = stick_attention(qs, ks, vs, b, s)
        merged = merge_branches(o_m, o_d, o_s, w_branch, i, proj, gate_col)
        h = matmul(merged, w_out, i, mode="residual", res=h)
        n2 = rmsnorm(h, norm_ffn[i])
        q = matmul(n2, peer_wq, i)
        cb, eb, na, wa = peer_scores(q, peer_key_a[i], peer_key_b[i])
        ffn_t = peer_dense(n2.T, peer_u[i].astype(BF16), peer_v[i].T.astype(BF16), cb, eb, na, wa)
        h = h + ffn_t.T
        n3 = rmsnorm(h, norm_ple[i])
        h = matmul(n3, w_ple_gate, i, mode="ple", res=h, p=p_flat, wp=w_ple_proj)
    return h.reshape(b, s, d)
```

```python
import functools
import math

import jax
import jax.numpy as jnp
from jax import lax
from jax.experimental import pallas as pl
from jax.experimental.pallas import tpu as pltpu

F32 = jnp.float32
BF16 = jnp.bfloat16

LANES = 128
HEAD_DIM = 128
N_HEADS = 8
MIX_WIDTH = N_HEADS * HEAD_DIM
DIFF_SUB = HEAD_DIM // 2
ROPE_THETA = 500000.0
ROT_HALF = HEAD_DIM // 8
ROT_HALF_DIFF = DIFF_SUB // 8
MOBA_BLOCK = 256
MOBA_TOPK = 3
PEER_HEADS = 8
PEER_NKEYS = 128
PEER_TOPK = 16
NORM_EPS = 1e-6
NEG_INF = -1e30
VMEM_LIMIT = 56 * 1024 * 1024

_NT = (((1,), (1,)), ((), ()))


def _params(*sem):
    return pltpu.CompilerParams(dimension_semantics=sem, vmem_limit_bytes=VMEM_LIMIT)


def _rmsnorm_kernel(x_ref, g_ref, o_ref):
    x = x_ref[...]
    ms = jnp.mean(x * x, axis=-1, keepdims=True)
    o_ref[...] = (x * lax.rsqrt(ms + NORM_EPS) * g_ref[...]).astype(o_ref.dtype)


def rmsnorm(x, gain, tm=256):
    t, d = x.shape
    tm = min(tm, t)
    return pl.pallas_call(
        _rmsnorm_kernel,
        grid=(t // tm,),
        in_specs=[pl.BlockSpec((tm, d), lambda i: (i, 0)),
                  pl.BlockSpec((1, d), lambda i: (0, 0))],
        out_specs=pl.BlockSpec((tm, d), lambda i: (i, 0)),
        out_shape=jax.ShapeDtypeStruct((t, d), BF16),
        compiler_params=_params("parallel"),
        name="rmsnorm",
    )(x, gain.reshape(1, d))


def _mm_kernel(*refs, mode):
    if mode == "plain":
        a_ref, b_ref, o_ref = refs
    elif mode == "residual":
        a_ref, b_ref, r_ref, o_ref = refs
    else:
        a_ref, b_ref, r_ref, p_ref, wp_ref, o_ref = refs
    acc = jnp.dot(a_ref[...], b_ref[...].astype(BF16), preferred_element_type=F32)
    if mode == "plain":
        o_ref[...] = acc.astype(o_ref.dtype)
    elif mode == "residual":
        o_ref[...] = (r_ref[...] + acc).astype(o_ref.dtype)
    else:
        emb = jnp.dot(p_ref[...].astype(BF16), wp_ref[...].astype(BF16),
                      preferred_element_type=F32)
        o_ref[...] = (r_ref[...] + jax.nn.sigmoid(acc) * emb).astype(o_ref.dtype)


def matmul(a, w, layer, *, mode="plain", res=None, p=None, wp=None, out_dtype=F32, tm=2048, tn=None):
    m, kdim = a.shape
    n = w.shape[2]
    if tn is None:
        tn = 512 if mode == "plain" else 256
    tm, tn = min(tm, m), min(tn, n)
    in_specs = [pl.BlockSpec((tm, kdim), lambda i, j: (i, 0), pipeline_mode=pl.Buffered(1)),
                pl.BlockSpec((None, kdim, tn), lambda i, j: (layer, 0, j))]
    args = [a, w]
    if mode in ("residual", "ple"):
        in_specs.append(pl.BlockSpec((tm, tn), lambda i, j: (i, j)))
        args.append(res)
    if mode == "ple":
        pd = p.shape[2]
        in_specs += [pl.BlockSpec((None, tm, pd), lambda i, j: (layer, i, 0)),
                     pl.BlockSpec((None, pd, tn), lambda i, j: (layer, 0, j))]
        args += [p, wp]
    return pl.pallas_call(
        functools.partial(_mm_kernel, mode=mode),
        grid=(m // tm, n // tn),
        in_specs=in_specs,
        out_specs=pl.BlockSpec((tm, tn), lambda i, j: (i, j)),
        out_shape=jax.ShapeDtypeStruct((m, n), out_dtype),
        compiler_params=_params("parallel", "arbitrary"),
        name="matmul_" + mode,
    )(*args)


def _rope(x, cos, sin, half, period):
    lane = lax.broadcasted_iota(jnp.int32, x.shape, 1)
    first = (lane & (period - 1)) < half
    partner = jnp.where(first, pltpu.roll(x, LANES - half, 1), pltpu.roll(x, half, 1))
    return x * cos + partner * jnp.where(first, -sin, sin)


def _prep_kernel(pos_ref, fm_ref, fd_ref, qng_ref, kng_ref, qdg_ref, kdg_ref,
                 qm_i, km_i, vm_i, qd_i, kd_i, vd_i, qs_i, ks_i, vs_i,
                 qm_o, qmf_o, km_o, vm_o, qd_o, kd_o, vd_o, qs_o, ks_o, vs_o, kmean_o):
    pos = pos_ref[...].astype(F32)
    ang_m = pos * fm_ref[...]
    ang_d = pos * fd_ref[...]
    cos_m, sin_m = jnp.cos(ang_m), jnp.sin(ang_m)
    cos_d, sin_d = jnp.cos(ang_d), jnp.sin(ang_d)
    lane = lax.broadcasted_iota(jnp.int32, ang_m.shape, 1)
    low = lane < DIFF_SUB

    def norm_full(x, g):
        ms = jnp.mean(x * x, axis=-1, keepdims=True)
        return x * lax.rsqrt(ms + NORM_EPS) * g

    def norm_halves(x, g):
        xx = x * x
        lo = jnp.sum(jnp.where(low, xx, 0.0), axis=-1, keepdims=True)
        hi = jnp.sum(jnp.where(low, 0.0, xx), axis=-1, keepdims=True)
        ms = jnp.where(low, lo, hi) * (1.0 / DIFF_SUB)
        return x * lax.rsqrt(ms + NORM_EPS) * g

    for h in range(N_HEADS):
        sl = slice(h * HEAD_DIM, (h + 1) * HEAD_DIM)
        qm = _rope(norm_full(qm_i[:, sl], qng_ref[...]), cos_m, sin_m, ROT_HALF, HEAD_DIM)
        km = _rope(norm_full(km_i[:, sl], kng_ref[...]), cos_m, sin_m, ROT_HALF, HEAD_DIM)
        qmf_o[:, sl] = qm
        qm_o[:, sl] = qm.astype(BF16)
        km_o[:, sl] = km.astype(BF16)
        kmean_o[0, h:h + 1, :] = jnp.mean(km, axis=0, keepdims=True)
        qd = _rope(norm_halves(qd_i[:, sl], qdg_ref[...]), cos_d, sin_d, ROT_HALF_DIFF, DIFF_SUB)
        kd = _rope(norm_halves(kd_i[:, sl], kdg_ref[...]), cos_d, sin_d, ROT_HALF_DIFF, DIFF_SUB)
        qd_o[:, sl] = qd.astype(BF16)
        kd_o[:, sl] = kd.astype(BF16)
    vm_o[...] = vm_i[...].astype(BF16)
    vd_o[...] = vd_i[...].astype(BF16)
    qs_o[...] = qs_i[...].astype(BF16)
    ks_o[...] = ks_i[...].astype(BF16)
    vs_o[...] = vs_i[...].astype(BF16)


def _rope_table(half, period):
    lane = jnp.arange(LANES)
    inv_freq = ROPE_THETA ** (-jnp.arange(half, dtype=F32) / half)
    tab = jnp.where((lane % period) < 2 * half, inv_freq[lane % half], 0.0)
    return tab.reshape(1, LANES).astype(F32)


def prep_qkv(proj, positions, qn_m, kn_m, qn_d, kn_d):
    t = proj.shape[0]
    tm = MOBA_BLOCK
    w = MIX_WIDTH
    row = lambda i: (i, 0)
    col_spec = [pl.BlockSpec((tm, w), functools.partial(lambda i, c: (i, c), c=c)) for c in range(9)]
    vec = pl.BlockSpec((1, LANES), lambda i: (0, 0))
    bf = jax.ShapeDtypeStruct((t, w), BF16)
    out_shape = [bf, jax.ShapeDtypeStruct((t, w), F32)] + [bf] * 8 + [
        jax.ShapeDtypeStruct((t // tm, N_HEADS, HEAD_DIM), F32)]
    out_specs = [pl.BlockSpec((tm, w), row)] * 10 + [
        pl.BlockSpec((1, N_HEADS, HEAD_DIM), lambda i: (i, 0, 0))]
    return pl.pallas_call(
        _prep_kernel,
        grid=(t // tm,),
        in_specs=[pl.BlockSpec((tm, 1), row), vec, vec, vec, vec, vec, vec] + col_spec,
        out_specs=out_specs,
        out_shape=out_shape,
        compiler_params=_params("parallel"),
        name="prep_qkv",
    )(positions.reshape(t, 1), _rope_table(ROT_HALF, HEAD_DIM), _rope_table(ROT_HALF_DIFF, DIFF_SUB),
      qn_m.reshape(1, LANES), kn_m.reshape(1, LANES),
      jnp.tile(qn_d, 2).reshape(1, LANES), jnp.tile(kn_d, 2).reshape(1, LANES),
      *([proj] * 9))


def _ones_column(rows, at=0):
    return (lax.broadcasted_iota(jnp.int32, (rows, LANES), 1) == at).astype(BF16)


def _value_with_ones(v):
    return jnp.concatenate([v, jnp.ones(v.shape, BF16)], axis=1)


def _moba_kernel(q_ref, qf_ref, kmean_ref, k_ref, v_ref, o_ref, *, tk):
    blk = MOBA_BLOCK
    qi = pl.program_id(2)
    scale = HEAD_DIM ** -0.5
    big = NEG_INF / scale
    nbp = kmean_ref.shape[0]

    gate = lax.dot_general(kmean_ref[...], qf_ref[...], _NT, precision=lax.Precision.HIGHEST,
                           preferred_element_type=F32)
    rid = lax.broadcasted_iota(jnp.int32, gate.shape, 0)
    rank = jnp.zeros(gate.shape, jnp.int32)
    for m in range(nbp):
        g_m = gate[m:m + 1, :]
        beats = (g_m > gate) | ((g_m == gate) & (rid > m))
        rank = rank + jnp.where(beats & (m < qi), 1, 0)
    allowed = ((rid < qi) & (rank < MOBA_TOPK)) | (rid == qi)
    pen_t = jnp.where(allowed, 0.0, big)
    pen_t = jnp.concatenate([pen_t, jnp.full((LANES - nbp, blk), big, F32)], axis=0)
    penalty = pen_t.T

    q = q_ref[...]
    lane = lax.broadcasted_iota(jnp.int32, (blk, LANES), 1)
    q_pos = qi * blk + lax.broadcasted_iota(jnp.int32, (blk, tk), 0)
    col = lax.broadcasted_iota(jnp.int32, (blk, tk), 1)
    krow = lax.broadcasted_iota(jnp.int32, (tk, LANES), 0)
    klane = lax.broadcasted_iota(jnp.int32, (tk, LANES), 1)
    n_tiles = (qi * blk) // tk + 1

    def scores(n, q_aug, shift_lane):
        off = pl.multiple_of(n * tk, tk)
        kb = k_ref[pl.ds(off, tk), :]
        block_of_row = n * (tk // blk) + krow // blk
        marks = ((klane == block_of_row) | (klane == shift_lane)).astype(BF16)
        s = lax.dot_general(q_aug, jnp.concatenate([kb, marks], axis=1), _NT,
                            preferred_element_type=F32) * scale
        return jnp.where(off + col <= q_pos, s, NEG_INF), off

    q_max = jnp.concatenate([q, penalty.astype(BF16)], axis=1)

    def sweep_max(n, best):
        s, _ = scores(n, q_max, -1)
        return jnp.maximum(best, s)

    best = lax.fori_loop(0, n_tiles, sweep_max, jnp.full((blk, tk), NEG_INF, F32))
    row_max = jnp.max(best, axis=-1, keepdims=True)
    shift_lane = LANES - 1
    q_sum = jnp.concatenate(
        [q, jnp.where(lane == shift_lane, -row_max / scale, penalty).astype(BF16)], axis=1)

    def sweep_sum(n, acc):
        s, off = scores(n, q_sum, shift_lane)
        vb = v_ref[pl.ds(off, tk), :]
        return acc + jnp.dot(jnp.exp(s).astype(BF16), _value_with_ones(vb), preferred_element_type=F32)

    acc = lax.fori_loop(0, n_tiles, sweep_sum, jnp.zeros((blk, 2 * HEAD_DIM), F32))
    o_ref[...] = (acc[:, :HEAD_DIM] / acc[:, HEAD_DIM:]).astype(o_ref.dtype)


def moba_attention(q, qf, kmean, k, v, batch, seq, tk=512):
    t = q.shape[0]
    blk = MOBA_BLOCK
    nq = seq // blk
    tk = min(tk, seq)
    qspec = pl.BlockSpec((blk, HEAD_DIM), lambda b, h, i: (b * nq + i, h))
    kvspec = pl.BlockSpec((seq, HEAD_DIM), lambda b, h, i: (b, h))
    return pl.pallas_call(
        functools.partial(_moba_kernel, tk=tk),
        grid=(batch, N_HEADS, nq),
        in_specs=[qspec, qspec,
                  pl.BlockSpec((None, None, kmean.shape[2], HEAD_DIM), lambda b, h, i: (b, h, 0, 0)),
                  kvspec, kvspec],
        out_specs=qspec,
        out_shape=jax.ShapeDtypeStruct((t, MIX_WIDTH), BF16),
        compiler_params=_params("parallel", "parallel", "arbitrary"),
        name="moba_attention",
    )(q, qf, kmean, k, v)


def _diff_kernel(lq1_ref, lk1_ref, lq2_ref, lk2_ref, g_ref, q_ref, k_ref, v_ref, o_ref, *, tq, tk, lam_init):
    qi = pl.program_id(2)
    q = q_ref[...] * (DIFF_SUB ** -0.5)
    lane = lax.broadcasted_iota(jnp.int32, q.shape, 1)
    zero = jnp.zeros_like(q)
    qq = jnp.concatenate([jnp.where(lane < DIFF_SUB, q, zero),
                          jnp.where(lane < DIFF_SUB, zero, q)], axis=0)
    rows = 2 * tq
    q_pos = qi * tq + (lax.broadcasted_iota(jnp.int32, (rows, tk), 0) & (tq - 1))
    col = lax.broadcasted_iota(jnp.int32, (rows, tk), 1)
    n_full = (qi * tq) // tk
    last = pl.multiple_of(n_full * tk, tk)
    causal = last + col <= q_pos

    def lane_tile_max(s):
        out = s[:, :LANES]
        for c in range(LANES, tk, LANES):
            out = jnp.maximum(out, s[:, c:c + LANES])
        return out

    def sweep_max(n, best):
        kb = k_ref[pl.ds(pl.multiple_of(n * tk, tk), tk), :]
        return jnp.maximum(best, lane_tile_max(lax.dot_general(qq, kb, _NT, preferred_element_type=F32)))

    best = lax.fori_loop(0, n_full, sweep_max, jnp.full((rows, LANES), NEG_INF, F32))
    s_last = lax.dot_general(qq, k_ref[pl.ds(last, tk), :], _NT, preferred_element_type=F32)
    best = jnp.maximum(best, lane_tile_max(jnp.where(causal, s_last, NEG_INF)))
    row_max = jnp.max(best, axis=-1, keepdims=True)

    lane_r = lax.broadcasted_iota(jnp.int32, (rows, LANES), 1)
    q_aug = jnp.concatenate([qq, jnp.where(lane_r == 0, -row_max, 0.0).astype(BF16)], axis=1)
    ones_col = _ones_column(tk)

    def tile_sum(off, mask):
        k_aug = jnp.concatenate([k_ref[pl.ds(off, tk), :], ones_col], axis=1)
        v_aug = _value_with_ones(v_ref[pl.ds(off, tk), :])
        outs = []
        for half in range(2):
            s = lax.dot_general(q_aug[half * tq:(half + 1) * tq], k_aug, _NT,
                                preferred_element_type=F32)
            if mask is not None:
                s = jnp.where(mask[half * tq:(half + 1) * tq], s, NEG_INF)
            outs.append(jnp.dot(jnp.exp(s).astype(BF16), v_aug, preferred_element_type=F32))
        return jnp.concatenate(outs, axis=0)

    def sweep_sum(n, acc):
        return acc + tile_sum(pl.multiple_of(n * tk, tk), None)

    acc = lax.fori_loop(0, n_full, sweep_sum, jnp.zeros((rows, 2 * HEAD_DIM), F32))
    acc = acc + tile_sum(last, causal)

    lam = (jnp.exp(jnp.sum(lq1_ref[...] * lk1_ref[...], axis=-1, keepdims=True))
           - jnp.exp(jnp.sum(lq2_ref[...] * lk2_ref[...], axis=-1, keepdims=True)) + lam_init)
    out = acc[:, :HEAD_DIM] / acc[:, HEAD_DIM:]
    o = out[:tq] - lam * out[tq:]
    ms = jnp.mean(o * o, axis=-1, keepdims=True)
    o_ref[...] = (o * lax.rsqrt(ms + NORM_EPS) * g_ref[...] * (1.0 - lam_init)).astype(o_ref.dtype)


def diff_attention(q, k, v, lq1, lk1, lq2, lk2, subln, lam_init, batch, seq, tq=256, tk=512):
    t = q.shape[0]
    tq, tk = min(tq, seq), min(tk, seq)
    nq = seq // tq
    qspec = pl.BlockSpec((tq, HEAD_DIM), lambda b, h, i: (b * nq + i, h))
    kvspec = pl.BlockSpec((seq, HEAD_DIM), lambda b, h, i: (b, h))
    lspec = pl.BlockSpec((1, DIFF_SUB), lambda b, h, i: (0, 0))
    return pl.pallas_call(
        functools.partial(_diff_kernel, tq=tq, tk=tk, lam_init=lam_init),
        grid=(batch, N_HEADS, nq),
        in_specs=[lspec, lspec, lspec, lspec,
                  pl.BlockSpec((1, HEAD_DIM), lambda b, h, i: (0, 0)), qspec, kvspec, kvspec],
        out_specs=qspec,
        out_shape=jax.ShapeDtypeStruct((t, MIX_WIDTH), BF16),
        compiler_params=_params("parallel", "parallel", "arbitrary"),
        name="diff_attention",
    )(lq1.reshape(1, DIFF_SUB), lk1.reshape(1, DIFF_SUB), lq2.reshape(1, DIFF_SUB),
      lk2.reshape(1, DIFF_SUB), subln.reshape(1, HEAD_DIM), q, k, v)


def _stick_kernel(q_ref, k_ref, v_ref, o_ref, *, tq, tk):
    qi = pl.program_id(2)
    scale = HEAD_DIM ** -0.5
    q = q_ref[...]
    upper = (lax.broadcasted_iota(jnp.int32, (tk, tk), 0)
             > lax.broadcasted_iota(jnp.int32, (tk, tk), 1)).astype(BF16)
    q_pos = qi * tq + lax.broadcasted_iota(jnp.int32, (tq, tk), 0)
    col = lax.broadcasted_iota(jnp.int32, (tq, tk), 1)
    n_diag = tq // tk
    n_past = qi * n_diag

    def tile(n, later, acc, masked):
        off = pl.multiple_of(n * tk, tk)
        kb = k_ref[pl.ds(off, tk), :]
        vb = v_ref[pl.ds(off, tk), :]
        z = lax.dot_general(q, kb, _NT, preferred_element_type=F32) * scale
        sp = jnp.maximum(z, 0.0) + jnp.log(1.0 + jnp.exp(-jnp.abs(z)))
        if masked:
            strict = (off + col) < q_pos
            sp = jnp.where(strict, sp, 0.0)
        hi = sp.astype(BF16)
        lo = (sp - hi.astype(F32)).astype(BF16)
        between = (jnp.dot(hi, upper, preferred_element_type=F32)
                   + jnp.dot(lo, upper, preferred_element_type=F32))
        a = jnp.exp(z - sp - between - later)
        if masked:
            a = jnp.where(strict, a, 0.0)
        acc = acc + jnp.dot(a.astype(BF16), vb, preferred_element_type=F32)
        return later + jnp.sum(sp, axis=-1, keepdims=True), acc

    later = jnp.zeros((tq, 1), F32)
    acc = jnp.zeros((tq, HEAD_DIM), F32)
    for d in range(n_diag):
        later, acc = tile(n_past + n_diag - 1 - d, later, acc, True)

    def body(it, carry):
        return tile(n_past - 1 - it, *carry, False)

    _, acc = lax.fori_loop(0, n_past, body, (later, acc))
    o_ref[...] = acc.astype(o_ref.dtype)


def stick_attention(q, k, v, batch, seq, tq=512, tk=256):
    t = q.shape[0]
    tq, tk = min(tq, seq), min(tk, seq)
    nq = seq // tq
    qspec = pl.BlockSpec((tq, HEAD_DIM), lambda b, h, i: (b * nq + i, h))
    kvspec = pl.BlockSpec((seq, HEAD_DIM), lambda b, h, i: (b, h))
    return pl.pallas_call(
        functools.partial(_stick_kernel, tq=tq, tk=tk),
        grid=(batch, N_HEADS, nq),
        in_specs=[qspec, kvspec, kvspec],
        out_specs=qspec,
        out_shape=jax.ShapeDtypeStruct((t, MIX_WIDTH), BF16),
        compiler_params=_params("parallel", "parallel", "arbitrary"),
        name="stick_attention",
    )(q, k, v)


def _merge_kernel(om_ref, od_ref, os_ref, w_ref, gm_ref, gd_ref, gs_ref, o_ref):
    acc = jax.nn.sigmoid(gm_ref[...]) * jnp.dot(om_ref[...], w_ref[0].astype(BF16),
                                                preferred_element_type=F32)
    acc += jax.nn.sigmoid(gd_ref[...]) * jnp.dot(od_ref[...], w_ref[1].astype(BF16),
                                                 preferred_element_type=F32)
    acc += jax.nn.sigmoid(gs_ref[...]) * jnp.dot(os_ref[...], w_ref[2].astype(BF16),
                                                 preferred_element_type=F32)
    o_ref[...] = acc.astype(o_ref.dtype)


def merge_branches(o_m, o_d, o_s, w_branch, layer, proj, gate_col, tm=1024, tn=512):
    t, kdim = o_m.shape
    d = w_branch.shape[3]
    tm, tn = min(tm, t), min(tn, d)
    ospec = pl.BlockSpec((tm, kdim), lambda i, j: (i, 0))
    gspecs = [pl.BlockSpec((tm, tn), functools.partial(lambda i, j, c: (i, c + j), c=(gate_col + c * d) // tn))
              for c in range(3)]
    return pl.pallas_call(
        _merge_kernel,
        grid=(t // tm, d // tn),
        in_specs=[ospec, ospec, ospec,
                  pl.BlockSpec((None, 3, kdim, tn), lambda i, j: (layer, 0, 0, j))] + gspecs,
        out_specs=pl.BlockSpec((tm, tn), lambda i, j: (i, j)),
        out_shape=jax.ShapeDtypeStruct((t, d), BF16),
        compiler_params=_params("parallel", "parallel"),
        name="merge_branches",
    )(o_m, o_d, o_s, w_branch, proj, proj, proj)


def _extract_top(x, rounds):
    rows, n = x.shape
    rid = lax.broadcasted_iota(jnp.int32, (rows, n), 0)
    kid = lax.broadcasted_iota(jnp.int32, (rounds, n), 0)

    def body(k, carry):
        x, rank, vals, idxs = carry
        best = jnp.max(x, axis=0, keepdims=True)
        first = jnp.min(jnp.where(x == best, rid, rows), axis=0, keepdims=True)
        hit = rid == first
        rank = jnp.where(hit, k, rank)
        x = jnp.where(hit, -jnp.inf, x)
        vals = jnp.where(kid == k, best, vals)
        idxs = jnp.where(kid == k, first, idxs)
        return x, rank, vals, idxs

    init = (x, jnp.full((rows, n), rounds, jnp.int32), jnp.zeros((rounds, n), F32),
            jnp.zeros((rounds, n), jnp.int32))
    _, rank, vals, idxs = lax.fori_loop(0, rounds, body, init)
    return vals, rank, idxs


def _peer_score_kernel(q_ref, keys_ref, cb_o, eb_o, na_o, wa_o):
    nk, topk = PEER_NKEYS, PEER_TOPK
    s = lax.dot_general(keys_ref[...], q_ref[...], _NT, precision=lax.Precision.HIGHEST,
                        preferred_element_type=F32)
    sa, sb = s[:nk], s[nk:]
    va, rank_a, _ = _extract_top(sa, topk)
    vb, rank_b, _ = _extract_top(sb, topk)
    n = s.shape[1]
    row8 = lax.broadcasted_iota(jnp.int32, (8, n), 0)
    ninf = -jnp.inf
    a = [va[r:r + 1] for r in range(topk)]
    groups = [a[0] + vb]
    for r, lim in ((1, 8), (2, 5), (3, 4), (4, 3)):
        groups.append(jnp.where(row8 < lim, a[r] + vb[0:8], ninf))
    a_mix = jnp.where(row8 < 2, a[5], jnp.where(row8 < 4, a[6], jnp.where(
        row8 < 6, a[7], jnp.where(row8 == 6, a[8], a[9]))))
    b_mix = jnp.where((row8 < 6) & ((row8 & 1) == 1), vb[1:2], vb[0:1])
    groups.append(a_mix + b_mix)
    tail = jnp.full((8, n), ninf, F32)
    for k in range(6):
        tail = jnp.where(row8 == k, a[10 + k] + vb[0:1], tail)
    groups.append(tail)
    cand = jnp.concatenate(groups, axis=0)
    top_s, rank_c, _ = _extract_top(cand, topk)
    taken = (rank_c < topk).astype(jnp.int32)
    counts = [jnp.sum(taken[0:16], axis=0, keepdims=True)]
    counts += [jnp.sum(taken[8 + 8 * r:16 + 8 * r], axis=0, keepdims=True) for r in range(1, 5)]
    counts += [taken[48 + 2 * k:49 + 2 * k] + taken[49 + 2 * k:50 + 2 * k] for k in range(3)]
    counts += [taken[54 + k:55 + k] for k in range(8)]
    count = jnp.concatenate(counts, axis=0)
    z = jnp.sum(jnp.exp(top_s - top_s[0:1]), axis=0, keepdims=True)
    na = jnp.zeros(sa.shape, jnp.int32)
    for r in range(topk):
        na = jnp.where(rank_a == r, count[r:r + 1], na)
    cb_o[...] = rank_b.astype(F32)
    eb_o[...] = jnp.exp(sb - vb[0:1])
    na_o[...] = na.astype(F32)
    wa_o[...] = jnp.exp(sa - va[0:1]) / z


def peer_scores(q, key_a, key_b, tn=256):
    t = q.shape[0]
    tn = min(tn, t)
    half = key_a.shape[-1]
    zeros = jnp.zeros_like(key_a)
    keys = jnp.concatenate([jnp.concatenate([key_a, zeros], axis=-1),
                            jnp.concatenate([zeros, key_b], axis=-1)], axis=1)
    del half
    out = jax.ShapeDtypeStruct((PEER_HEADS, PEER_NKEYS, t), F32)
    ospec = pl.BlockSpec((None, PEER_NKEYS, tn), lambda i, h: (h, 0, i))
    return pl.pallas_call(
        _peer_score_kernel,
        grid=(t // tn, PEER_HEADS),
        in_specs=[pl.BlockSpec((tn, LANES), lambda i, h: (i, h)),
                  pl.BlockSpec((None, 2 * PEER_NKEYS, LANES), lambda i, h: (h, 0, 0))],
        out_specs=[ospec] * 4,
        out_shape=[out] * 4,
        compiler_params=_params("parallel", "parallel"),
        name="peer_scores",
    )(q, keys)


def _peer_dense_kernel(xt_ref, u_ref, vt_ref, cb_ref, eb_ref, na_ref, wa_ref, o_ref,
                       coef_even, coef_odd, *, te):
    e = pl.program_id(1)
    last = pl.num_programs(1) - 2
    tn = o_ref.shape[1]
    half = tn // 2

    @pl.when(e == 0)
    def _():
        o_ref[...] = jnp.zeros_like(o_ref)
        coef_odd[...] = jnp.zeros_like(coef_odd)

    def gate_into_coef(hid, lane0, coef_out):
        chunk = 16
        for di in range(te // PEER_NKEYS):
            i = jnp.minimum(e, last) * (te // PEER_NKEYS) + di
            for sub in range(0, hid.shape[1], LANES):
                lanes = slice(lane0 + sub, lane0 + sub + LANES)
                n_rows = [na_ref[i, h:h + 1, lanes] for h in range(PEER_HEADS)]
                w_rows = [wa_ref[i, h:h + 1, lanes] for h in range(PEER_HEADS)]
                for r in range(0, PEER_NKEYS, chunk):
                    rs = slice(r, r + chunk)
                    w = None
                    for h in range(PEER_HEADS):
                        term = jnp.where(cb_ref[h, rs, lanes] < n_rows[h],
                                         eb_ref[h, rs, lanes] * w_rows[h], 0.0)
                        w = term if w is None else w + term
                    row0 = di * PEER_NKEYS + r
                    x = hid[row0:row0 + chunk, sub:sub + LANES]
                    act = 0.5 * x * (1.0 + lax.erf(x * (2.0 ** -0.5)))
                    coef_out[row0:row0 + chunk, lanes] = (act * w).astype(BF16)

    def step(coef_out, coef_in):
        lo, hi = slice(0, half), slice(half, tn)
        hid_lo = jnp.dot(u_ref[...], xt_ref[:, lo], preferred_element_type=F32)
        o_ref[:, lo] += jnp.dot(vt_ref[...], coef_in[:, lo], preferred_element_type=F32)
        hid_hi = jnp.dot(u_ref[...], xt_ref[:, hi], preferred_element_type=F32)
        gate_into_coef(hid_lo, 0, coef_out)
        o_ref[:, hi] += jnp.dot(vt_ref[...], coef_in[:, hi], preferred_element_type=F32)
        gate_into_coef(hid_hi, half, coef_out)

    @pl.when((e & 1) == 0)
    def _():
        step(coef_even, coef_odd)

    @pl.when((e & 1) == 1)
    def _():
        step(coef_odd, coef_even)


def peer_dense(xt, u, vt, cb, eb, na, wa, tn=512, te=256):
    d, t = xt.shape
    n_exp = u.shape[0]
    tn, te = min(tn, t), min(te, n_exp)
    n_e = n_exp // te
    sspec = pl.BlockSpec((PEER_HEADS, PEER_NKEYS, tn), lambda i, e: (0, 0, i))
    aspec = pl.BlockSpec((PEER_NKEYS, PEER_HEADS, tn), lambda i, e: (0, 0, i))
    na, wa = na.transpose(1, 0, 2), wa.transpose(1, 0, 2)
    return pl.pallas_call(
        functools.partial(_peer_dense_kernel, te=te),
        grid=(t // tn, n_e + 1),
        in_specs=[pl.BlockSpec((d, tn), lambda i, e: (0, i)),
                  pl.BlockSpec((te, d), lambda i, e: (jnp.minimum(e, n_e - 1), 0)),
                  pl.BlockSpec((d, te), lambda i, e: (0, jnp.maximum(e - 1, 0))),
                  sspec, sspec, aspec, aspec],
        out_specs=pl.BlockSpec((d, tn), lambda i, e: (0, i)),
        out_shape=jax.ShapeDtypeStruct((d, t), F32),
        scratch_shapes=[pltpu.VMEM((te, tn), BF16), pltpu.VMEM((te, tn), BF16)],
        compiler_params=_params("parallel", "arbitrary"),
        name="peer_dense",
    )(xt, u, vt, cb, eb, na, wa)


def kernel(x, p, positions, norm_mix, w_in, qn_moba, kn_moba, qn_diff, kn_diff, lambda_q1, lambda_k1, lambda_q2, lambda_k2, subln_diff, w_branch, w_out, norm_ffn, peer_wq, peer_key_a, peer_key_b, peer_u, peer_v, norm_ple, w_ple_gate, w_ple_proj):
    b, s, d = x.shape
    t = b * s
    depth = w_in.shape[0]
    nb = s // MOBA_BLOCK
    gate_col = 9 * MIX_WIDTH
    h = x.reshape(t, d)
    p_flat = p.reshape(depth, t, -1)
    for i in range(depth):
        n1 = rmsnorm(h, norm_mix[i])
        proj = matmul(n1, w_in, i)
        (qm, qmf, km, vm, qd, kd, vd, qs, ks, vs, kmean) = prep_qkv(
            proj, positions, qn_moba[i], kn_moba[i], qn_diff[i], kn_diff[i])
        kmean = kmean.reshape(b, nb, N_HEADS, HEAD_DIM).transpose(0, 2, 1, 3)
        kmean = jnp.pad(kmean, ((0, 0), (0, 0), (0, -nb % 8), (0, 0)))
        o_m = moba_attention(qm, qmf, kmean, km, vm, b, s)
        lam_init = 0.8 - 0.6 * math.exp(-0.3 * i)
        o_d = diff_attention(qd, kd, vd, lambda_q1[i], lambda_k1[i], lambda_q2[i], lambda_k2[i],
                             subln_diff[i], lam_init, b, s)
        o_s = stick_attention(qs, ks, vs, b, s)
        merged = merge_branches(o_m, o_d, o_s, w_branch, i, proj, gate_col)
        h = matmul(merged, w_out, i, mode="residual", res=h)
        n2 = rmsnorm(h, norm_ffn[i])
        q = matmul(n2, peer_wq, i)
        cb, eb, na, wa = peer_scores(q, peer_key_a[i], peer_key_b[i])
        ffn_t = peer_dense(n2.T, peer_u[i].astype(BF16), peer_v[i].T.astype(BF16), cb, eb, na, wa)
        h = h + ffn_t.T
        n3 = rmsnorm(h, norm_ple[i])
        h = matmul(n3, w_ple_gate, i, mode="ple", res=h, p=p_flat, wp=w_ple_proj)
    return h.reshape(b, s, d)
```

```python
import functools
import math

import jax
import jax.numpy as jnp
from jax import lax
from jax.experimental import pallas as pl
from jax.experimental.pallas import tpu as pltpu

F32 = jnp.float32
BF16 = jnp.bfloat16

LANES = 128
HEAD_DIM = 128
N_HEADS = 8
MIX_WIDTH = N_HEADS * HEAD_DIM
DIFF_SUB = HEAD_DIM // 2
ROPE_THETA = 500000.0
ROT_HALF = HEAD_DIM // 8
ROT_HALF_DIFF = DIFF_SUB // 8
MOBA_BLOCK = 256
MOBA_TOPK = 3
PEER_HEADS = 8
PEER_NKEYS = 128
PEER_TOPK = 16
NORM_EPS = 1e-6
NEG_INF = -1e30
VMEM_LIMIT = 56 * 1024 * 1024

_NT = (((1,), (1,)), ((), ()))


def _params(*sem):
    return pltpu.CompilerParams(dimension_semantics=sem, vmem_limit_bytes=VMEM_LIMIT)


def _rmsnorm_kernel(x_ref, g_ref, o_ref, *maybe_ot_ref):
    x = x_ref[...]
    ms = jnp.mean(x * x, axis=-1, keepdims=True)
    y = x * lax.rsqrt(ms + NORM_EPS) * g_ref[...]
    o_ref[...] = y.astype(o_ref.dtype)
    for ot_ref in maybe_ot_ref:
        ot_ref[...] = y.T.astype(ot_ref.dtype)


def rmsnorm(x, gain, tm=256, with_transpose=False):
    t, d = x.shape
    tm = min(tm, t)
    out_specs = [pl.BlockSpec((tm, d), lambda i: (i, 0))]
    out_shape = [jax.ShapeDtypeStruct((t, d), BF16)]
    if with_transpose:
        out_specs.append(pl.BlockSpec((d, tm), lambda i: (0, i)))
        out_shape.append(jax.ShapeDtypeStruct((d, t), BF16))
    outs = pl.pallas_call(
        _rmsnorm_kernel,
        grid=(t // tm,),
        in_specs=[pl.BlockSpec((tm, d), lambda i: (i, 0)),
                  pl.BlockSpec((1, d), lambda i: (0, 0))],
        out_specs=out_specs,
        out_shape=out_shape,
        compiler_params=_params("parallel"),
        name="rmsnorm_t" if with_transpose else "rmsnorm",
    )(x, gain.reshape(1, d))
    return outs if with_transpose else outs[0]


def _transpose_kernel(x_ref, o_ref):
    o_ref[...] = x_ref[...].T.astype(o_ref.dtype)


def transpose_to_bf16(x, layer, tb=1024):
    _, r, c = x.shape
    tr, tc = min(tb, r), min(tb, c)
    return pl.pallas_call(
        _transpose_kernel,
        grid=(r // tr, c // tc),
        in_specs=[pl.BlockSpec((None, tr, tc), lambda i, j: (layer, i, j))],
        out_specs=pl.BlockSpec((tc, tr), lambda i, j: (j, i)),
        out_shape=jax.ShapeDtypeStruct((c, r), BF16),
        compiler_params=_params("parallel", "parallel"),
        name="transpose_to_bf16",
    )(x)


def _mm_kernel(*refs, mode):
    if mode == "plain":
        a_ref, b_ref, o_ref = refs
    elif mode == "residual":
        a_ref, b_ref, r_ref, o_ref = refs
    else:
        a_ref, b_ref, r_ref, p_ref, wp_ref, o_ref = refs
    acc = jnp.dot(a_ref[...], b_ref[...].astype(BF16), preferred_element_type=F32)
    if mode == "plain":
        o_ref[...] = acc.astype(o_ref.dtype)
    elif mode == "residual":
        o_ref[...] = (r_ref[...] + acc).astype(o_ref.dtype)
    else:
        emb = jnp.dot(p_ref[...].astype(BF16), wp_ref[...].astype(BF16),
                      preferred_element_type=F32)
        o_ref[...] = (r_ref[...] + jax.nn.sigmoid(acc) * emb).astype(o_ref.dtype)


def matmul(a, w, layer, *, mode="plain", res=None, p=None, wp=None, out_dtype=F32, tm=2048, tn=None):
    m, kdim = a.shape
    n = w.shape[2]
    if tn is None:
        tn = 512 if mode == "plain" else 256
    tm, tn = min(tm, m), min(tn, n)
    in_specs = [pl.BlockSpec((tm, kdim), lambda i, j: (i, 0), pipeline_mode=pl.Buffered(1)),
                pl.BlockSpec((None, kdim, tn), lambda i, j: (layer, 0, j))]
    args = [a, w]
    if mode in ("residual", "ple"):
        in_specs.append(pl.BlockSpec((tm, tn), lambda i, j: (i, j)))
        args.append(res)
    if mode == "ple":
        pd = p.shape[2]
        in_specs += [pl.BlockSpec((None, tm, pd), lambda i, j: (layer, i, 0)),
                     pl.BlockSpec((None, pd, tn), lambda i, j: (layer, 0, j))]
        args += [p, wp]
    return pl.pallas_call(
        functools.partial(_mm_kernel, mode=mode),
        grid=(m // tm, n // tn),
        in_specs=in_specs,
        out_specs=pl.BlockSpec((tm, tn), lambda i, j: (i, j)),
        out_shape=jax.ShapeDtypeStruct((m, n), out_dtype),
        compiler_params=_params("parallel", "arbitrary"),
        name="matmul_" + mode,
    )(*args)


def _rope(x, cos, sin, half, period):
    lane = lax.broadcasted_iota(jnp.int32, x.shape, 1)
    first = (lane & (period - 1)) < half
    partner = jnp.where(first, pltpu.roll(x, LANES - half, 1), pltpu.roll(x, half, 1))
    return x * cos + partner * jnp.where(first, -sin, sin)


def _prep_kernel(pos_ref, fm_ref, fd_ref, qng_ref, kng_ref, qdg_ref, kdg_ref,
                 qm_i, km_i, vm_i, qd_i, kd_i, vd_i, qs_i, ks_i, vs_i,
                 qm_o, qmf_o, km_o, vm_o, qd_o, kd_o, vd_o, qs_o, ks_o, vs_o, kmean_o):
    pos = pos_ref[...].astype(F32)
    ang_m = pos * fm_ref[...]
    ang_d = pos * fd_ref[...]
    cos_m, sin_m = jnp.cos(ang_m), jnp.sin(ang_m)
    cos_d, sin_d = jnp.cos(ang_d), jnp.sin(ang_d)
    lane = lax.broadcasted_iota(jnp.int32, ang_m.shape, 1)
    low = lane < DIFF_SUB

    def norm_full(x, g):
        ms = jnp.mean(x * x, axis=-1, keepdims=True)
        return x * lax.rsqrt(ms + NORM_EPS) * g

    def norm_halves(x, g):
        xx = x * x
        lo = jnp.sum(jnp.where(low, xx, 0.0), axis=-1, keepdims=True)
        hi = jnp.sum(jnp.where(low, 0.0, xx), axis=-1, keepdims=True)
        ms = jnp.where(low, lo, hi) * (1.0 / DIFF_SUB)
        return x * lax.rsqrt(ms + NORM_EPS) * g

    for h in range(N_HEADS):
        sl = slice(h * HEAD_DIM, (h + 1) * HEAD_DIM)
        qm = _rope(norm_full(qm_i[:, sl], qng_ref[...]), cos_m, sin_m, ROT_HALF, HEAD_DIM)
        km = _rope(norm_full(km_i[:, sl], kng_ref[...]), cos_m, sin_m, ROT_HALF, HEAD_DIM)
        qmf_o[:, sl] = qm
        qm_o[:, sl] = qm.astype(BF16)
        km_o[:, sl] = km.astype(BF16)
        kmean_o[0, h:h + 1, :] = jnp.mean(km, axis=0, keepdims=True)
        qd = _rope(norm_halves(qd_i[:, sl], qdg_ref[...]), cos_d, sin_d, ROT_HALF_DIFF, DIFF_SUB)
        kd = _rope(norm_halves(kd_i[:, sl], kdg_ref[...]), cos_d, sin_d, ROT_HALF_DIFF, DIFF_SUB)
        qd_o[:, sl] = qd.astype(BF16)
        kd_o[:, sl] = kd.astype(BF16)
    vm_o[...] = vm_i[...].astype(BF16)
    vd_o[...] = vd_i[...].astype(BF16)
    qs_o[...] = qs_i[...].astype(BF16)
    ks_o[...] = ks_i[...].astype(BF16)
    vs_o[...] = vs_i[...].astype(BF16)


def _rope_table(half, period):
    lane = jnp.arange(LANES)
    inv_freq = ROPE_THETA ** (-jnp.arange(half, dtype=F32) / half)
    tab = jnp.where((lane % period) < 2 * half, inv_freq[lane % half], 0.0)
    return tab.reshape(1, LANES).astype(F32)


def prep_qkv(proj, positions, qn_m, kn_m, qn_d, kn_d):
    t = proj.shape[0]
    tm = MOBA_BLOCK
    w = MIX_WIDTH
    row = lambda i: (i, 0)
    col_spec = [pl.BlockSpec((tm, w), functools.partial(lambda i, c: (i, c), c=c)) for c in range(9)]
    vec = pl.BlockSpec((1, LANES), lambda i: (0, 0))
    bf = jax.ShapeDtypeStruct((t, w), BF16)
    out_shape = [bf, jax.ShapeDtypeStruct((t, w), F32)] + [bf] * 8 + [
        jax.ShapeDtypeStruct((t // tm, N_HEADS, HEAD_DIM), F32)]
    out_specs = [pl.BlockSpec((tm, w), row)] * 10 + [
        pl.BlockSpec((1, N_HEADS, HEAD_DIM), lambda i: (i, 0, 0))]
    return pl.pallas_call(
        _prep_kernel,
        grid=(t // tm,),
        in_specs=[pl.BlockSpec((tm, 1), row), vec, vec, vec, vec, vec, vec] + col_spec,
        out_specs=out_specs,
        out_shape=out_shape,
        compiler_params=_params("parallel"),
        name="prep_qkv",
    )(positions.reshape(t, 1), _rope_table(ROT_HALF, HEAD_DIM), _rope_table(ROT_HALF_DIFF, DIFF_SUB),
      qn_m.reshape(1, LANES), kn_m.reshape(1, LANES),
      jnp.tile(qn_d, 2).reshape(1, LANES), jnp.tile(kn_d, 2).reshape(1, LANES),
      *([proj] * 9))


def _ones_column(rows, at=0):
    return (lax.broadcasted_iota(jnp.int32, (rows, LANES), 1) == at).astype(BF16)


def _value_with_ones(v):
    return jnp.concatenate([v, jnp.ones(v.shape, BF16)], axis=1)


def _moba_kernel(q_ref, qf_ref, kmean_ref, k_ref, v_ref, o_ref, *, tk):
    blk = MOBA_BLOCK
    qi = pl.program_id(2)
    scale = HEAD_DIM ** -0.5
    big = NEG_INF / scale
    nbp = kmean_ref.shape[0]

    gate = lax.dot_general(kmean_ref[...], qf_ref[...], _NT, precision=lax.Precision.HIGHEST,
                           preferred_element_type=F32)
    rid = lax.broadcasted_iota(jnp.int32, gate.shape, 0)
    rank = jnp.zeros(gate.shape, jnp.int32)
    for m in range(nbp):
        g_m = gate[m:m + 1, :]
        beats = (g_m > gate) | ((g_m == gate) & (rid > m))
        rank = rank + jnp.where(beats & (m < qi), 1, 0)
    allowed = ((rid < qi) & (rank < MOBA_TOPK)) | (rid == qi)
    pen_t = jnp.where(allowed, 0.0, big)
    pen_t = jnp.concatenate([pen_t, jnp.full((LANES - nbp, blk), big, F32)], axis=0)
    penalty = pen_t.T

    q = q_ref[...]
    lane = lax.broadcasted_iota(jnp.int32, (blk, LANES), 1)
    q_pos = qi * blk + lax.broadcasted_iota(jnp.int32, (blk, tk), 0)
    col = lax.broadcasted_iota(jnp.int32, (blk, tk), 1)
    krow = lax.broadcasted_iota(jnp.int32, (tk, LANES), 0)
    klane = lax.broadcasted_iota(jnp.int32, (tk, LANES), 1)
    n_tiles = (qi * blk) // tk + 1

    def scores(n, q_aug, shift_lane):
        off = pl.multiple_of(n * tk, tk)
        kb = k_ref[pl.ds(off, tk), :]
        block_of_row = n * (tk // blk) + krow // blk
        marks = ((klane == block_of_row) | (klane == shift_lane)).astype(BF16)
        s = lax.dot_general(q_aug, jnp.concatenate([kb, marks], axis=1), _NT,
                            preferred_element_type=F32) * scale
        return jnp.where(off + col <= q_pos, s, NEG_INF), off

    q_max = jnp.concatenate([q, penalty.astype(BF16)], axis=1)

    def sweep_max(n, best):
        s, _ = scores(n, q_max, -1)
        return jnp.maximum(best, s)

    best = lax.fori_loop(0, n_tiles, sweep_max, jnp.full((blk, tk), NEG_INF, F32))
    row_max = jnp.max(best, axis=-1, keepdims=True)
    shift_lane = LANES - 1
    q_sum = jnp.concatenate(
        [q, jnp.where(lane == shift_lane, -row_max / scale, penalty).astype(BF16)], axis=1)

    def sweep_sum(n, acc):
        s, off = scores(n, q_sum, shift_lane)
        vb = v_ref[pl.ds(off, tk), :]
        return acc + jnp.dot(jnp.exp(s).astype(BF16), _value_with_ones(vb), preferred_element_type=F32)

    acc = lax.fori_loop(0, n_tiles, sweep_sum, jnp.zeros((blk, 2 * HEAD_DIM), F32))
    o_ref[...] = (acc[:, :HEAD_DIM] / acc[:, HEAD_DIM:]).astype(o_ref.dtype)


def moba_attention(q, qf, kmean, k, v, batch, seq, tk=512):
    t = q.shape[0]
    blk = MOBA_BLOCK
    nq = seq // blk
    tk = min(tk, seq)
    qspec = pl.BlockSpec((blk, HEAD_DIM), lambda b, h, i: (b * nq + i, h))
    kvspec = pl.BlockSpec((seq, HEAD_DIM), lambda b, h, i: (b, h))
    return pl.pallas_call(
        functools.partial(_moba_kernel, tk=tk),
        grid=(batch, N_HEADS, nq),
        in_specs=[qspec, qspec,
                  pl.BlockSpec((None, None, kmean.shape[2], HEAD_DIM), lambda b, h, i: (b, h, 0, 0)),
                  kvspec, kvspec],
        out_specs=qspec,
        out_shape=jax.ShapeDtypeStruct((t, MIX_WIDTH), BF16),
        compiler_params=_params("parallel", "parallel", "arbitrary"),
        name="moba_attention",
    )(q, qf, kmean, k, v)


def _diff_kernel(lq1_ref, lk1_ref, lq2_ref, lk2_ref, g_ref, q_ref, k_ref, v_ref, o_ref, *, tq, tk, lam_init):
    qi = pl.program_id(2)
    q = q_ref[...] * (DIFF_SUB ** -0.5)
    lane = lax.broadcasted_iota(jnp.int32, q.shape, 1)
    zero = jnp.zeros_like(q)
    qq = jnp.concatenate([jnp.where(lane < DIFF_SUB, q, zero),
                          jnp.where(lane < DIFF_SUB, zero, q)], axis=0)
    rows = 2 * tq
    q_pos = qi * tq + (lax.broadcasted_iota(jnp.int32, (rows, tk), 0) & (tq - 1))
    col = lax.broadcasted_iota(jnp.int32, (rows, tk), 1)
    n_full = (qi * tq) // tk
    last = pl.multiple_of(n_full * tk, tk)
    causal = last + col <= q_pos

    def lane_tile_max(s):
        out = s[:, :LANES]
        for c in range(LANES, tk, LANES):
            out = jnp.maximum(out, s[:, c:c + LANES])
        return out

    def sweep_max(n, best):
        kb = k_ref[pl.ds(pl.multiple_of(n * tk, tk), tk), :]
        return jnp.maximum(best, lane_tile_max(lax.dot_general(qq, kb, _NT, preferred_element_type=F32)))

    best = lax.fori_loop(0, n_full, sweep_max, jnp.full((rows, LANES), NEG_INF, F32))
    s_last = lax.dot_general(qq, k_ref[pl.ds(last, tk), :], _NT, preferred_element_type=F32)
    best = jnp.maximum(best, lane_tile_max(jnp.where(causal, s_last, NEG_INF)))
    row_max = jnp.max(best, axis=-1, keepdims=True)

    lane_r = lax.broadcasted_iota(jnp.int32, (rows, LANES), 1)
    q_aug = jnp.concatenate([qq, jnp.where(lane_r == 0, -row_max, 0.0).astype(BF16)], axis=1)
    ones_col = _ones_column(tk)

    def tile_sum(off, mask):
        k_aug = jnp.concatenate([k_ref[pl.ds(off, tk), :], ones_col], axis=1)
        v_aug = _value_with_ones(v_ref[pl.ds(off, tk), :])
        outs = []
        for half in range(2):
            s = lax.dot_general(q_aug[half * tq:(half + 1) * tq], k_aug, _NT,
                                preferred_element_type=F32)
            if mask is not None:
                s = jnp.where(mask[half * tq:(half + 1) * tq], s, NEG_INF)
            outs.append(jnp.dot(jnp.exp(s).astype(BF16), v_aug, preferred_element_type=F32))
        return jnp.concatenate(outs, axis=0)

    def sweep_sum(n, acc):
        return acc + tile_sum(pl.multiple_of(n * tk, tk), None)

    acc = lax.fori_loop(0, n_full, sweep_sum, jnp.zeros((rows, 2 * HEAD_DIM), F32))
    acc = acc + tile_sum(last, causal)

    lam = (jnp.exp(jnp.sum(lq1_ref[...] * lk1_ref[...], axis=-1, keepdims=True))
           - jnp.exp(jnp.sum(lq2_ref[...] * lk2_ref[...], axis=-1, keepdims=True)) + lam_init)
    out = acc[:, :HEAD_DIM] / acc[:, HEAD_DIM:]
    o = out[:tq] - lam * out[tq:]
    ms = jnp.mean(o * o, axis=-1, keepdims=True)
    o_ref[...] = (o * lax.rsqrt(ms + NORM_EPS) * g_ref[...] * (1.0 - lam_init)).astype(o_ref.dtype)


def diff_attention(q, k, v, lq1, lk1, lq2, lk2, subln, lam_init, batch, seq, tq=256, tk=512):
    t = q.shape[0]
    tq, tk = min(tq, seq), min(tk, seq)
    nq = seq // tq
    qspec = pl.BlockSpec((tq, HEAD_DIM), lambda b, h, i: (b * nq + i, h))
    kvspec = pl.BlockSpec((seq, HEAD_DIM), lambda b, h, i: (b, h))
    lspec = pl.BlockSpec((1, DIFF_SUB), lambda b, h, i: (0, 0))
    return pl.pallas_call(
        functools.partial(_diff_kernel, tq=tq, tk=tk, lam_init=lam_init),
        grid=(batch, N_HEADS, nq),
        in_specs=[lspec, lspec, lspec, lspec,
                  pl.BlockSpec((1, HEAD_DIM), lambda b, h, i: (0, 0)), qspec, kvspec, kvspec],
        out_specs=qspec,
        out_shape=jax.ShapeDtypeStruct((t, MIX_WIDTH), BF16),
        compiler_params=_params("parallel", "parallel", "arbitrary"),
        name="diff_attention",
    )(lq1.reshape(1, DIFF_SUB), lk1.reshape(1, DIFF_SUB), lq2.reshape(1, DIFF_SUB),
      lk2.reshape(1, DIFF_SUB), subln.reshape(1, HEAD_DIM), q, k, v)


def _stick_kernel(q_ref, k_ref, v_ref, o_ref, *, tq, tk):
    qi = pl.program_id(2)
    scale = HEAD_DIM ** -0.5
    n_heads = q_ref.shape[1] // HEAD_DIM
    upper = (lax.broadcasted_iota(jnp.int32, (tk, tk), 0)
             > lax.broadcasted_iota(jnp.int32, (tk, tk), 1)).astype(BF16)
    q_pos = qi * tq + lax.broadcasted_iota(jnp.int32, (tq, tk), 0)
    col = lax.broadcasted_iota(jnp.int32, (tq, tk), 1)
    n_diag = tq // tk
    n_past = qi * n_diag

    def tile(n, hd, later, acc, masked):
        off = pl.multiple_of(n * tk, tk)
        cols = slice(hd * HEAD_DIM, (hd + 1) * HEAD_DIM)
        kb = k_ref[pl.ds(off, tk), cols]
        vb = v_ref[pl.ds(off, tk), cols]
        z = lax.dot_general(q_ref[:, cols], kb, _NT, preferred_element_type=F32) * scale
        sp = jnp.maximum(z, 0.0) + jnp.log(1.0 + jnp.exp(-jnp.abs(z)))
        if masked:
            strict = (off + col) < q_pos
            sp = jnp.where(strict, sp, 0.0)
        hi = sp.astype(BF16)
        lo = (sp - hi.astype(F32)).astype(BF16)
        between = (jnp.dot(hi, upper, preferred_element_type=F32)
                   + jnp.dot(lo, upper, preferred_element_type=F32))
        a = jnp.exp(z - sp - between - later)
        if masked:
            a = jnp.where(strict, a, 0.0)
        acc = acc + jnp.dot(a.astype(BF16), vb, preferred_element_type=F32)
        return later + jnp.sum(sp, axis=-1, keepdims=True), acc

    state = [(jnp.zeros((tq, 1), F32), jnp.zeros((tq, HEAD_DIM), F32)) for _ in range(n_heads)]
    for d in range(n_diag):
        state = [tile(n_past + n_diag - 1 - d, hd, *state[hd], True) for hd in range(n_heads)]

    def body(it, carry):
        return tuple(tile(n_past - 1 - it, hd, *carry[hd], False) for hd in range(n_heads))

    state = lax.fori_loop(0, n_past, body, tuple(state))
    for hd in range(n_heads):
        o_ref[:, hd * HEAD_DIM:(hd + 1) * HEAD_DIM] = state[hd][1].astype(o_ref.dtype)


def stick_attention(q, k, v, batch, seq, tq=512, tk=256, heads_per_step=2):
    t = q.shape[0]
    tq, tk = min(tq, seq), min(tk, seq)
    nq = seq // tq
    width = heads_per_step * HEAD_DIM
    qspec = pl.BlockSpec((tq, width), lambda b, h, i: (b * nq + i, h))
    kvspec = pl.BlockSpec((seq, width), lambda b, h, i: (b, h))
    return pl.pallas_call(
        functools.partial(_stick_kernel, tq=tq, tk=tk),
        grid=(batch, N_HEADS // heads_per_step, nq),
        in_specs=[qspec, kvspec, kvspec],
        out_specs=qspec,
        out_shape=jax.ShapeDtypeStruct((t, MIX_WIDTH), BF16),
        compiler_params=_params("parallel", "parallel", "arbitrary"),
        name="stick_attention",
    )(q, k, v)


def _merge_kernel(om_ref, od_ref, os_ref, w_ref, gm_ref, gd_ref, gs_ref, o_ref):
    acc = jax.nn.sigmoid(gm_ref[...]) * jnp.dot(om_ref[...], w_ref[0].astype(BF16),
                                                preferred_element_type=F32)
    acc += jax.nn.sigmoid(gd_ref[...]) * jnp.dot(od_ref[...], w_ref[1].astype(BF16),
                                                 preferred_element_type=F32)
    acc += jax.nn.sigmoid(gs_ref[...]) * jnp.dot(os_ref[...], w_ref[2].astype(BF16),
                                                 preferred_element_type=F32)
    o_ref[...] = acc.astype(o_ref.dtype)


def merge_branches(o_m, o_d, o_s, w_branch, layer, proj, gate_col, tm=1024, tn=512):
    t, kdim = o_m.shape
    d = w_branch.shape[3]
    tm, tn = min(tm, t), min(tn, d)
    ospec = pl.BlockSpec((tm, kdim), lambda i, j: (i, 0))
    gspecs = [pl.BlockSpec((tm, tn), functools.partial(lambda i, j, c: (i, c + j), c=(gate_col + c * d) // tn))
              for c in range(3)]
    return pl.pallas_call(
        _merge_kernel,
        grid=(t // tm, d // tn),
        in_specs=[ospec, ospec, ospec,
                  pl.BlockSpec((None, 3, kdim, tn), lambda i, j: (layer, 0, 0, j))] + gspecs,
        out_specs=pl.BlockSpec((tm, tn), lambda i, j: (i, j)),
        out_shape=jax.ShapeDtypeStruct((t, d), BF16),
        compiler_params=_params("parallel", "parallel"),
        name="merge_branches",
    )(o_m, o_d, o_s, w_branch, proj, proj, proj)


def _extract_top(x, rounds):
    rows, n = x.shape
    rid = lax.broadcasted_iota(jnp.int32, (rows, n), 0)
    kid = lax.broadcasted_iota(jnp.int32, (rounds, n), 0)

    def body(k, carry):
        x, rank, vals, idxs = carry
        best = jnp.max(x, axis=0, keepdims=True)
        first = jnp.min(jnp.where(x == best, rid, rows), axis=0, keepdims=True)
        hit = rid == first
        rank = jnp.where(hit, k, rank)
        x = jnp.where(hit, -jnp.inf, x)
        vals = jnp.where(kid == k, best, vals)
        idxs = jnp.where(kid == k, first, idxs)
        return x, rank, vals, idxs

    init = (x, jnp.full((rows, n), rounds, jnp.int32), jnp.zeros((rounds, n), F32),
            jnp.zeros((rounds, n), jnp.int32))
    _, rank, vals, idxs = lax.fori_loop(0, rounds, body, init)
    return vals, rank, idxs


def _peer_score_kernel(q_ref, keys_ref, cb_o, eb_o, na_o, wa_o):
    nk, topk = PEER_NKEYS, PEER_TOPK
    s = lax.dot_general(keys_ref[...], q_ref[...], _NT, precision=lax.Precision.HIGHEST,
                        preferred_element_type=F32)
    sa, sb = s[:nk], s[nk:]
    va, rank_a, _ = _extract_top(sa, topk)
    vb, rank_b, _ = _extract_top(sb, topk)
    n = s.shape[1]
    row8 = lax.broadcasted_iota(jnp.int32, (8, n), 0)
    ninf = -jnp.inf
    a = [va[r:r + 1] for r in range(topk)]
    groups = [a[0] + vb]
    for r, lim in ((1, 8), (2, 5), (3, 4), (4, 3)):
        groups.append(jnp.where(row8 < lim, a[r] + vb[0:8], ninf))
    a_mix = jnp.where(row8 < 2, a[5], jnp.where(row8 < 4, a[6], jnp.where(
        row8 < 6, a[7], jnp.where(row8 == 6, a[8], a[9]))))
    b_mix = jnp.where((row8 < 6) & ((row8 & 1) == 1), vb[1:2], vb[0:1])
    groups.append(a_mix + b_mix)
    tail = jnp.full((8, n), ninf, F32)
    for k in range(6):
        tail = jnp.where(row8 == k, a[10 + k] + vb[0:1], tail)
    groups.append(tail)
    cand = jnp.concatenate(groups, axis=0)
    top_s, rank_c, _ = _extract_top(cand, topk)
    taken = (rank_c < topk).astype(jnp.int32)
    counts = [jnp.sum(taken[0:16], axis=0, keepdims=True)]
    counts += [jnp.sum(taken[8 + 8 * r:16 + 8 * r], axis=0, keepdims=True) for r in range(1, 5)]
    counts += [taken[48 + 2 * k:49 + 2 * k] + taken[49 + 2 * k:50 + 2 * k] for k in range(3)]
    counts += [taken[54 + k:55 + k] for k in range(8)]
    count = jnp.concatenate(counts, axis=0)
    z = jnp.sum(jnp.exp(top_s - top_s[0:1]), axis=0, keepdims=True)
    na = jnp.zeros(sa.shape, jnp.int32)
    for r in range(topk):
        na = jnp.where(rank_a == r, count[r:r + 1], na)
    cb_o[...] = rank_b.astype(F32)
    eb_o[...] = jnp.exp(sb - vb[0:1])
    na_o[...] = na.astype(F32)
    wa_o[...] = jnp.exp(sa - va[0:1]) / z


def peer_scores(q, key_a, key_b, tn=256):
    t = q.shape[0]
    tn = min(tn, t)
    half = key_a.shape[-1]
    zeros = jnp.zeros_like(key_a)
    keys = jnp.concatenate([jnp.concatenate([key_a, zeros], axis=-1),
                            jnp.concatenate([zeros, key_b], axis=-1)], axis=1)
    del half
    out = jax.ShapeDtypeStruct((PEER_HEADS, PEER_NKEYS, t), F32)
    ospec = pl.BlockSpec((None, PEER_NKEYS, tn), lambda i, h: (h, 0, i))
    return pl.pallas_call(
        _peer_score_kernel,
        grid=(t // tn, PEER_HEADS),
        in_specs=[pl.BlockSpec((tn, LANES), lambda i, h: (i, h)),
                  pl.BlockSpec((None, 2 * PEER_NKEYS, LANES), lambda i, h: (h, 0, 0))],
        out_specs=[ospec] * 4,
        out_shape=[out] * 4,
        compiler_params=_params("parallel", "parallel"),
        name="peer_scores",
    )(q, keys)


def _peer_dense_kernel(xt_ref, u_ref, vt_ref, cb_ref, eb_ref, na_ref, wa_ref, o_ref,
                       coef_even, coef_odd, *, te):
    e = pl.program_id(1)
    last = pl.num_programs(1) - 2
    tn = o_ref.shape[1]
    half = tn // 2

    @pl.when(e == 0)
    def _():
        o_ref[...] = jnp.zeros_like(o_ref)
        coef_odd[...] = jnp.zeros_like(coef_odd)

    def gate_into_coef(hid, lane0, coef_out):
        chunk = 16
        for di in range(te // PEER_NKEYS):
            i = jnp.minimum(e, last) * (te // PEER_NKEYS) + di
            for sub in range(0, hid.shape[1], LANES):
                lanes = slice(lane0 + sub, lane0 + sub + LANES)
                n_rows = [na_ref[i, h:h + 1, lanes] for h in range(PEER_HEADS)]
                w_rows = [wa_ref[i, h:h + 1, lanes] for h in range(PEER_HEADS)]
                for r in range(0, PEER_NKEYS, chunk):
                    rs = slice(r, r + chunk)
                    w = None
                    for h in range(PEER_HEADS):
                        term = jnp.where(cb_ref[h, rs, lanes] < n_rows[h],
                                         eb_ref[h, rs, lanes] * w_rows[h], 0.0)
                        w = term if w is None else w + term
                    row0 = di * PEER_NKEYS + r
                    x = hid[row0:row0 + chunk, sub:sub + LANES]
                    act = 0.5 * x * (1.0 + lax.erf(x * (2.0 ** -0.5)))
                    coef_out[row0:row0 + chunk, lanes] = (act * w).astype(BF16)

    def step(coef_out, coef_in):
        lo, hi = slice(0, half), slice(half, tn)
        hid_lo = jnp.dot(u_ref[...], xt_ref[:, lo], preferred_element_type=F32)
        o_ref[:, lo] += jnp.dot(vt_ref[...], coef_in[:, lo], preferred_element_type=F32)
        hid_hi = jnp.dot(u_ref[...], xt_ref[:, hi], preferred_element_type=F32)
        gate_into_coef(hid_lo, 0, coef_out)
        o_ref[:, hi] += jnp.dot(vt_ref[...], coef_in[:, hi], preferred_element_type=F32)
        gate_into_coef(hid_hi, half, coef_out)

    @pl.when((e & 1) == 0)
    def _():
        step(coef_even, coef_odd)

    @pl.when((e & 1) == 1)
    def _():
        step(coef_odd, coef_even)


def peer_dense(xt, u, vt, cb, eb, na, wa, tn=512, te=256):
    d, t = xt.shape
    n_exp = u.shape[0]
    tn, te = min(tn, t), min(te, n_exp)
    n_e = n_exp // te
    sspec = pl.BlockSpec((PEER_HEADS, PEER_NKEYS, tn), lambda i, e: (0, 0, i))
    aspec = pl.BlockSpec((PEER_NKEYS, PEER_HEADS, tn), lambda i, e: (0, 0, i))
    na, wa = na.transpose(1, 0, 2), wa.transpose(1, 0, 2)
    return pl.pallas_call(
        functools.partial(_peer_dense_kernel, te=te),
        grid=(t // tn, n_e + 1),
        in_specs=[pl.BlockSpec((d, tn), lambda i, e: (0, i)),
                  pl.BlockSpec((te, d), lambda i, e: (jnp.minimum(e, n_e - 1), 0)),
                  pl.BlockSpec((d, te), lambda i, e: (0, jnp.maximum(e - 1, 0))),
                  sspec, sspec, aspec, aspec],
        out_specs=pl.BlockSpec((d, tn), lambda i, e: (0, i)),
        out_shape=jax.ShapeDtypeStruct((d, t), F32),
        scratch_shapes=[pltpu.VMEM((te, tn), BF16), pltpu.VMEM((te, tn), BF16)],
        compiler_params=_params("parallel", "arbitrary"),
        name="peer_dense",
    )(xt, u, vt, cb, eb, na, wa)


def kernel(x, p, positions, norm_mix, w_in, qn_moba, kn_moba, qn_diff, kn_diff, lambda_q1, lambda_k1, lambda_q2, lambda_k2, subln_diff, w_branch, w_out, norm_ffn, peer_wq, peer_key_a, peer_key_b, peer_u, peer_v, norm_ple, w_ple_gate, w_ple_proj):
    b, s, d = x.shape
    t = b * s
    depth = w_in.shape[0]
    nb = s // MOBA_BLOCK
    gate_col = 9 * MIX_WIDTH
    h = x.reshape(t, d)
    p_flat = p.reshape(depth, t, -1)
    for i in range(depth):
        n1 = rmsnorm(h, norm_mix[i])
        proj = matmul(n1, w_in, i)
        (qm, qmf, km, vm, qd, kd, vd, qs, ks, vs, kmean) = prep_qkv(
            proj, positions, qn_moba[i], kn_moba[i], qn_diff[i], kn_diff[i])
        kmean = kmean.reshape(b, nb, N_HEADS, HEAD_DIM).transpose(0, 2, 1, 3)
        kmean = jnp.pad(kmean, ((0, 0), (0, 0), (0, -nb % 8), (0, 0)))
        o_m = moba_attention(qm, qmf, kmean, km, vm, b, s)
        lam_init = 0.8 - 0.6 * math.exp(-0.3 * i)
        o_d = diff_attention(qd, kd, vd, lambda_q1[i], lambda_k1[i], lambda_q2[i], lambda_k2[i],
                             subln_diff[i], lam_init, b, s)
        o_s = stick_attention(qs, ks, vs, b, s)
        merged = merge_branches(o_m, o_d, o_s, w_branch, i, proj, gate_col)
        h = matmul(merged, w_out, i, mode="residual", res=h)
        n2, n2_t = rmsnorm(h, norm_ffn[i], with_transpose=True)
        q = matmul(n2, peer_wq, i)
        cb, eb, na, wa = peer_scores(q, peer_key_a[i], peer_key_b[i])
        ffn_t = peer_dense(n2_t, peer_u[i].astype(BF16), transpose_to_bf16(peer_v, i), cb, eb, na, wa)
        h = h + ffn_t.T
        n3 = rmsnorm(h, norm_ple[i])
        h = matmul(n3, w_ple_gate, i, mode="ple", res=h, p=p_flat, wp=w_ple_proj)
    return h.reshape(b, s, d)
```

```python
import functools
import math

import jax
import jax.numpy as jnp
from jax import lax
from jax.experimental import pallas as pl
from jax.experimental.pallas import tpu as pltpu

F32 = jnp.float32
BF16 = jnp.bfloat16

LANES = 128
HEAD_DIM = 128
N_HEADS = 8
MIX_WIDTH = N_HEADS * HEAD_DIM
DIFF_SUB = HEAD_DIM // 2
ROPE_THETA = 500000.0
ROT_HALF = HEAD_DIM // 8
ROT_HALF_DIFF = DIFF_SUB // 8
MOBA_BLOCK = 256
MOBA_TOPK = 3
PEER_HEADS = 8
PEER_NKEYS = 128
PEER_TOPK = 16
NORM_EPS = 1e-6
NEG_INF = -1e30
VMEM_LIMIT = 56 * 1024 * 1024

_NT = (((1,), (1,)), ((), ()))


def _params(*sem):
    return pltpu.CompilerParams(dimension_semantics=sem, vmem_limit_bytes=VMEM_LIMIT)


def _rmsnorm_kernel(x_ref, g_ref, o_ref, *maybe_ot_ref):
    x = x_ref[...]
    ms = jnp.mean(x * x, axis=-1, keepdims=True)
    y = x * lax.rsqrt(ms + NORM_EPS) * g_ref[...]
    o_ref[...] = y.astype(o_ref.dtype)
    for ot_ref in maybe_ot_ref:
        ot_ref[...] = y.T.astype(ot_ref.dtype)


def rmsnorm(x, gain, tm=256, with_transpose=False):
    t, d = x.shape
    tm = min(tm, t)
    out_specs = [pl.BlockSpec((tm, d), lambda i: (i, 0))]
    out_shape = [jax.ShapeDtypeStruct((t, d), BF16)]
    if with_transpose:
        out_specs.append(pl.BlockSpec((d, tm), lambda i: (0, i)))
        out_shape.append(jax.ShapeDtypeStruct((d, t), BF16))
    outs = pl.pallas_call(
        _rmsnorm_kernel,
        grid=(t // tm,),
        in_specs=[pl.BlockSpec((tm, d), lambda i: (i, 0)),
                  pl.BlockSpec((1, d), lambda i: (0, 0))],
        out_specs=out_specs,
        out_shape=out_shape,
        compiler_params=_params("parallel"),
        name="rmsnorm_t" if with_transpose else "rmsnorm",
    )(x, gain.reshape(1, d))
    return outs if with_transpose else outs[0]


def _relayout_kernel(x_ref, o_ref, *, transpose):
    x = x_ref[...]
    o_ref[...] = (x.T if transpose else x).astype(o_ref.dtype)


def to_bf16(x, layer, transpose=False, tb=1024):
    _, r, c = x.shape
    tr, tc = min(tb, r), min(tb, c)
    if transpose:
        out_spec, out_dims = pl.BlockSpec((tc, tr), lambda i, j: (j, i)), (c, r)
    else:
        out_spec, out_dims = pl.BlockSpec((tr, tc), lambda i, j: (i, j)), (r, c)
    return pl.pallas_call(
        functools.partial(_relayout_kernel, transpose=transpose),
        grid=(r // tr, c // tc),
        in_specs=[pl.BlockSpec((None, tr, tc), lambda i, j: (layer, i, j))],
        out_specs=out_spec,
        out_shape=jax.ShapeDtypeStruct(out_dims, BF16),
        compiler_params=_params("parallel", "parallel"),
        name="to_bf16_t" if transpose else "to_bf16",
    )(x)


def _mm_kernel(*refs, mode):
    if mode == "plain":
        a_ref, b_ref, o_ref = refs
    elif mode == "residual":
        a_ref, b_ref, r_ref, o_ref = refs
    else:
        a_ref, b_ref, r_ref, p_ref, wp_ref, o_ref = refs
    acc = jnp.dot(a_ref[...], b_ref[...].astype(BF16), preferred_element_type=F32)
    if mode == "plain":
        o_ref[...] = acc.astype(o_ref.dtype)
    elif mode == "residual":
        o_ref[...] = (r_ref[...] + acc).astype(o_ref.dtype)
    else:
        emb = jnp.dot(p_ref[...].astype(BF16), wp_ref[...].astype(BF16),
                      preferred_element_type=F32)
        o_ref[...] = (r_ref[...] + jax.nn.sigmoid(acc) * emb).astype(o_ref.dtype)


def matmul(a, w, layer, *, mode="plain", res=None, p=None, wp=None, out_dtype=F32, tm=2048, tn=None):
    m, kdim = a.shape
    n = w.shape[2]
    if tn is None:
        tn = 512 if mode == "plain" else 256
    tm, tn = min(tm, m), min(tn, n)
    in_specs = [pl.BlockSpec((tm, kdim), lambda i, j: (i, 0), pipeline_mode=pl.Buffered(1)),
                pl.BlockSpec((None, kdim, tn), lambda i, j: (layer, 0, j))]
    args = [a, w]
    if mode in ("residual", "ple"):
        in_specs.append(pl.BlockSpec((tm, tn), lambda i, j: (i, j)))
        args.append(res)
    if mode == "ple":
        pd = p.shape[2]
        in_specs += [pl.BlockSpec((None, tm, pd), lambda i, j: (layer, i, 0)),
                     pl.BlockSpec((None, pd, tn), lambda i, j: (layer, 0, j))]
        args += [p, wp]
    return pl.pallas_call(
        functools.partial(_mm_kernel, mode=mode),
        grid=(m // tm, n // tn),
        in_specs=in_specs,
        out_specs=pl.BlockSpec((tm, tn), lambda i, j: (i, j)),
        out_shape=jax.ShapeDtypeStruct((m, n), out_dtype),
        compiler_params=_params("parallel", "arbitrary"),
        name="matmul_" + mode,
    )(*args)


def _rope(x, cos, sin, half, period):
    lane = lax.broadcasted_iota(jnp.int32, x.shape, 1)
    first = (lane & (period - 1)) < half
    partner = jnp.where(first, pltpu.roll(x, LANES - half, 1), pltpu.roll(x, half, 1))
    return x * cos + partner * jnp.where(first, -sin, sin)


def _prep_kernel(pos_ref, fm_ref, fd_ref, qng_ref, kng_ref, qdg_ref, kdg_ref,
                 qm_i, km_i, qd_i, kd_i, qm_o, qmf_o, km_o, qd_o, kd_o, kmean_o):
    pos = pos_ref[...].astype(F32)
    ang_m = pos * fm_ref[...]
    ang_d = pos * fd_ref[...]
    cos_m, sin_m = jnp.cos(ang_m), jnp.sin(ang_m)
    cos_d, sin_d = jnp.cos(ang_d), jnp.sin(ang_d)
    lane = lax.broadcasted_iota(jnp.int32, ang_m.shape, 1)
    low = lane < DIFF_SUB

    def norm_full(x, g):
        ms = jnp.mean(x * x, axis=-1, keepdims=True)
        return x * lax.rsqrt(ms + NORM_EPS) * g

    def norm_halves(x, g):
        xx = x * x
        lo = jnp.sum(jnp.where(low, xx, 0.0), axis=-1, keepdims=True)
        hi = jnp.sum(jnp.where(low, 0.0, xx), axis=-1, keepdims=True)
        ms = jnp.where(low, lo, hi) * (1.0 / DIFF_SUB)
        return x * lax.rsqrt(ms + NORM_EPS) * g

    for h in range(N_HEADS):
        sl = slice(h * HEAD_DIM, (h + 1) * HEAD_DIM)
        qm = _rope(norm_full(qm_i[:, sl].astype(F32), qng_ref[...]), cos_m, sin_m, ROT_HALF, HEAD_DIM)
        km = _rope(norm_full(km_i[:, sl].astype(F32), kng_ref[...]), cos_m, sin_m, ROT_HALF, HEAD_DIM)
        qmf_o[:, sl] = qm
        qm_o[:, sl] = qm.astype(BF16)
        km_o[:, sl] = km.astype(BF16)
        kmean_o[0, h:h + 1, :] = jnp.mean(km, axis=0, keepdims=True)
        qd = _rope(norm_halves(qd_i[:, sl].astype(F32), qdg_ref[...]), cos_d, sin_d,
                   ROT_HALF_DIFF, DIFF_SUB)
        kd = _rope(norm_halves(kd_i[:, sl].astype(F32), kdg_ref[...]), cos_d, sin_d,
                   ROT_HALF_DIFF, DIFF_SUB)
        qd_o[:, sl] = qd.astype(BF16)
        kd_o[:, sl] = kd.astype(BF16)


def _rope_table(half, period):
    lane = jnp.arange(LANES)
    inv_freq = ROPE_THETA ** (-jnp.arange(half, dtype=F32) / half)
    tab = jnp.where((lane % period) < 2 * half, inv_freq[lane % half], 0.0)
    return tab.reshape(1, LANES).astype(F32)


def prep_qkv(proj, positions, qn_m, kn_m, qn_d, kn_d):
    t = proj.shape[0]
    tm = MOBA_BLOCK
    w = MIX_WIDTH
    row = lambda i: (i, 0)
    col_spec = [pl.BlockSpec((tm, w), functools.partial(lambda i, c: (i, c), c=c)) for c in (0, 1, 3, 4)]
    vec = pl.BlockSpec((1, LANES), lambda i: (0, 0))
    bf = jax.ShapeDtypeStruct((t, w), BF16)
    out_shape = [bf, jax.ShapeDtypeStruct((t, w), F32), bf, bf, bf,
                 jax.ShapeDtypeStruct((t // tm, N_HEADS, HEAD_DIM), F32)]
    out_specs = [pl.BlockSpec((tm, w), row)] * 5 + [
        pl.BlockSpec((1, N_HEADS, HEAD_DIM), lambda i: (i, 0, 0))]
    return pl.pallas_call(
        _prep_kernel,
        grid=(t // tm,),
        in_specs=[pl.BlockSpec((tm, 1), row), vec, vec, vec, vec, vec, vec] + col_spec,
        out_specs=out_specs,
        out_shape=out_shape,
        compiler_params=_params("parallel"),
        name="prep_qkv",
    )(positions.reshape(t, 1), _rope_table(ROT_HALF, HEAD_DIM), _rope_table(ROT_HALF_DIFF, DIFF_SUB),
      qn_m.reshape(1, LANES), kn_m.reshape(1, LANES),
      jnp.tile(qn_d, 2).reshape(1, LANES), jnp.tile(kn_d, 2).reshape(1, LANES),
      *([proj] * 4))


def _ones_column(rows, at=0):
    return (lax.broadcasted_iota(jnp.int32, (rows, LANES), 1) == at).astype(BF16)


def _value_with_ones(v):
    return jnp.concatenate([v, jnp.ones(v.shape, BF16)], axis=1)


def _moba_kernel(q_ref, qf_ref, kmean_ref, k_ref, v_ref, o_ref, *, tk):
    blk = MOBA_BLOCK
    qi = pl.program_id(2)
    scale = HEAD_DIM ** -0.5
    big = NEG_INF / scale
    nbp = kmean_ref.shape[0]

    gate = lax.dot_general(kmean_ref[...], qf_ref[...], _NT, precision=lax.Precision.HIGHEST,
                           preferred_element_type=F32)
    rid = lax.broadcasted_iota(jnp.int32, gate.shape, 0)
    rank = jnp.zeros(gate.shape, jnp.int32)
    for m in range(nbp):
        g_m = gate[m:m + 1, :]
        beats = (g_m > gate) | ((g_m == gate) & (rid > m))
        rank = rank + jnp.where(beats & (m < qi), 1, 0)
    allowed = ((rid < qi) & (rank < MOBA_TOPK)) | (rid == qi)
    pen_t = jnp.where(allowed, 0.0, big)
    pen_t = jnp.concatenate([pen_t, jnp.full((LANES - nbp, blk), big, F32)], axis=0)
    penalty = pen_t.T

    q = q_ref[...]
    lane = lax.broadcasted_iota(jnp.int32, (blk, LANES), 1)
    q_pos = qi * blk + lax.broadcasted_iota(jnp.int32, (blk, tk), 0)
    col = lax.broadcasted_iota(jnp.int32, (blk, tk), 1)
    krow = lax.broadcasted_iota(jnp.int32, (tk, LANES), 0)
    klane = lax.broadcasted_iota(jnp.int32, (tk, LANES), 1)
    n_tiles = (qi * blk) // tk + 1

    def scores(n, q_aug, shift_lane):
        off = pl.multiple_of(n * tk, tk)
        kb = k_ref[pl.ds(off, tk), :]
        block_of_row = n * (tk // blk) + krow // blk
        marks = ((klane == block_of_row) | (klane == shift_lane)).astype(BF16)
        s = lax.dot_general(q_aug, jnp.concatenate([kb, marks], axis=1), _NT,
                            preferred_element_type=F32) * scale
        return jnp.where(off + col <= q_pos, s, NEG_INF), off

    q_max = jnp.concatenate([q, penalty.astype(BF16)], axis=1)

    def sweep_max(n, best):
        s, _ = scores(n, q_max, -1)
        return jnp.maximum(best, s)

    best = lax.fori_loop(0, n_tiles, sweep_max, jnp.full((blk, tk), NEG_INF, F32))
    row_max = jnp.max(best, axis=-1, keepdims=True)
    shift_lane = LANES - 1
    q_sum = jnp.concatenate(
        [q, jnp.where(lane == shift_lane, -row_max / scale, penalty).astype(BF16)], axis=1)

    def sweep_sum(n, acc):
        s, off = scores(n, q_sum, shift_lane)
        vb = v_ref[pl.ds(off, tk), :]
        return acc + jnp.dot(jnp.exp(s).astype(BF16), _value_with_ones(vb), preferred_element_type=F32)

    acc = lax.fori_loop(0, n_tiles, sweep_sum, jnp.zeros((blk, 2 * HEAD_DIM), F32))
    o_ref[...] = (acc[:, :HEAD_DIM] / acc[:, HEAD_DIM:]).astype(o_ref.dtype)


def moba_attention(q, qf, kmean, k, v, v_group, batch, seq, tk=512):
    t = q.shape[0]
    blk = MOBA_BLOCK
    nq = seq // blk
    tk = min(tk, seq)
    qspec = pl.BlockSpec((blk, HEAD_DIM), lambda b, h, i: (b * nq + i, h))
    kspec = pl.BlockSpec((seq, HEAD_DIM), lambda b, h, i: (b, h))
    vspec = pl.BlockSpec((seq, HEAD_DIM), lambda b, h, i: (b, v_group * N_HEADS + h))
    return pl.pallas_call(
        functools.partial(_moba_kernel, tk=tk),
        grid=(batch, N_HEADS, nq),
        in_specs=[qspec, qspec,
                  pl.BlockSpec((None, None, kmean.shape[2], HEAD_DIM), lambda b, h, i: (b, h, 0, 0)),
                  kspec, vspec],
        out_specs=qspec,
        out_shape=jax.ShapeDtypeStruct((t, MIX_WIDTH), BF16),
        compiler_params=_params("parallel", "parallel", "arbitrary"),
        name="moba_attention",
    )(q, qf, kmean, k, v)


def _diff_kernel(lq1_ref, lk1_ref, lq2_ref, lk2_ref, g_ref, q_ref, k_ref, v_ref, o_ref, *, tq, tk, lam_init):
    qi = pl.program_id(2)
    q = q_ref[...] * (DIFF_SUB ** -0.5)
    lane = lax.broadcasted_iota(jnp.int32, q.shape, 1)
    zero = jnp.zeros_like(q)
    qq = jnp.concatenate([jnp.where(lane < DIFF_SUB, q, zero),
                          jnp.where(lane < DIFF_SUB, zero, q)], axis=0)
    rows = 2 * tq
    q_pos = qi * tq + (lax.broadcasted_iota(jnp.int32, (rows, tk), 0) & (tq - 1))
    col = lax.broadcasted_iota(jnp.int32, (rows, tk), 1)
    n_full = (qi * tq) // tk
    last = pl.multiple_of(n_full * tk, tk)
    causal = last + col <= q_pos

    def lane_tile_max(s):
        out = s[:, :LANES]
        for c in range(LANES, tk, LANES):
            out = jnp.maximum(out, s[:, c:c + LANES])
        return out

    def sweep_max(n, best):
        kb = k_ref[pl.ds(pl.multiple_of(n * tk, tk), tk), :]
        return jnp.maximum(best, lane_tile_max(lax.dot_general(qq, kb, _NT, preferred_element_type=F32)))

    best = lax.fori_loop(0, n_full, sweep_max, jnp.full((rows, LANES), NEG_INF, F32))
    s_last = lax.dot_general(qq, k_ref[pl.ds(last, tk), :], _NT, preferred_element_type=F32)
    best = jnp.maximum(best, lane_tile_max(jnp.where(causal, s_last, NEG_INF)))
    row_max = jnp.max(best, axis=-1, keepdims=True)

    lane_r = lax.broadcasted_iota(jnp.int32, (rows, LANES), 1)
    q_aug = jnp.concatenate([qq, jnp.where(lane_r == 0, -row_max, 0.0).astype(BF16)], axis=1)
    ones_col = _ones_column(tk)

    def tile_sum(off, mask):
        k_aug = jnp.concatenate([k_ref[pl.ds(off, tk), :], ones_col], axis=1)
        v_aug = _value_with_ones(v_ref[pl.ds(off, tk), :])
        outs = []
        for half in range(2):
            s = lax.dot_general(q_aug[half * tq:(half + 1) * tq], k_aug, _NT,
                                preferred_element_type=F32)
            if mask is not None:
                s = jnp.where(mask[half * tq:(half + 1) * tq], s, NEG_INF)
            outs.append(jnp.dot(jnp.exp(s).astype(BF16), v_aug, preferred_element_type=F32))
        return jnp.concatenate(outs, axis=0)

    def sweep_sum(n, acc):
        return acc + tile_sum(pl.multiple_of(n * tk, tk), None)

    acc = lax.fori_loop(0, n_full, sweep_sum, jnp.zeros((rows, 2 * HEAD_DIM), F32))
    acc = acc + tile_sum(last, causal)

    lam = (jnp.exp(jnp.sum(lq1_ref[...] * lk1_ref[...], axis=-1, keepdims=True))
           - jnp.exp(jnp.sum(lq2_ref[...] * lk2_ref[...], axis=-1, keepdims=True)) + lam_init)
    out = acc[:, :HEAD_DIM] / acc[:, HEAD_DIM:]
    o = out[:tq] - lam * out[tq:]
    ms = jnp.mean(o * o, axis=-1, keepdims=True)
    o_ref[...] = (o * lax.rsqrt(ms + NORM_EPS) * g_ref[...] * (1.0 - lam_init)).astype(o_ref.dtype)


def diff_attention(q, k, v, v_group, lq1, lk1, lq2, lk2, subln, lam_init, batch, seq, tq=256, tk=512):
    t = q.shape[0]
    tq, tk = min(tq, seq), min(tk, seq)
    nq = seq // tq
    qspec = pl.BlockSpec((tq, HEAD_DIM), lambda b, h, i: (b * nq + i, h))
    kspec = pl.BlockSpec((seq, HEAD_DIM), lambda b, h, i: (b, h))
    vspec = pl.BlockSpec((seq, HEAD_DIM), lambda b, h, i: (b, v_group * N_HEADS + h))
    lspec = pl.BlockSpec((1, DIFF_SUB), lambda b, h, i: (0, 0))
    return pl.pallas_call(
        functools.partial(_diff_kernel, tq=tq, tk=tk, lam_init=lam_init),
        grid=(batch, N_HEADS, nq),
        in_specs=[lspec, lspec, lspec, lspec,
                  pl.BlockSpec((1, HEAD_DIM), lambda b, h, i: (0, 0)), qspec, kspec, vspec],
        out_specs=qspec,
        out_shape=jax.ShapeDtypeStruct((t, MIX_WIDTH), BF16),
        compiler_params=_params("parallel", "parallel", "arbitrary"),
        name="diff_attention",
    )(lq1.reshape(1, DIFF_SUB), lk1.reshape(1, DIFF_SUB), lq2.reshape(1, DIFF_SUB),
      lk2.reshape(1, DIFF_SUB), subln.reshape(1, HEAD_DIM), q, k, v)


def _stick_kernel(q_ref, k_ref, v_ref, o_ref, *, tq, tk):
    qi = pl.program_id(2)
    scale = HEAD_DIM ** -0.5
    n_heads = q_ref.shape[1] // HEAD_DIM
    upper = (lax.broadcasted_iota(jnp.int32, (tk, tk), 0)
             > lax.broadcasted_iota(jnp.int32, (tk, tk), 1)).astype(BF16)
    q_pos = qi * tq + lax.broadcasted_iota(jnp.int32, (tq, tk), 0)
    col = lax.broadcasted_iota(jnp.int32, (tq, tk), 1)
    n_diag = tq // tk
    n_past = qi * n_diag

    def tile(n, hd, later, acc, masked):
        off = pl.multiple_of(n * tk, tk)
        cols = slice(hd * HEAD_DIM, (hd + 1) * HEAD_DIM)
        kb = k_ref[pl.ds(off, tk), cols]
        vb = v_ref[pl.ds(off, tk), cols]
        z = lax.dot_general(q_ref[:, cols], kb, _NT, preferred_element_type=F32) * scale
        sp = jnp.maximum(z, 0.0) + jnp.log(1.0 + jnp.exp(-jnp.abs(z)))
        if masked:
            strict = (off + col) < q_pos
            sp = jnp.where(strict, sp, 0.0)
        hi = sp.astype(BF16)
        lo = (sp - hi.astype(F32)).astype(BF16)
        between = (jnp.dot(hi, upper, preferred_element_type=F32)
                   + jnp.dot(lo, upper, preferred_element_type=F32))
        a = jnp.exp(z - sp - between - later)
        if masked:
            a = jnp.where(strict, a, 0.0)
        acc = acc + jnp.dot(a.astype(BF16), vb, preferred_element_type=F32)
        return later + jnp.sum(sp, axis=-1, keepdims=True), acc

    state = [(jnp.zeros((tq, 1), F32), jnp.zeros((tq, HEAD_DIM), F32)) for _ in range(n_heads)]
    for d in range(n_diag):
        state = [tile(n_past + n_diag - 1 - d, hd, *state[hd], True) for hd in range(n_heads)]

    def body(it, carry):
        return tuple(tile(n_past - 1 - it, hd, *carry[hd], False) for hd in range(n_heads))

    state = lax.fori_loop(0, n_past, body, tuple(state))
    for hd in range(n_heads):
        o_ref[:, hd * HEAD_DIM:(hd + 1) * HEAD_DIM] = state[hd][1].astype(o_ref.dtype)


def stick_attention(qkv, q_group, batch, seq, tq=512, tk=256, heads_per_step=2):
    t = qkv.shape[0]
    tq, tk = min(tq, seq), min(tk, seq)
    nq = seq // tq
    width = heads_per_step * HEAD_DIM
    per_group = MIX_WIDTH // width

    def col(group):
        return lambda b, h, i: (b, group * per_group + h)

    qspec = pl.BlockSpec((tq, width), lambda b, h, i: (b * nq + i, q_group * per_group + h))
    kspec = pl.BlockSpec((seq, width), col(q_group + 1))
    vspec = pl.BlockSpec((seq, width), col(q_group + 2))
    return pl.pallas_call(
        functools.partial(_stick_kernel, tq=tq, tk=tk),
        grid=(batch, N_HEADS // heads_per_step, nq),
        in_specs=[qspec, kspec, vspec],
        out_specs=pl.BlockSpec((tq, width), lambda b, h, i: (b * nq + i, h)),
        out_shape=jax.ShapeDtypeStruct((t, MIX_WIDTH), BF16),
        compiler_params=_params("parallel", "parallel", "arbitrary"),
        name="stick_attention",
    )(qkv, qkv, qkv)


def _merge_kernel(om_ref, od_ref, os_ref, w_ref, gm_ref, gd_ref, gs_ref, o_ref):
    acc = None
    for k, (g_ref, x_ref) in enumerate(((gm_ref, om_ref), (gd_ref, od_ref), (gs_ref, os_ref))):
        term = jax.nn.sigmoid(g_ref[...].astype(F32)) * jnp.dot(
            x_ref[...], w_ref[k].astype(BF16), preferred_element_type=F32)
        acc = term if acc is None else acc + term
    o_ref[...] = acc.astype(o_ref.dtype)


def merge_branches(o_m, o_d, o_s, w_branch, layer, proj, gate_col, tm=1024, tn=512):
    t, kdim = o_m.shape
    d = w_branch.shape[3]
    tm, tn = min(tm, t), min(tn, d)
    ospec = pl.BlockSpec((tm, kdim), lambda i, j: (i, 0))
    gspecs = [pl.BlockSpec((tm, tn), functools.partial(lambda i, j, c: (i, c + j), c=(gate_col + c * d) // tn))
              for c in range(3)]
    return pl.pallas_call(
        _merge_kernel,
        grid=(t // tm, d // tn),
        in_specs=[ospec, ospec, ospec,
                  pl.BlockSpec((None, 3, kdim, tn), lambda i, j: (layer, 0, 0, j))] + gspecs,
        out_specs=pl.BlockSpec((tm, tn), lambda i, j: (i, j)),
        out_shape=jax.ShapeDtypeStruct((t, d), BF16),
        compiler_params=_params("parallel", "parallel"),
        name="merge_branches",
    )(o_m, o_d, o_s, w_branch, proj, proj, proj)


def _extract_top(x, rounds, break_ties):
    rows, n = x.shape
    rid = lax.broadcasted_iota(jnp.int32, (rows, n), 0)
    kid = lax.broadcasted_iota(jnp.int32, (rounds, n), 0)

    def body(k, carry):
        x, rank, vals = carry
        best = jnp.max(x, axis=0, keepdims=True)
        hit = x == best
        if break_ties:
            hit = rid == jnp.min(jnp.where(hit, rid, rows), axis=0, keepdims=True)
        rank = jnp.where(hit, k, rank)
        x = jnp.where(hit, -jnp.inf, x)
        vals = jnp.where(kid == k, best, vals)
        return x, rank, vals

    init = (x, jnp.full((rows, n), rounds, jnp.int32), jnp.zeros((rounds, n), F32))
    _, rank, vals = lax.fori_loop(0, rounds, body, init)
    return vals, rank


def _peer_score_kernel(q_ref, keys_ref, cb_o, eb_o, na_o, wa_o):
    nk, topk = PEER_NKEYS, PEER_TOPK
    s = lax.dot_general(keys_ref[...], q_ref[...], _NT, precision=lax.Precision.HIGHEST,
                        preferred_element_type=F32)

    def taken_off(rank):
        taken = jnp.sum((rank < topk).astype(jnp.int32), axis=0, keepdims=True)
        return jnp.max(jnp.where(taken != topk, 1, 0))

    suspect = _peer_select(s, cb_o, eb_o, na_o, wa_o, False, taken_off)

    @pl.when(suspect > 0)
    def _():
        _peer_select(s, cb_o, eb_o, na_o, wa_o, True, taken_off)


def _peer_select(s, cb_o, eb_o, na_o, wa_o, break_ties, taken_off):
    nk, topk = PEER_NKEYS, PEER_TOPK
    sa, sb = s[:nk], s[nk:]
    va, rank_a = _extract_top(sa, topk, break_ties)
    vb, rank_b = _extract_top(sb, topk, break_ties)
    n = s.shape[1]
    row8 = lax.broadcasted_iota(jnp.int32, (8, n), 0)
    ninf = -jnp.inf
    a = [va[r:r + 1] for r in range(topk)]
    groups = [a[0] + vb]
    for r, lim in ((1, 8), (2, 5), (3, 4), (4, 3)):
        groups.append(jnp.where(row8 < lim, a[r] + vb[0:8], ninf))
    a_mix = jnp.where(row8 < 2, a[5], jnp.where(row8 < 4, a[6], jnp.where(
        row8 < 6, a[7], jnp.where(row8 == 6, a[8], a[9]))))
    b_mix = jnp.where((row8 < 6) & ((row8 & 1) == 1), vb[1:2], vb[0:1])
    groups.append(a_mix + b_mix)
    tail = jnp.full((8, n), ninf, F32)
    for k in range(6):
        tail = jnp.where(row8 == k, a[10 + k] + vb[0:1], tail)
    groups.append(tail)
    cand = jnp.concatenate(groups, axis=0)
    top_s, rank_c = _extract_top(cand, topk, break_ties)
    taken = (rank_c < topk).astype(jnp.int32)
    counts = [jnp.sum(taken[0:16], axis=0, keepdims=True)]
    counts += [jnp.sum(taken[8 + 8 * r:16 + 8 * r], axis=0, keepdims=True) for r in range(1, 5)]
    counts += [taken[48 + 2 * k:49 + 2 * k] + taken[49 + 2 * k:50 + 2 * k] for k in range(3)]
    counts += [taken[54 + k:55 + k] for k in range(8)]
    count = jnp.concatenate(counts, axis=0)
    z = jnp.sum(jnp.exp(top_s - top_s[0:1]), axis=0, keepdims=True)
    na = jnp.zeros(sa.shape, jnp.int32)
    for r in range(topk):
        na = jnp.where(rank_a == r, count[r:r + 1], na)
    cb_o[...] = rank_b.astype(F32)
    eb_o[...] = jnp.exp(sb - vb[0:1])
    na_o[...] = na.astype(F32)
    wa_o[...] = jnp.exp(sa - va[0:1]) / z
    return taken_off(rank_a) + taken_off(rank_b) + taken_off(rank_c)


def peer_scores(q, key_a, key_b, tn=256):
    t = q.shape[0]
    tn = min(tn, t)
    half = key_a.shape[-1]
    zeros = jnp.zeros_like(key_a)
    keys = jnp.concatenate([jnp.concatenate([key_a, zeros], axis=-1),
                            jnp.concatenate([zeros, key_b], axis=-1)], axis=1)
    del half
    out = jax.ShapeDtypeStruct((PEER_HEADS, PEER_NKEYS, t), F32)
    ospec = pl.BlockSpec((None, PEER_NKEYS, tn), lambda i, h: (h, 0, i))
    return pl.pallas_call(
        _peer_score_kernel,
        grid=(t // tn, PEER_HEADS),
        in_specs=[pl.BlockSpec((tn, LANES), lambda i, h: (i, h)),
                  pl.BlockSpec((None, 2 * PEER_NKEYS, LANES), lambda i, h: (h, 0, 0))],
        out_specs=[ospec] * 4,
        out_shape=[out] * 4,
        compiler_params=_params("parallel", "parallel"),
        name="peer_scores",
    )(q, keys)


def _peer_dense_kernel(xt_ref, u_ref, vt_ref, cb_ref, eb_ref, na_ref, wa_ref, o_ref,
                       coef_even, coef_odd, *, te):
    e = pl.program_id(1)
    last = pl.num_programs(1) - 2
    tn = o_ref.shape[1]
    half = tn // 2

    @pl.when(e == 0)
    def _():
        o_ref[...] = jnp.zeros_like(o_ref)
        coef_odd[...] = jnp.zeros_like(coef_odd)

    def gate_into_coef(hid, lane0, coef_out):
        chunk = 16
        for di in range(te // PEER_NKEYS):
            i = jnp.minimum(e, last) * (te // PEER_NKEYS) + di
            for sub in range(0, hid.shape[1], LANES):
                lanes = slice(lane0 + sub, lane0 + sub + LANES)
                n_rows = [na_ref[i, h:h + 1, lanes] for h in range(PEER_HEADS)]
                w_rows = [wa_ref[i, h:h + 1, lanes] for h in range(PEER_HEADS)]
                for r in range(0, PEER_NKEYS, chunk):
                    rs = slice(r, r + chunk)
                    w = None
                    for h in range(PEER_HEADS):
                        term = jnp.where(cb_ref[h, rs, lanes] < n_rows[h],
                                         eb_ref[h, rs, lanes] * w_rows[h], 0.0)
                        w = term if w is None else w + term
                    row0 = di * PEER_NKEYS + r
                    x = hid[row0:row0 + chunk, sub:sub + LANES]
                    act = 0.5 * x * (1.0 + lax.erf(x * (2.0 ** -0.5)))
                    coef_out[row0:row0 + chunk, lanes] = (act * w).astype(BF16)

    def step(coef_out, coef_in):
        lo, hi = slice(0, half), slice(half, tn)
        hid_lo = jnp.dot(u_ref[...], xt_ref[:, lo], preferred_element_type=F32)
        o_ref[:, lo] += jnp.dot(vt_ref[...], coef_in[:, lo], preferred_element_type=F32)
        hid_hi = jnp.dot(u_ref[...], xt_ref[:, hi], preferred_element_type=F32)
        gate_into_coef(hid_lo, 0, coef_out)
        o_ref[:, hi] += jnp.dot(vt_ref[...], coef_in[:, hi], preferred_element_type=F32)
        gate_into_coef(hid_hi, half, coef_out)

    @pl.when((e & 1) == 0)
    def _():
        step(coef_even, coef_odd)

    @pl.when((e & 1) == 1)
    def _():
        step(coef_odd, coef_even)


def peer_dense(xt, u, vt, cb, eb, na, wa, tn=512, te=256):
    d, t = xt.shape
    n_exp = u.shape[0]
    tn, te = min(tn, t), min(te, n_exp)
    n_e = n_exp // te
    sspec = pl.BlockSpec((PEER_HEADS, PEER_NKEYS, tn), lambda i, e: (0, 0, i))
    aspec = pl.BlockSpec((PEER_NKEYS, PEER_HEADS, tn), lambda i, e: (0, 0, i))
    na, wa = na.transpose(1, 0, 2), wa.transpose(1, 0, 2)
    return pl.pallas_call(
        functools.partial(_peer_dense_kernel, te=te),
        grid=(t // tn, n_e + 1),
        in_specs=[pl.BlockSpec((d, tn), lambda i, e: (0, i)),
                  pl.BlockSpec((te, d), lambda i, e: (jnp.minimum(e, n_e - 1), 0)),
                  pl.BlockSpec((d, te), lambda i, e: (0, jnp.maximum(e - 1, 0))),
                  sspec, sspec, aspec, aspec],
        out_specs=pl.BlockSpec((d, tn), lambda i, e: (0, i)),
        out_shape=jax.ShapeDtypeStruct((d, t), F32),
        scratch_shapes=[pltpu.VMEM((te, tn), BF16), pltpu.VMEM((te, tn), BF16)],
        compiler_params=_params("parallel", "arbitrary"),
        name="peer_dense",
    )(xt, u, vt, cb, eb, na, wa)


def kernel(x, p, positions, norm_mix, w_in, qn_moba, kn_moba, qn_diff, kn_diff, lambda_q1, lambda_k1, lambda_q2, lambda_k2, subln_diff, w_branch, w_out, norm_ffn, peer_wq, peer_key_a, peer_key_b, peer_u, peer_v, norm_ple, w_ple_gate, w_ple_proj):
    b, s, d = x.shape
    t = b * s
    depth = w_in.shape[0]
    nb = s // MOBA_BLOCK
    gate_col = 9 * MIX_WIDTH
    h = x.reshape(t, d)
    p_flat = p.reshape(depth, t, -1)
    for i in range(depth):
        n1 = rmsnorm(h, norm_mix[i])
        proj = matmul(n1, w_in, i, out_dtype=BF16)
        qm, qmf, km, qd, kd, kmean = prep_qkv(
            proj, positions, qn_moba[i], kn_moba[i], qn_diff[i], kn_diff[i])
        kmean = kmean.reshape(b, nb, N_HEADS, HEAD_DIM).transpose(0, 2, 1, 3)
        kmean = jnp.pad(kmean, ((0, 0), (0, 0), (0, -nb % 8), (0, 0)))
        o_m = moba_attention(qm, qmf, kmean, km, proj, 2, b, s)
        lam_init = 0.8 - 0.6 * math.exp(-0.3 * i)
        o_d = diff_attention(qd, kd, proj, 5, lambda_q1[i], lambda_k1[i], lambda_q2[i], lambda_k2[i],
                             subln_diff[i], lam_init, b, s)
        o_s = stick_attention(proj, 6, b, s)
        merged = merge_branches(o_m, o_d, o_s, w_branch, i, proj, gate_col)
        h = matmul(merged, w_out, i, mode="residual", res=h)
        n2, n2_t = rmsnorm(h, norm_ffn[i], with_transpose=True)
        q = matmul(n2, peer_wq, i)
        cb, eb, na, wa = peer_scores(q, peer_key_a[i], peer_key_b[i])
        ffn_t = peer_dense(n2_t, to_bf16(peer_u, i), to_bf16(peer_v, i, transpose=True), cb, eb, na, wa)
        h = h + ffn_t.T
        n3 = rmsnorm(h, norm_ple[i])
        h = matmul(n3, w_ple_gate, i, mode="ple", res=h, p=p_flat, wp=w_ple_proj)
    return h.reshape(b, s, d)
```

```python
import functools
import math

import jax
import jax.numpy as jnp
from jax import lax
from jax.experimental import pallas as pl
from jax.experimental.pallas import tpu as pltpu

F32 = jnp.float32
BF16 = jnp.bfloat16

LANES = 128
HEAD_DIM = 128
N_HEADS = 8
MIX_WIDTH = N_HEADS * HEAD_DIM
DIFF_SUB = HEAD_DIM // 2
ROPE_THETA = 500000.0
ROT_HALF = HEAD_DIM // 8
ROT_HALF_DIFF = DIFF_SUB // 8
MOBA_BLOCK = 256
MOBA_TOPK = 3
PEER_HEADS = 8
PEER_NKEYS = 128
PEER_TOPK = 16
NORM_EPS = 1e-6
NEG_INF = -1e30
VMEM_LIMIT = 56 * 1024 * 1024

_NT = (((1,), (1,)), ((), ()))


def _params(*sem):
    return pltpu.CompilerParams(dimension_semantics=sem, vmem_limit_bytes=VMEM_LIMIT)


def _rmsnorm_kernel(x_ref, g_ref, o_ref, *maybe_ot_ref):
    x = x_ref[...]
    ms = jnp.mean(x * x, axis=-1, keepdims=True)
    y = x * lax.rsqrt(ms + NORM_EPS) * g_ref[...]
    o_ref[...] = y.astype(o_ref.dtype)
    for ot_ref in maybe_ot_ref:
        ot_ref[...] = y.T.astype(ot_ref.dtype)


def rmsnorm(x, gain, tm=256, with_transpose=False):
    t, d = x.shape
    tm = min(tm, t)
    out_specs = [pl.BlockSpec((tm, d), lambda i: (i, 0))]
    out_shape = [jax.ShapeDtypeStruct((t, d), BF16)]
    if with_transpose:
        out_specs.append(pl.BlockSpec((d, tm), lambda i: (0, i)))
        out_shape.append(jax.ShapeDtypeStruct((d, t), BF16))
    outs = pl.pallas_call(
        _rmsnorm_kernel,
        grid=(t // tm,),
        in_specs=[pl.BlockSpec((tm, d), lambda i: (i, 0)),
                  pl.BlockSpec((1, d), lambda i: (0, 0))],
        out_specs=out_specs,
        out_shape=out_shape,
        compiler_params=_params("parallel"),
        name="rmsnorm_t" if with_transpose else "rmsnorm",
    )(x, gain.reshape(1, d))
    return outs if with_transpose else outs[0]


def _residual_rmsnorm_kernel(x_ref, yt_ref, g_ref, h_ref, n_ref):
    h = x_ref[...] + yt_ref[...].T
    h_ref[...] = h
    ms = jnp.mean(h * h, axis=-1, keepdims=True)
    n_ref[...] = (h * lax.rsqrt(ms + NORM_EPS) * g_ref[...]).astype(n_ref.dtype)


def residual_rmsnorm(x, y_t, gain, tm=256):
    t, d = x.shape
    tm = min(tm, t)
    row = pl.BlockSpec((tm, d), lambda i: (i, 0))
    return pl.pallas_call(
        _residual_rmsnorm_kernel,
        grid=(t // tm,),
        in_specs=[row, pl.BlockSpec((d, tm), lambda i: (0, i)), pl.BlockSpec((1, d), lambda i: (0, 0))],
        out_specs=[row, row],
        out_shape=[jax.ShapeDtypeStruct((t, d), F32), jax.ShapeDtypeStruct((t, d), BF16)],
        compiler_params=_params("parallel"),
        name="residual_rmsnorm",
    )(x, y_t, gain.reshape(1, d))


def _relayout_kernel(x_ref, o_ref, *, transpose):
    x = x_ref[...]
    o_ref[...] = (x.T if transpose else x).astype(o_ref.dtype)


def to_bf16(x, layer, transpose=False, tb=1024):
    _, r, c = x.shape
    tr, tc = min(tb, r), min(tb, c)
    if transpose:
        out_spec, out_dims = pl.BlockSpec((tc, tr), lambda i, j: (j, i)), (c, r)
    else:
        out_spec, out_dims = pl.BlockSpec((tr, tc), lambda i, j: (i, j)), (r, c)
    return pl.pallas_call(
        functools.partial(_relayout_kernel, transpose=transpose),
        grid=(r // tr, c // tc),
        in_specs=[pl.BlockSpec((None, tr, tc), lambda i, j: (layer, i, j))],
        out_specs=out_spec,
        out_shape=jax.ShapeDtypeStruct(out_dims, BF16),
        compiler_params=_params("parallel", "parallel"),
        name="to_bf16_t" if transpose else "to_bf16",
    )(x)


def _mm_kernel(*refs, mode):
    if mode == "plain":
        a_ref, b_ref, o_ref = refs
    elif mode == "residual":
        a_ref, b_ref, r_ref, o_ref = refs
    else:
        a_ref, b_ref, r_ref, p_ref, wp_ref, o_ref = refs
    acc = jnp.dot(a_ref[...], b_ref[...].astype(BF16), preferred_element_type=F32)
    if mode == "plain":
        o_ref[...] = acc.astype(o_ref.dtype)
    elif mode == "residual":
        o_ref[...] = (r_ref[...] + acc).astype(o_ref.dtype)
    else:
        emb = jnp.dot(p_ref[...].astype(BF16), wp_ref[...].astype(BF16),
                      preferred_element_type=F32)
        o_ref[...] = (r_ref[...] + jax.nn.sigmoid(acc) * emb).astype(o_ref.dtype)


def matmul(a, w, layer, *, mode="plain", res=None, p=None, wp=None, out_dtype=F32, tm=2048, tn=None):
    m, kdim = a.shape
    n = w.shape[2]
    if tn is None:
        tn = 512 if mode == "plain" else 256
    tm, tn = min(tm, m), min(tn, n)
    in_specs = [pl.BlockSpec((tm, kdim), lambda i, j: (i, 0), pipeline_mode=pl.Buffered(1)),
                pl.BlockSpec((None, kdim, tn), lambda i, j: (layer, 0, j))]
    args = [a, w]
    if mode in ("residual", "ple"):
        in_specs.append(pl.BlockSpec((tm, tn), lambda i, j: (i, j)))
        args.append(res)
    if mode == "ple":
        pd = p.shape[2]
        in_specs += [pl.BlockSpec((None, tm, pd), lambda i, j: (layer, i, 0)),
                     pl.BlockSpec((None, pd, tn), lambda i, j: (layer, 0, j))]
        args += [p, wp]
    return pl.pallas_call(
        functools.partial(_mm_kernel, mode=mode),
        grid=(m // tm, n // tn),
        in_specs=in_specs,
        out_specs=pl.BlockSpec((tm, tn), lambda i, j: (i, j)),
        out_shape=jax.ShapeDtypeStruct((m, n), out_dtype),
        compiler_params=_params("parallel", "arbitrary"),
        name="matmul_" + mode,
    )(*args)


def _rope(x, cos, sin, half, period):
    lane = lax.broadcasted_iota(jnp.int32, x.shape, 1)
    first = (lane & (period - 1)) < half
    partner = jnp.where(first, pltpu.roll(x, LANES - half, 1), pltpu.roll(x, half, 1))
    return x * cos + partner * jnp.where(first, -sin, sin)


def _prep_kernel(pos_ref, fm_ref, fd_ref, qng_ref, kng_ref, qdg_ref, kdg_ref,
                 qm_i, km_i, qd_i, kd_i, qm_o, qmf_o, km_o, qd_o, kd_o, kmean_o):
    pos = pos_ref[...].astype(F32)
    ang_m = pos * fm_ref[...]
    ang_d = pos * fd_ref[...]
    cos_m, sin_m = jnp.cos(ang_m), jnp.sin(ang_m)
    cos_d, sin_d = jnp.cos(ang_d), jnp.sin(ang_d)
    lane = lax.broadcasted_iota(jnp.int32, ang_m.shape, 1)
    low = lane < DIFF_SUB

    def norm_full(x, g):
        ms = jnp.mean(x * x, axis=-1, keepdims=True)
        return x * lax.rsqrt(ms + NORM_EPS) * g

    def norm_halves(x, g):
        xx = x * x
        lo = jnp.sum(jnp.where(low, xx, 0.0), axis=-1, keepdims=True)
        hi = jnp.sum(jnp.where(low, 0.0, xx), axis=-1, keepdims=True)
        ms = jnp.where(low, lo, hi) * (1.0 / DIFF_SUB)
        return x * lax.rsqrt(ms + NORM_EPS) * g

    for h in range(N_HEADS):
        sl = slice(h * HEAD_DIM, (h + 1) * HEAD_DIM)
        qm = _rope(norm_full(qm_i[:, sl].astype(F32), qng_ref[...]), cos_m, sin_m, ROT_HALF, HEAD_DIM)
        km = _rope(norm_full(km_i[:, sl].astype(F32), kng_ref[...]), cos_m, sin_m, ROT_HALF, HEAD_DIM)
        qmf_o[:, sl] = qm
        qm_o[:, sl] = qm.astype(BF16)
        km_o[:, sl] = km.astype(BF16)
        kmean_o[0, h:h + 1, :] = jnp.mean(km, axis=0, keepdims=True)
        qd = _rope(norm_halves(qd_i[:, sl].astype(F32), qdg_ref[...]), cos_d, sin_d,
                   ROT_HALF_DIFF, DIFF_SUB)
        kd = _rope(norm_halves(kd_i[:, sl].astype(F32), kdg_ref[...]), cos_d, sin_d,
                   ROT_HALF_DIFF, DIFF_SUB)
        qd_o[:, sl] = qd.astype(BF16)
        kd_o[:, sl] = kd.astype(BF16)


def _rope_table(half, period):
    lane = jnp.arange(LANES)
    inv_freq = ROPE_THETA ** (-jnp.arange(half, dtype=F32) / half)
    tab = jnp.where((lane % period) < 2 * half, inv_freq[lane % half], 0.0)
    return tab.reshape(1, LANES).astype(F32)


def prep_qkv(proj, positions, qn_m, kn_m, qn_d, kn_d):
    t = proj.shape[0]
    tm = MOBA_BLOCK
    w = MIX_WIDTH
    row = lambda i: (i, 0)
    col_spec = [pl.BlockSpec((tm, w), functools.partial(lambda i, c: (i, c), c=c)) for c in (0, 1, 3, 4)]
    vec = pl.BlockSpec((1, LANES), lambda i: (0, 0))
    bf = jax.ShapeDtypeStruct((t, w), BF16)
    out_shape = [bf, jax.ShapeDtypeStruct((t, w), F32), bf, bf, bf,
                 jax.ShapeDtypeStruct((t // tm, N_HEADS, HEAD_DIM), F32)]
    out_specs = [pl.BlockSpec((tm, w), row)] * 5 + [
        pl.BlockSpec((1, N_HEADS, HEAD_DIM), lambda i: (i, 0, 0))]
    return pl.pallas_call(
        _prep_kernel,
        grid=(t // tm,),
        in_specs=[pl.BlockSpec((tm, 1), row), vec, vec, vec, vec, vec, vec] + col_spec,
        out_specs=out_specs,
        out_shape=out_shape,
        compiler_params=_params("parallel"),
        name="prep_qkv",
    )(positions.reshape(t, 1), _rope_table(ROT_HALF, HEAD_DIM), _rope_table(ROT_HALF_DIFF, DIFF_SUB),
      qn_m.reshape(1, LANES), kn_m.reshape(1, LANES),
      jnp.tile(qn_d, 2).reshape(1, LANES), jnp.tile(kn_d, 2).reshape(1, LANES),
      *([proj] * 4))


def _ones_column(rows, at=0):
    return (lax.broadcasted_iota(jnp.int32, (rows, LANES), 1) == at).astype(BF16)


def _value_with_ones(v):
    return jnp.concatenate([v, jnp.ones(v.shape, BF16)], axis=1)


def _moba_kernel(q_ref, qf_ref, kmean_ref, k_ref, v_ref, o_ref, knorm_ref, *, tk):
    blk = MOBA_BLOCK
    qi = pl.program_id(2)
    scale = HEAD_DIM ** -0.5
    big = NEG_INF / scale
    nbp = kmean_ref.shape[0]

    @pl.when(qi == 0)
    def _():
        knorm_ref[...] = jnp.broadcast_to(_max_row_norm(k_ref, 0, HEAD_DIM), knorm_ref.shape)

    gate = lax.dot_general(kmean_ref[...], qf_ref[...], _NT, precision=lax.Precision.HIGHEST,
                           preferred_element_type=F32)
    rid = lax.broadcasted_iota(jnp.int32, gate.shape, 0)
    rank = jnp.zeros(gate.shape, jnp.int32)
    for m in range(nbp):
        g_m = gate[m:m + 1, :]
        beats = (g_m > gate) | ((g_m == gate) & (rid > m))
        rank = rank + jnp.where(beats & (m < qi), 1, 0)
    allowed = ((rid < qi) & (rank < MOBA_TOPK)) | (rid == qi)
    pen_t = jnp.where(allowed, 0.0, big)
    pen_t = jnp.concatenate([pen_t, jnp.full((LANES - nbp, blk), big, F32)], axis=0)
    penalty = pen_t.T

    q = q_ref[...]
    lane = lax.broadcasted_iota(jnp.int32, (blk, LANES), 1)
    q_pos = qi * blk + lax.broadcasted_iota(jnp.int32, (blk, tk), 0)
    col = lax.broadcasted_iota(jnp.int32, (blk, tk), 1)
    krow = lax.broadcasted_iota(jnp.int32, (tk, LANES), 0)
    klane = lax.broadcasted_iota(jnp.int32, (tk, LANES), 1)
    n_tiles = (qi * blk) // tk + 1

    def scores(n, q_aug, shift_lane):
        off = pl.multiple_of(n * tk, tk)
        kb = k_ref[pl.ds(off, tk), :]
        block_of_row = n * (tk // blk) + krow // blk
        marks = ((klane == block_of_row) | (klane == shift_lane)).astype(BF16)
        s = lax.dot_general(q_aug, jnp.concatenate([kb, marks], axis=1), _NT,
                            preferred_element_type=F32) * scale
        return jnp.where(off + col <= q_pos, s, NEG_INF), off

    q_max = jnp.concatenate([q, penalty.astype(BF16)], axis=1)

    def sweep_max(n, best):
        s, _ = scores(n, q_max, -1)
        return jnp.maximum(best, s)

    def exact_row_max():
        best = lax.fori_loop(0, n_tiles, sweep_max, jnp.full((blk, tk), NEG_INF, F32))
        return jnp.max(best, axis=-1, keepdims=True)

    qsq = q.astype(F32)
    bound = jnp.sqrt(jnp.sum(qsq * qsq, axis=-1, keepdims=True)) * knorm_ref[0:1, 0:1] * scale
    shift = lax.cond(2.0 * jnp.max(bound) < SHIFT_SLACK, lambda: bound, exact_row_max)
    shift_lane = LANES - 1
    q_sum = jnp.concatenate(
        [q, jnp.where(lane == shift_lane, -shift / scale, penalty).astype(BF16)], axis=1)

    def sweep_sum(n, acc):
        s, off = scores(n, q_sum, shift_lane)
        vb = v_ref[pl.ds(off, tk), :]
        return acc + jnp.dot(jnp.exp(s).astype(BF16), _value_with_ones(vb), preferred_element_type=F32)

    acc = lax.fori_loop(0, n_tiles, sweep_sum, jnp.zeros((blk, 2 * HEAD_DIM), F32))
    o_ref[...] = (acc[:, :HEAD_DIM] / acc[:, HEAD_DIM:]).astype(o_ref.dtype)


def moba_attention(q, qf, kmean, k, v, v_group, batch, seq, tk=512):
    t = q.shape[0]
    blk = MOBA_BLOCK
    nq = seq // blk
    tk = min(tk, seq)
    qspec = pl.BlockSpec((blk, HEAD_DIM), lambda b, h, i: (b * nq + i, h))
    kspec = pl.BlockSpec((seq, HEAD_DIM), lambda b, h, i: (b, h))
    vspec = pl.BlockSpec((seq, HEAD_DIM), lambda b, h, i: (b, v_group * N_HEADS + h))
    return pl.pallas_call(
        functools.partial(_moba_kernel, tk=tk),
        grid=(batch, N_HEADS, nq),
        in_specs=[qspec, qspec,
                  pl.BlockSpec((None, None, kmean.shape[2], HEAD_DIM), lambda b, h, i: (b, h, 0, 0)),
                  kspec, vspec],
        out_specs=qspec,
        out_shape=jax.ShapeDtypeStruct((t, MIX_WIDTH), BF16),
        scratch_shapes=[pltpu.VMEM((8, LANES), F32)],
        compiler_params=_params("parallel", "parallel", "arbitrary"),
        name="moba_attention",
    )(q, qf, kmean, k, v)


SHIFT_SLACK = 80.0


def _max_row_norm(x_ref, lane_lo, lane_hi, chunk=512):
    n_rows = x_ref.shape[0]
    chunk = min(chunk, n_rows)
    lane = lax.broadcasted_iota(jnp.int32, (chunk, x_ref.shape[1]), 1)
    keep = (lane >= lane_lo) & (lane < lane_hi)

    def body(c, best):
        x = x_ref[pl.ds(pl.multiple_of(c * chunk, chunk), chunk), :].astype(F32)
        sq = jnp.sum(jnp.where(keep, x * x, 0.0), axis=-1, keepdims=True)
        return jnp.maximum(best, jnp.max(sq, axis=0, keepdims=True))

    return jnp.sqrt(lax.fori_loop(0, n_rows // chunk, body, jnp.zeros((1, 1), F32)))


def _diff_kernel(lq1_ref, lk1_ref, lq2_ref, lk2_ref, g_ref, q_ref, k_ref, v_ref, o_ref, knorm_ref,
                 *, tq, tk, lam_init):
    qi = pl.program_id(2)

    @pl.when(qi == 0)
    def _():
        knorm_ref[0] = jnp.broadcast_to(_max_row_norm(k_ref, 0, DIFF_SUB), knorm_ref.shape[1:])
        knorm_ref[1] = jnp.broadcast_to(_max_row_norm(k_ref, DIFF_SUB, HEAD_DIM), knorm_ref.shape[1:])

    q = q_ref[...] * (DIFF_SUB ** -0.5)
    lane = lax.broadcasted_iota(jnp.int32, q.shape, 1)
    zero = jnp.zeros_like(q)
    qq = jnp.concatenate([jnp.where(lane < DIFF_SUB, q, zero),
                          jnp.where(lane < DIFF_SUB, zero, q)], axis=0)
    rows = 2 * tq
    q_pos = qi * tq + (lax.broadcasted_iota(jnp.int32, (rows, tk), 0) & (tq - 1))
    col = lax.broadcasted_iota(jnp.int32, (rows, tk), 1)
    n_full = (qi * tq) // tk
    last = pl.multiple_of(n_full * tk, tk)
    causal = last + col <= q_pos

    def lane_tile_max(s):
        out = s[:, :LANES]
        for c in range(LANES, tk, LANES):
            out = jnp.maximum(out, s[:, c:c + LANES])
        return out

    def sweep_max(n, best):
        kb = k_ref[pl.ds(pl.multiple_of(n * tk, tk), tk), :]
        return jnp.maximum(best, lane_tile_max(lax.dot_general(qq, kb, _NT, preferred_element_type=F32)))

    def exact_row_max():
        best = lax.fori_loop(0, n_full, sweep_max, jnp.full((rows, LANES), NEG_INF, F32))
        s_last = lax.dot_general(qq, k_ref[pl.ds(last, tk), :], _NT, preferred_element_type=F32)
        best = jnp.maximum(best, lane_tile_max(jnp.where(causal, s_last, NEG_INF)))
        return jnp.max(best, axis=-1, keepdims=True)

    qf = qq.astype(F32)
    q_norm = jnp.sqrt(jnp.sum(qf * qf, axis=-1, keepdims=True))
    first_half = lax.broadcasted_iota(jnp.int32, (rows, 1), 0) < tq
    bound = q_norm * jnp.where(first_half, knorm_ref[0][0:1, 0:1], knorm_ref[1][0:1, 0:1])
    shift = lax.cond(2.0 * jnp.max(bound) < SHIFT_SLACK, lambda: bound, exact_row_max)

    lane_r = lax.broadcasted_iota(jnp.int32, (rows, LANES), 1)
    q_aug = jnp.concatenate([qq, jnp.where(lane_r == 0, -shift, 0.0).astype(BF16)], axis=1)
    ones_col = _ones_column(tk)

    def tile_sum(off, mask):
        k_aug = jnp.concatenate([k_ref[pl.ds(off, tk), :], ones_col], axis=1)
        v_aug = _value_with_ones(v_ref[pl.ds(off, tk), :])
        outs = []
        for half in range(2):
            s = lax.dot_general(q_aug[half * tq:(half + 1) * tq], k_aug, _NT,
                                preferred_element_type=F32)
            if mask is not None:
                s = jnp.where(mask[half * tq:(half + 1) * tq], s, NEG_INF)
            outs.append(jnp.dot(jnp.exp(s).astype(BF16), v_aug, preferred_element_type=F32))
        return jnp.concatenate(outs, axis=0)

    def sweep_sum(n, acc):
        return acc + tile_sum(pl.multiple_of(n * tk, tk), None)

    acc = lax.fori_loop(0, n_full, sweep_sum, jnp.zeros((rows, 2 * HEAD_DIM), F32))
    acc = acc + tile_sum(last, causal)

    lam = (jnp.exp(jnp.sum(lq1_ref[...] * lk1_ref[...], axis=-1, keepdims=True))
           - jnp.exp(jnp.sum(lq2_ref[...] * lk2_ref[...], axis=-1, keepdims=True)) + lam_init)
    out = acc[:, :HEAD_DIM] / acc[:, HEAD_DIM:]
    o = out[:tq] - lam * out[tq:]
    ms = jnp.mean(o * o, axis=-1, keepdims=True)
    o_ref[...] = (o * lax.rsqrt(ms + NORM_EPS) * g_ref[...] * (1.0 - lam_init)).astype(o_ref.dtype)


def diff_attention(q, k, v, v_group, lq1, lk1, lq2, lk2, subln, lam_init, batch, seq, tq=256, tk=512):
    t = q.shape[0]
    tq, tk = min(tq, seq), min(tk, seq)
    nq = seq // tq
    qspec = pl.BlockSpec((tq, HEAD_DIM), lambda b, h, i: (b * nq + i, h))
    kspec = pl.BlockSpec((seq, HEAD_DIM), lambda b, h, i: (b, h))
    vspec = pl.BlockSpec((seq, HEAD_DIM), lambda b, h, i: (b, v_group * N_HEADS + h))
    lspec = pl.BlockSpec((1, DIFF_SUB), lambda b, h, i: (0, 0))
    return pl.pallas_call(
        functools.partial(_diff_kernel, tq=tq, tk=tk, lam_init=lam_init),
        grid=(batch, N_HEADS, nq),
        in_specs=[lspec, lspec, lspec, lspec,
                  pl.BlockSpec((1, HEAD_DIM), lambda b, h, i: (0, 0)), qspec, kspec, vspec],
        out_specs=qspec,
        out_shape=jax.ShapeDtypeStruct((t, MIX_WIDTH), BF16),
        scratch_shapes=[pltpu.VMEM((2, 8, LANES), F32)],
        compiler_params=_params("parallel", "parallel", "arbitrary"),
        name="diff_attention",
    )(lq1.reshape(1, DIFF_SUB), lk1.reshape(1, DIFF_SUB), lq2.reshape(1, DIFF_SUB),
      lk2.reshape(1, DIFF_SUB), subln.reshape(1, HEAD_DIM), q, k, v)


def _stick_kernel(q_ref, k_ref, v_ref, o_ref, *, tq, tk):
    qi = pl.program_id(2)
    scale = HEAD_DIM ** -0.5
    n_heads = q_ref.shape[1] // HEAD_DIM
    upper = (lax.broadcasted_iota(jnp.int32, (tk, tk), 0)
             > lax.broadcasted_iota(jnp.int32, (tk, tk), 1)).astype(BF16)
    q_pos = qi * tq + lax.broadcasted_iota(jnp.int32, (tq, tk), 0)
    col = lax.broadcasted_iota(jnp.int32, (tq, tk), 1)
    n_diag = tq // tk
    n_past = qi * n_diag

    def tile(n, hd, later, acc, masked):
        off = pl.multiple_of(n * tk, tk)
        cols = slice(hd * HEAD_DIM, (hd + 1) * HEAD_DIM)
        kb = k_ref[pl.ds(off, tk), cols]
        vb = v_ref[pl.ds(off, tk), cols]
        z = lax.dot_general(q_ref[:, cols], kb, _NT, preferred_element_type=F32) * scale
        sp = jnp.maximum(z, 0.0) + jnp.log(1.0 + jnp.exp(-jnp.abs(z)))
        if masked:
            strict = (off + col) < q_pos
            sp = jnp.where(strict, sp, 0.0)
        hi = sp.astype(BF16)
        lo = (sp - hi.astype(F32)).astype(BF16)
        between = (jnp.dot(hi, upper, preferred_element_type=F32)
                   + jnp.dot(lo, upper, preferred_element_type=F32))
        a = jnp.exp(z - sp - between - later)
        if masked:
            a = jnp.where(strict, a, 0.0)
        acc = acc + jnp.dot(a.astype(BF16), vb, preferred_element_type=F32)
        return later + jnp.sum(sp, axis=-1, keepdims=True), acc

    state = [(jnp.zeros((tq, 1), F32), jnp.zeros((tq, HEAD_DIM), F32)) for _ in range(n_heads)]
    for d in range(n_diag):
        state = [tile(n_past + n_diag - 1 - d, hd, *state[hd], True) for hd in range(n_heads)]

    def body(it, carry):
        return tuple(tile(n_past - 1 - it, hd, *carry[hd], False) for hd in range(n_heads))

    state = lax.fori_loop(0, n_past, body, tuple(state))
    for hd in range(n_heads):
        o_ref[:, hd * HEAD_DIM:(hd + 1) * HEAD_DIM] = state[hd][1].astype(o_ref.dtype)


def stick_attention(qkv, q_group, batch, seq, tq=512, tk=256, heads_per_step=2):
    t = qkv.shape[0]
    tq, tk = min(tq, seq), min(tk, seq)
    nq = seq // tq
    width = heads_per_step * HEAD_DIM
    per_group = MIX_WIDTH // width

    def col(group):
        return lambda b, h, i: (b, group * per_group + h)

    qspec = pl.BlockSpec((tq, width), lambda b, h, i: (b * nq + i, q_group * per_group + h))
    kspec = pl.BlockSpec((seq, width), col(q_group + 1))
    vspec = pl.BlockSpec((seq, width), col(q_group + 2))
    return pl.pallas_call(
        functools.partial(_stick_kernel, tq=tq, tk=tk),
        grid=(batch, N_HEADS // heads_per_step, nq),
        in_specs=[qspec, kspec, vspec],
        out_specs=pl.BlockSpec((tq, width), lambda b, h, i: (b * nq + i, h)),
        out_shape=jax.ShapeDtypeStruct((t, MIX_WIDTH), BF16),
        compiler_params=_params("parallel", "parallel", "arbitrary"),
        name="stick_attention",
    )(qkv, qkv, qkv)


def _merge_kernel(om_ref, od_ref, os_ref, w_ref, gm_ref, gd_ref, gs_ref, o_ref):
    acc = None
    for k, (g_ref, x_ref) in enumerate(((gm_ref, om_ref), (gd_ref, od_ref), (gs_ref, os_ref))):
        term = jax.nn.sigmoid(g_ref[...].astype(F32)) * jnp.dot(
            x_ref[...], w_ref[k].astype(BF16), preferred_element_type=F32)
        acc = term if acc is None else acc + term
    o_ref[...] = acc.astype(o_ref.dtype)


def merge_branches(o_m, o_d, o_s, w_branch, layer, proj, gate_col, tm=1024, tn=512):
    t, kdim = o_m.shape
    d = w_branch.shape[3]
    tm, tn = min(tm, t), min(tn, d)
    ospec = pl.BlockSpec((tm, kdim), lambda i, j: (i, 0))
    gspecs = [pl.BlockSpec((tm, tn), functools.partial(lambda i, j, c: (i, c + j), c=(gate_col + c * d) // tn))
              for c in range(3)]
    return pl.pallas_call(
        _merge_kernel,
        grid=(t // tm, d // tn),
        in_specs=[ospec, ospec, ospec,
                  pl.BlockSpec((None, 3, kdim, tn), lambda i, j: (layer, 0, 0, j))] + gspecs,
        out_specs=pl.BlockSpec((tm, tn), lambda i, j: (i, j)),
        out_shape=jax.ShapeDtypeStruct((t, d), BF16),
        compiler_params=_params("parallel", "parallel"),
        name="merge_branches",
    )(o_m, o_d, o_s, w_branch, proj, proj, proj)


def _extract_top(x, rounds, break_ties):
    rows, n = x.shape
    rid = lax.broadcasted_iota(jnp.int32, (rows, n), 0)
    kid = lax.broadcasted_iota(jnp.int32, (rounds, n), 0)

    def body(k, carry):
        x, rank, vals = carry
        best = jnp.max(x, axis=0, keepdims=True)
        hit = x == best
        if break_ties:
            hit = rid == jnp.min(jnp.where(hit, rid, rows), axis=0, keepdims=True)
        rank = jnp.where(hit, k, rank)
        x = jnp.where(hit, -jnp.inf, x)
        vals = jnp.where(kid == k, best, vals)
        return x, rank, vals

    init = (x, jnp.full((rows, n), rounds, jnp.int32), jnp.zeros((rounds, n), F32))
    _, rank, vals = lax.fori_loop(0, rounds, body, init)
    return vals, rank


def _peer_score_kernel(q_ref, keys_ref, cb_o, eb_o, na_o, wa_o):
    nk, topk = PEER_NKEYS, PEER_TOPK
    s = lax.dot_general(keys_ref[...], q_ref[...], _NT, precision=lax.Precision.HIGHEST,
                        preferred_element_type=F32)

    def taken_off(rank):
        taken = jnp.sum((rank < topk).astype(jnp.int32), axis=0, keepdims=True)
        return jnp.max(jnp.where(taken != topk, 1, 0))

    suspect = _peer_select(s, cb_o, eb_o, na_o, wa_o, False, taken_off)

    @pl.when(suspect > 0)
    def _():
        _peer_select(s, cb_o, eb_o, na_o, wa_o, True, taken_off)


def _peer_select(s, cb_o, eb_o, na_o, wa_o, break_ties, taken_off):
    nk, topk = PEER_NKEYS, PEER_TOPK
    sa, sb = s[:nk], s[nk:]
    va, rank_a = _extract_top(sa, topk, break_ties)
    vb, rank_b = _extract_top(sb, topk, break_ties)
    n = s.shape[1]
    row8 = lax.broadcasted_iota(jnp.int32, (8, n), 0)
    ninf = -jnp.inf
    a = [va[r:r + 1] for r in range(topk)]
    groups = [a[0] + vb]
    for r, lim in ((1, 8), (2, 5), (3, 4), (4, 3)):
        groups.append(jnp.where(row8 < lim, a[r] + vb[0:8], ninf))
    a_mix = jnp.where(row8 < 2, a[5], jnp.where(row8 < 4, a[6], jnp.where(
        row8 < 6, a[7], jnp.where(row8 == 6, a[8], a[9]))))
    b_mix = jnp.where((row8 < 6) & ((row8 & 1) == 1), vb[1:2], vb[0:1])
    groups.append(a_mix + b_mix)
    tail = jnp.full((8, n), ninf, F32)
    for k in range(6):
        tail = jnp.where(row8 == k, a[10 + k] + vb[0:1], tail)
    groups.append(tail)
    cand = jnp.concatenate(groups, axis=0)
    top_s, rank_c = _extract_top(cand, topk, break_ties)
    taken = (rank_c < topk).astype(jnp.int32)
    counts = [jnp.sum(taken[0:16], axis=0, keepdims=True)]
    counts += [jnp.sum(taken[8 + 8 * r:16 + 8 * r], axis=0, keepdims=True) for r in range(1, 5)]
    counts += [taken[48 + 2 * k:49 + 2 * k] + taken[49 + 2 * k:50 + 2 * k] for k in range(3)]
    counts += [taken[54 + k:55 + k] for k in range(8)]
    count = jnp.concatenate(counts, axis=0)
    z = jnp.sum(jnp.exp(top_s - top_s[0:1]), axis=0, keepdims=True)
    na = jnp.zeros(sa.shape, jnp.int32)
    for r in range(topk):
        na = jnp.where(rank_a == r, count[r:r + 1], na)
    cb_o[...] = rank_b.astype(F32)
    eb_o[...] = jnp.exp(sb - vb[0:1])
    na_o[...] = na.astype(F32)
    wa_o[...] = jnp.exp(sa - va[0:1]) / z
    return taken_off(rank_a) + taken_off(rank_b) + taken_off(rank_c)


def peer_scores(q, key_a, key_b, tn=256):
    t = q.shape[0]
    tn = min(tn, t)
    half = key_a.shape[-1]
    zeros = jnp.zeros_like(key_a)
    keys = jnp.concatenate([jnp.concatenate([key_a, zeros], axis=-1),
                            jnp.concatenate([zeros, key_b], axis=-1)], axis=1)
    del half
    out = jax.ShapeDtypeStruct((PEER_HEADS, PEER_NKEYS, t), F32)
    ospec = pl.BlockSpec((None, PEER_NKEYS, tn), lambda i, h: (h, 0, i))
    return pl.pallas_call(
        _peer_score_kernel,
        grid=(t // tn, PEER_HEADS),
        in_specs=[pl.BlockSpec((tn, LANES), lambda i, h: (i, h)),
                  pl.BlockSpec((None, 2 * PEER_NKEYS, LANES), lambda i, h: (h, 0, 0))],
        out_specs=[ospec] * 4,
        out_shape=[out] * 4,
        compiler_params=_params("parallel", "parallel"),
        name="peer_scores",
    )(q, keys)


def _peer_dense_kernel(xt_ref, u_ref, vt_ref, cb_ref, eb_ref, na_ref, wa_ref, o_ref,
                       coef_even, coef_odd, *, te):
    e = pl.program_id(1)
    last = pl.num_programs(1) - 2
    tn = o_ref.shape[1]
    half = tn // 2

    @pl.when(e == 0)
    def _():
        o_ref[...] = jnp.zeros_like(o_ref)
        coef_odd[...] = jnp.zeros_like(coef_odd)

    def gate_into_coef(hid, lane0, coef_out):
        chunk = 16
        for di in range(te // PEER_NKEYS):
            i = jnp.minimum(e, last) * (te // PEER_NKEYS) + di
            for sub in range(0, hid.shape[1], LANES):
                lanes = slice(lane0 + sub, lane0 + sub + LANES)
                n_rows = [na_ref[i, h:h + 1, lanes] for h in range(PEER_HEADS)]
                w_rows = [wa_ref[i, h:h + 1, lanes] for h in range(PEER_HEADS)]
                for r in range(0, PEER_NKEYS, chunk):
                    rs = slice(r, r + chunk)
                    w = None
                    for h in range(PEER_HEADS):
                        term = jnp.where(cb_ref[h, rs, lanes] < n_rows[h],
                                         eb_ref[h, rs, lanes] * w_rows[h], 0.0)
                        w = term if w is None else w + term
                    row0 = di * PEER_NKEYS + r
                    x = hid[row0:row0 + chunk, sub:sub + LANES]
                    act = 0.5 * x * (1.0 + lax.erf(x * (2.0 ** -0.5)))
                    coef_out[row0:row0 + chunk, lanes] = (act * w).astype(BF16)

    def step(coef_out, coef_in):
        lo, hi = slice(0, half), slice(half, tn)
        hid_lo = jnp.dot(u_ref[...], xt_ref[:, lo], preferred_element_type=F32)
        o_ref[:, lo] += jnp.dot(vt_ref[...], coef_in[:, lo], preferred_element_type=F32)
        hid_hi = jnp.dot(u_ref[...], xt_ref[:, hi], preferred_element_type=F32)
        gate_into_coef(hid_lo, 0, coef_out)
        o_ref[:, hi] += jnp.dot(vt_ref[...], coef_in[:, hi], preferred_element_type=F32)
        gate_into_coef(hid_hi, half, coef_out)

    @pl.when((e & 1) == 0)
    def _():
        step(coef_even, coef_odd)

    @pl.when((e & 1) == 1)
    def _():
        step(coef_odd, coef_even)


def peer_dense(xt, u, vt, cb, eb, na, wa, tn=512, te=256):
    d, t = xt.shape
    n_exp = u.shape[0]
    tn, te = min(tn, t), min(te, n_exp)
    n_e = n_exp // te
    sspec = pl.BlockSpec((PEER_HEADS, PEER_NKEYS, tn), lambda i, e: (0, 0, i))
    aspec = pl.BlockSpec((PEER_NKEYS, PEER_HEADS, tn), lambda i, e: (0, 0, i))
    na, wa = na.transpose(1, 0, 2), wa.transpose(1, 0, 2)
    return pl.pallas_call(
        functools.partial(_peer_dense_kernel, te=te),
        grid=(t // tn, n_e + 1),
        in_specs=[pl.BlockSpec((d, tn), lambda i, e: (0, i)),
                  pl.BlockSpec((te, d), lambda i, e: (jnp.minimum(e, n_e - 1), 0)),
                  pl.BlockSpec((d, te), lambda i, e: (0, jnp.maximum(e - 1, 0))),
                  sspec, sspec, aspec, aspec],
        out_specs=pl.BlockSpec((d, tn), lambda i, e: (0, i)),
        out_shape=jax.ShapeDtypeStruct((d, t), F32),
        scratch_shapes=[pltpu.VMEM((te, tn), BF16), pltpu.VMEM((te, tn), BF16)],
        compiler_params=_params("parallel", "arbitrary"),
        name="peer_dense",
    )(xt, u, vt, cb, eb, na, wa)


def kernel(x, p, positions, norm_mix, w_in, qn_moba, kn_moba, qn_diff, kn_diff, lambda_q1, lambda_k1, lambda_q2, lambda_k2, subln_diff, w_branch, w_out, norm_ffn, peer_wq, peer_key_a, peer_key_b, peer_u, peer_v, norm_ple, w_ple_gate, w_ple_proj):
    b, s, d = x.shape
    t = b * s
    depth = w_in.shape[0]
    nb = s // MOBA_BLOCK
    gate_col = 9 * MIX_WIDTH
    h = x.reshape(t, d)
    p_flat = p.reshape(depth, t, -1)
    for i in range(depth):
        n1 = rmsnorm(h, norm_mix[i])
        proj = matmul(n1, w_in, i, out_dtype=BF16)
        qm, qmf, km, qd, kd, kmean = prep_qkv(
            proj, positions, qn_moba[i], kn_moba[i], qn_diff[i], kn_diff[i])
        kmean = kmean.reshape(b, nb, N_HEADS, HEAD_DIM).transpose(0, 2, 1, 3)
        kmean = jnp.pad(kmean, ((0, 0), (0, 0), (0, -nb % 8), (0, 0)))
        o_m = moba_attention(qm, qmf, kmean, km, proj, 2, b, s)
        lam_init = 0.8 - 0.6 * math.exp(-0.3 * i)
        o_d = diff_attention(qd, kd, proj, 5, lambda_q1[i], lambda_k1[i], lambda_q2[i], lambda_k2[i],
                             subln_diff[i], lam_init, b, s)
        o_s = stick_attention(proj, 6, b, s)
        merged = merge_branches(o_m, o_d, o_s, w_branch, i, proj, gate_col)
        h = matmul(merged, w_out, i, mode="residual", res=h)
        n2, n2_t = rmsnorm(h, norm_ffn[i], with_transpose=True)
        q = matmul(n2, peer_wq, i)
        cb, eb, na, wa = peer_scores(q, peer_key_a[i], peer_key_b[i])
        ffn_t = peer_dense(n2_t, to_bf16(peer_u, i), to_bf16(peer_v, i, transpose=True), cb, eb, na, wa)
        h, n3 = residual_rmsnorm(h, ffn_t, norm_ple[i])
        h = matmul(n3, w_ple_gate, i, mode="ple", res=h, p=p_flat, wp=w_ple_proj)
    return h.reshape(b, s, d)
```

```python
import functools
import math

import jax
import jax.numpy as jnp
from jax import lax
from jax.experimental import pallas as pl
from jax.experimental.pallas import tpu as pltpu

F32 = jnp.float32
BF16 = jnp.bfloat16

LANES = 128
HEAD_DIM = 128
N_HEADS = 8
MIX_WIDTH = N_HEADS * HEAD_DIM
DIFF_SUB = HEAD_DIM // 2
ROPE_THETA = 500000.0
ROT_HALF = HEAD_DIM // 8
ROT_HALF_DIFF = DIFF_SUB // 8
MOBA_BLOCK = 256
MOBA_TOPK = 3
PEER_HEADS = 8
PEER_NKEYS = 128
PEER_TOPK = 16
NORM_EPS = 1e-6
NEG_INF = -1e30
V7X_VMEM_BYTES = 64 * 1024 * 1024
VMEM_LIMIT = V7X_VMEM_BYTES * 7 // 8

_NT = (((1,), (1,)), ((), ()))


def _params(*sem):
    return pltpu.CompilerParams(dimension_semantics=sem, vmem_limit_bytes=VMEM_LIMIT)


def _rmsnorm_kernel(x_ref, g_ref, o_ref, *maybe_ot_ref):
    x = x_ref[...]
    ms = jnp.mean(x * x, axis=-1, keepdims=True)
    y = x * lax.rsqrt(ms + NORM_EPS) * g_ref[...]
    o_ref[...] = y.astype(o_ref.dtype)
    for ot_ref in maybe_ot_ref:
        ot_ref[...] = y.T.astype(ot_ref.dtype)


def rmsnorm(x, gain, tm=256, with_transpose=False):
    t, d = x.shape
    tm = min(tm, t)
    out_specs = [pl.BlockSpec((tm, d), lambda i: (i, 0))]
    out_shape = [jax.ShapeDtypeStruct((t, d), BF16)]
    if with_transpose:
        out_specs.append(pl.BlockSpec((d, tm), lambda i: (0, i)))
        out_shape.append(jax.ShapeDtypeStruct((d, t), BF16))
    outs = pl.pallas_call(
        _rmsnorm_kernel,
        grid=(t // tm,),
        in_specs=[pl.BlockSpec((tm, d), lambda i: (i, 0)),
                  pl.BlockSpec((1, d), lambda i: (0, 0))],
        out_specs=out_specs,
        out_shape=out_shape,
        compiler_params=_params("parallel"),
        name="rmsnorm_t" if with_transpose else "rmsnorm",
    )(x, gain.reshape(1, d))
    return outs if with_transpose else outs[0]


def _residual_rmsnorm_kernel(x_ref, yt_ref, g_ref, h_ref, n_ref):
    h = x_ref[...] + yt_ref[...].T
    h_ref[...] = h
    ms = jnp.mean(h * h, axis=-1, keepdims=True)
    n_ref[...] = (h * lax.rsqrt(ms + NORM_EPS) * g_ref[...]).astype(n_ref.dtype)


def residual_rmsnorm(x, y_t, gain, tm=256):
    t, d = x.shape
    tm = min(tm, t)
    row = pl.BlockSpec((tm, d), lambda i: (i, 0))
    return pl.pallas_call(
        _residual_rmsnorm_kernel,
        grid=(t // tm,),
        in_specs=[row, pl.BlockSpec((d, tm), lambda i: (0, i)), pl.BlockSpec((1, d), lambda i: (0, 0))],
        out_specs=[row, row],
        out_shape=[jax.ShapeDtypeStruct((t, d), F32), jax.ShapeDtypeStruct((t, d), BF16)],
        compiler_params=_params("parallel"),
        name="residual_rmsnorm",
    )(x, y_t, gain.reshape(1, d))


def _relayout_kernel(x_ref, o_ref, *, transpose):
    x = x_ref[...]
    o_ref[...] = (x.T if transpose else x).astype(o_ref.dtype)


def to_bf16(x, layer, transpose=False, tb=1024):
    _, r, c = x.shape
    tr, tc = min(tb, r), min(tb, c)
    if transpose:
        out_spec, out_dims = pl.BlockSpec((tc, tr), lambda i, j: (j, i)), (c, r)
    else:
        out_spec, out_dims = pl.BlockSpec((tr, tc), lambda i, j: (i, j)), (r, c)
    return pl.pallas_call(
        functools.partial(_relayout_kernel, transpose=transpose),
        grid=(r // tr, c // tc),
        in_specs=[pl.BlockSpec((None, tr, tc), lambda i, j: (layer, i, j))],
        out_specs=out_spec,
        out_shape=jax.ShapeDtypeStruct(out_dims, BF16),
        compiler_params=_params("parallel", "parallel"),
        name="to_bf16_t" if transpose else "to_bf16",
    )(x)


def _mm_kernel(*refs, mode):
    if mode == "plain":
        a_ref, b_ref, o_ref = refs
    elif mode == "residual":
        a_ref, b_ref, r_ref, o_ref = refs
    else:
        a_ref, b_ref, r_ref, p_ref, wp_ref, o_ref = refs
    acc = jnp.dot(a_ref[...], b_ref[...].astype(BF16), preferred_element_type=F32)
    if mode == "plain":
        o_ref[...] = acc.astype(o_ref.dtype)
    elif mode == "residual":
        o_ref[...] = (r_ref[...] + acc).astype(o_ref.dtype)
    else:
        emb = jnp.dot(p_ref[...].astype(BF16), wp_ref[...].astype(BF16),
                      preferred_element_type=F32)
        o_ref[...] = (r_ref[...] + jax.nn.sigmoid(acc) * emb).astype(o_ref.dtype)


def matmul(a, w, layer, *, mode="plain", res=None, p=None, wp=None, out_dtype=F32, tm=2048, tn=None):
    m, kdim = a.shape
    n = w.shape[2]
    if tn is None:
        tn = 256 if mode == "ple" else 512
    tm, tn = min(tm, m), min(tn, n)
    in_specs = [pl.BlockSpec((tm, kdim), lambda i, j: (i, 0), pipeline_mode=pl.Buffered(1)),
                pl.BlockSpec((None, kdim, tn), lambda i, j: (layer, 0, j))]
    args = [a, w]
    if mode in ("residual", "ple"):
        in_specs.append(pl.BlockSpec((tm, tn), lambda i, j: (i, j)))
        args.append(res)
    if mode == "ple":
        pd = p.shape[2]
        in_specs += [pl.BlockSpec((None, tm, pd), lambda i, j: (layer, i, 0)),
                     pl.BlockSpec((None, pd, tn), lambda i, j: (layer, 0, j))]
        args += [p, wp]
    return pl.pallas_call(
        functools.partial(_mm_kernel, mode=mode),
        grid=(m // tm, n // tn),
        in_specs=in_specs,
        out_specs=pl.BlockSpec((tm, tn), lambda i, j: (i, j)),
        out_shape=jax.ShapeDtypeStruct((m, n), out_dtype),
        compiler_params=_params("parallel", "arbitrary"),
        name="matmul_" + mode,
    )(*args)


def _rope(x, cos, sin, half, period):
    lane = lax.broadcasted_iota(jnp.int32, x.shape, 1)
    first = (lane & (period - 1)) < half
    partner = jnp.where(first, pltpu.roll(x, LANES - half, 1), pltpu.roll(x, half, 1))
    return x * cos + partner * jnp.where(first, -sin, sin)


def _prep_kernel(pos_ref, fm_ref, fd_ref, qng_ref, kng_ref, qdg_ref, kdg_ref,
                 qm_i, km_i, qd_i, kd_i, qm_o, qmf_o, km_o, qd_o, kd_o, kmean_o):
    pos = pos_ref[...].astype(F32)
    ang_m = pos * fm_ref[...]
    ang_d = pos * fd_ref[...]
    cos_m, sin_m = jnp.cos(ang_m), jnp.sin(ang_m)
    cos_d, sin_d = jnp.cos(ang_d), jnp.sin(ang_d)
    lane = lax.broadcasted_iota(jnp.int32, ang_m.shape, 1)
    low = lane < DIFF_SUB

    def norm_full(x, g):
        ms = jnp.mean(x * x, axis=-1, keepdims=True)
        return x * lax.rsqrt(ms + NORM_EPS) * g

    def norm_halves(x, g):
        xx = x * x
        lo = jnp.sum(jnp.where(low, xx, 0.0), axis=-1, keepdims=True)
        hi = jnp.sum(jnp.where(low, 0.0, xx), axis=-1, keepdims=True)
        ms = jnp.where(low, lo, hi) * (1.0 / DIFF_SUB)
        return x * lax.rsqrt(ms + NORM_EPS) * g

    for h in range(N_HEADS):
        sl = slice(h * HEAD_DIM, (h + 1) * HEAD_DIM)
        qm = _rope(norm_full(qm_i[:, sl].astype(F32), qng_ref[...]), cos_m, sin_m, ROT_HALF, HEAD_DIM)
        km = _rope(norm_full(km_i[:, sl].astype(F32), kng_ref[...]), cos_m, sin_m, ROT_HALF, HEAD_DIM)
        qmf_o[:, sl] = qm
        qm_o[:, sl] = qm.astype(BF16)
        km_o[:, sl] = km.astype(BF16)
        kmean_o[0, h:h + 1, :] = jnp.mean(km, axis=0, keepdims=True)
        qd = _rope(norm_halves(qd_i[:, sl].astype(F32), qdg_ref[...]), cos_d, sin_d,
                   ROT_HALF_DIFF, DIFF_SUB)
        kd = _rope(norm_halves(kd_i[:, sl].astype(F32), kdg_ref[...]), cos_d, sin_d,
                   ROT_HALF_DIFF, DIFF_SUB)
        qd_o[:, sl] = qd.astype(BF16)
        kd_o[:, sl] = kd.astype(BF16)


def _rope_table(half, period):
    lane = jnp.arange(LANES)
    inv_freq = ROPE_THETA ** (-jnp.arange(half, dtype=F32) / half)
    tab = jnp.where((lane % period) < 2 * half, inv_freq[lane % half], 0.0)
    return tab.reshape(1, LANES).astype(F32)


def prep_qkv(proj, positions, qn_m, kn_m, qn_d, kn_d):
    t = proj.shape[0]
    tm = MOBA_BLOCK
    w = MIX_WIDTH
    row = lambda i: (i, 0)
    col_spec = [pl.BlockSpec((tm, w), functools.partial(lambda i, c: (i, c), c=c)) for c in (0, 1, 3, 4)]
    vec = pl.BlockSpec((1, LANES), lambda i: (0, 0))
    bf = jax.ShapeDtypeStruct((t, w), BF16)
    out_shape = [bf, jax.ShapeDtypeStruct((t, w), F32), bf, bf, bf,
                 jax.ShapeDtypeStruct((t // tm, N_HEADS, HEAD_DIM), F32)]
    out_specs = [pl.BlockSpec((tm, w), row)] * 5 + [
        pl.BlockSpec((1, N_HEADS, HEAD_DIM), lambda i: (i, 0, 0))]
    return pl.pallas_call(
        _prep_kernel,
        grid=(t // tm,),
        in_specs=[pl.BlockSpec((tm, 1), row), vec, vec, vec, vec, vec, vec] + col_spec,
        out_specs=out_specs,
        out_shape=out_shape,
        compiler_params=_params("parallel"),
        name="prep_qkv",
    )(positions.reshape(t, 1), _rope_table(ROT_HALF, HEAD_DIM), _rope_table(ROT_HALF_DIFF, DIFF_SUB),
      qn_m.reshape(1, LANES), kn_m.reshape(1, LANES),
      jnp.tile(qn_d, 2).reshape(1, LANES), jnp.tile(kn_d, 2).reshape(1, LANES),
      *([proj] * 4))


def _ones_column(rows, at=0):
    return (lax.broadcasted_iota(jnp.int32, (rows, LANES), 1) == at).astype(BF16)


def _value_with_ones(v):
    return jnp.concatenate([v, jnp.ones(v.shape, BF16)], axis=1)


def _moba_kernel(q_ref, qf_ref, kmean_ref, k_ref, v_ref, o_ref, knorm_ref, *, tk):
    blk = MOBA_BLOCK
    qi = pl.program_id(2)
    scale = HEAD_DIM ** -0.5
    big = NEG_INF / scale
    nbp = kmean_ref.shape[0]

    @pl.when(qi == 0)
    def _():
        knorm_ref[...] = jnp.broadcast_to(_max_row_norm(k_ref, 0, HEAD_DIM), knorm_ref.shape)

    gate = lax.dot_general(kmean_ref[...], qf_ref[...], _NT, precision=lax.Precision.HIGHEST,
                           preferred_element_type=F32)
    rid = lax.broadcasted_iota(jnp.int32, gate.shape, 0)
    rank = jnp.zeros(gate.shape, jnp.int32)
    for m in range(nbp):
        g_m = gate[m:m + 1, :]
        beats = (g_m > gate) | ((g_m == gate) & (rid > m))
        rank = rank + jnp.where(beats & (m < qi), 1, 0)
    allowed = ((rid < qi) & (rank < MOBA_TOPK)) | (rid == qi)
    pen_t = jnp.where(allowed, 0.0, big)
    pen_t = jnp.concatenate([pen_t, jnp.full((LANES - nbp, blk), big, F32)], axis=0)
    penalty = pen_t.T

    q = q_ref[...]
    lane = lax.broadcasted_iota(jnp.int32, (blk, LANES), 1)
    q_pos = qi * blk + lax.broadcasted_iota(jnp.int32, (blk, tk), 0)
    col = lax.broadcasted_iota(jnp.int32, (blk, tk), 1)
    krow = lax.broadcasted_iota(jnp.int32, (tk, LANES), 0)
    klane = lax.broadcasted_iota(jnp.int32, (tk, LANES), 1)
    n_tiles = (qi * blk) // tk + 1

    def scores(n, q_aug, shift_lane, causal):
        off = pl.multiple_of(n * tk, tk)
        kb = k_ref[pl.ds(off, tk), :]
        block_of_row = n * (tk // blk) + krow // blk
        marks = ((klane == block_of_row) | (klane == shift_lane)).astype(BF16)
        s = lax.dot_general(q_aug, jnp.concatenate([kb, marks], axis=1), _NT,
                            preferred_element_type=F32) * scale
        if causal:
            s = jnp.where(off + col <= q_pos, s, NEG_INF)
        return s, off

    q_max = jnp.concatenate([q, penalty.astype(BF16)], axis=1)

    def sweep_max(n, best):
        s, _ = scores(n, q_max, -1, True)
        return jnp.maximum(best, s)

    def exact_row_max():
        best = lax.fori_loop(0, n_tiles, sweep_max, jnp.full((blk, tk), NEG_INF, F32))
        return jnp.max(best, axis=-1, keepdims=True)

    qsq = q.astype(F32)
    bound = jnp.sqrt(jnp.sum(qsq * qsq, axis=-1, keepdims=True)) * knorm_ref[0:1, 0:1] * scale
    shift = lax.cond(2.0 * jnp.max(bound) < SHIFT_SLACK, lambda: bound, exact_row_max)
    shift_lane = LANES - 1
    q_sum = jnp.concatenate(
        [q, jnp.where(lane == shift_lane, -shift / scale, penalty).astype(BF16)], axis=1)

    def tile_sum(n, causal):
        s, off = scores(n, q_sum, shift_lane, causal)
        vb = v_ref[pl.ds(off, tk), :]
        return jnp.dot(jnp.exp(s).astype(BF16), _value_with_ones(vb), preferred_element_type=F32)

    acc = lax.fori_loop(0, n_tiles - 1, lambda n, acc: acc + tile_sum(n, False),
                        jnp.zeros((blk, 2 * HEAD_DIM), F32))
    acc = acc + tile_sum(n_tiles - 1, True)
    o_ref[...] = (acc[:, :HEAD_DIM] / acc[:, HEAD_DIM:]).astype(o_ref.dtype)


def moba_attention(q, qf, kmean, k, v, v_group, batch, seq, tk=512):
    t = q.shape[0]
    blk = MOBA_BLOCK
    nq = seq // blk
    tk = min(tk, seq)
    qspec = pl.BlockSpec((blk, HEAD_DIM), lambda b, h, i: (b * nq + i, h))
    kspec = pl.BlockSpec((seq, HEAD_DIM), lambda b, h, i: (b, h))
    vspec = pl.BlockSpec((seq, HEAD_DIM), lambda b, h, i: (b, v_group * N_HEADS + h))
    return pl.pallas_call(
        functools.partial(_moba_kernel, tk=tk),
        grid=(batch, N_HEADS, nq),
        in_specs=[qspec, qspec,
                  pl.BlockSpec((None, None, kmean.shape[2], HEAD_DIM), lambda b, h, i: (b, h, 0, 0)),
                  kspec, vspec],
        out_specs=qspec,
        out_shape=jax.ShapeDtypeStruct((t, MIX_WIDTH), BF16),
        scratch_shapes=[pltpu.VMEM((8, LANES), F32)],
        compiler_params=_params("parallel", "parallel", "arbitrary"),
        name="moba_attention",
    )(q, qf, kmean, k, v)


SHIFT_SLACK = 80.0


def _max_row_norm(x_ref, lane_lo, lane_hi, chunk=512):
    n_rows = x_ref.shape[0]
    chunk = min(chunk, n_rows)
    lane = lax.broadcasted_iota(jnp.int32, (chunk, x_ref.shape[1]), 1)
    keep = (lane >= lane_lo) & (lane < lane_hi)

    def body(c, best):
        x = x_ref[pl.ds(pl.multiple_of(c * chunk, chunk), chunk), :].astype(F32)
        sq = jnp.sum(jnp.where(keep, x * x, 0.0), axis=-1, keepdims=True)
        return jnp.maximum(best, jnp.max(sq, axis=0, keepdims=True))

    return jnp.sqrt(lax.fori_loop(0, n_rows // chunk, body, jnp.zeros((1, 1), F32)))


def _diff_kernel(lq1_ref, lk1_ref, lq2_ref, lk2_ref, g_ref, q_ref, k_ref, v_ref, o_ref, knorm_ref,
                 *, tq, tk, lam_init):
    qi = pl.program_id(2)

    @pl.when(qi == 0)
    def _():
        knorm_ref[0] = jnp.broadcast_to(_max_row_norm(k_ref, 0, DIFF_SUB), knorm_ref.shape[1:])
        knorm_ref[1] = jnp.broadcast_to(_max_row_norm(k_ref, DIFF_SUB, HEAD_DIM), knorm_ref.shape[1:])

    q = q_ref[...] * (DIFF_SUB ** -0.5)
    lane = lax.broadcasted_iota(jnp.int32, q.shape, 1)
    zero = jnp.zeros_like(q)
    qq = jnp.concatenate([jnp.where(lane < DIFF_SUB, q, zero),
                          jnp.where(lane < DIFF_SUB, zero, q)], axis=0)
    rows = 2 * tq
    q_pos = qi * tq + (lax.broadcasted_iota(jnp.int32, (rows, tk), 0) & (tq - 1))
    col = lax.broadcasted_iota(jnp.int32, (rows, tk), 1)
    n_full = (qi * tq) // tk
    last = pl.multiple_of(n_full * tk, tk)
    causal = last + col <= q_pos

    def lane_tile_max(s):
        out = s[:, :LANES]
        for c in range(LANES, tk, LANES):
            out = jnp.maximum(out, s[:, c:c + LANES])
        return out

    def sweep_max(n, best):
        kb = k_ref[pl.ds(pl.multiple_of(n * tk, tk), tk), :]
        return jnp.maximum(best, lane_tile_max(lax.dot_general(qq, kb, _NT, preferred_element_type=F32)))

    def exact_row_max():
        best = lax.fori_loop(0, n_full, sweep_max, jnp.full((rows, LANES), NEG_INF, F32))
        s_last = lax.dot_general(qq, k_ref[pl.ds(last, tk), :], _NT, preferred_element_type=F32)
        best = jnp.maximum(best, lane_tile_max(jnp.where(causal, s_last, NEG_INF)))
        return jnp.max(best, axis=-1, keepdims=True)

    qf = qq.astype(F32)
    q_norm = jnp.sqrt(jnp.sum(qf * qf, axis=-1, keepdims=True))
    first_half = lax.broadcasted_iota(jnp.int32, (rows, 1), 0) < tq
    bound = q_norm * jnp.where(first_half, knorm_ref[0][0:1, 0:1], knorm_ref[1][0:1, 0:1])
    shift = lax.cond(2.0 * jnp.max(bound) < SHIFT_SLACK, lambda: bound, exact_row_max)

    lane_r = lax.broadcasted_iota(jnp.int32, (rows, LANES), 1)
    q_aug = jnp.concatenate([qq, jnp.where(lane_r == 0, -shift, 0.0).astype(BF16)], axis=1)
    ones_col = _ones_column(tk)

    def tile_sum(off, mask):
        k_aug = jnp.concatenate([k_ref[pl.ds(off, tk), :], ones_col], axis=1)
        v_aug = _value_with_ones(v_ref[pl.ds(off, tk), :])
        outs = []
        for half in range(2):
            s = lax.dot_general(q_aug[half * tq:(half + 1) * tq], k_aug, _NT,
                                preferred_element_type=F32)
            if mask is not None:
                s = jnp.where(mask[half * tq:(half + 1) * tq], s, NEG_INF)
            outs.append(jnp.dot(jnp.exp(s).astype(BF16), v_aug, preferred_element_type=F32))
        return jnp.concatenate(outs, axis=0)

    def sweep_sum(n, acc):
        return acc + tile_sum(pl.multiple_of(n * tk, tk), None)

    acc = lax.fori_loop(0, n_full, sweep_sum, jnp.zeros((rows, 2 * HEAD_DIM), F32))
    acc = acc + tile_sum(last, causal)

    lam = (jnp.exp(jnp.sum(lq1_ref[...] * lk1_ref[...], axis=-1, keepdims=True))
           - jnp.exp(jnp.sum(lq2_ref[...] * lk2_ref[...], axis=-1, keepdims=True)) + lam_init)
    out = acc[:, :HEAD_DIM] / acc[:, HEAD_DIM:]
    o = out[:tq] - lam * out[tq:]
    ms = jnp.mean(o * o, axis=-1, keepdims=True)
    o_ref[...] = (o * lax.rsqrt(ms + NORM_EPS) * g_ref[...] * (1.0 - lam_init)).astype(o_ref.dtype)


def diff_attention(q, k, v, v_group, lq1, lk1, lq2, lk2, subln, lam_init, batch, seq, tq=256, tk=512):
    t = q.shape[0]
    tq, tk = min(tq, seq), min(tk, seq)
    nq = seq // tq
    qspec = pl.BlockSpec((tq, HEAD_DIM), lambda b, h, i: (b * nq + i, h))
    kspec = pl.BlockSpec((seq, HEAD_DIM), lambda b, h, i: (b, h))
    vspec = pl.BlockSpec((seq, HEAD_DIM), lambda b, h, i: (b, v_group * N_HEADS + h))
    lspec = pl.BlockSpec((1, DIFF_SUB), lambda b, h, i: (0, 0))
    return pl.pallas_call(
        functools.partial(_diff_kernel, tq=tq, tk=tk, lam_init=lam_init),
        grid=(batch, N_HEADS, nq),
        in_specs=[lspec, lspec, lspec, lspec,
                  pl.BlockSpec((1, HEAD_DIM), lambda b, h, i: (0, 0)), qspec, kspec, vspec],
        out_specs=qspec,
        out_shape=jax.ShapeDtypeStruct((t, MIX_WIDTH), BF16),
        scratch_shapes=[pltpu.VMEM((2, 8, LANES), F32)],
        compiler_params=_params("parallel", "parallel", "arbitrary"),
        name="diff_attention",
    )(lq1.reshape(1, DIFF_SUB), lk1.reshape(1, DIFF_SUB), lq2.reshape(1, DIFF_SUB),
      lk2.reshape(1, DIFF_SUB), subln.reshape(1, HEAD_DIM), q, k, v)


def _stick_kernel(q_ref, k_ref, v_ref, o_ref, *, tq, tk):
    qi = pl.program_id(2)
    scale = HEAD_DIM ** -0.5
    n_heads = q_ref.shape[1] // HEAD_DIM
    upper = (lax.broadcasted_iota(jnp.int32, (tk, tk), 0)
             > lax.broadcasted_iota(jnp.int32, (tk, tk), 1)).astype(BF16)
    q_pos = qi * tq + lax.broadcasted_iota(jnp.int32, (tq, tk), 0)
    col = lax.broadcasted_iota(jnp.int32, (tq, tk), 1)
    n_diag = tq // tk
    n_past = qi * n_diag

    def tile(n, hd, later, acc, masked):
        off = pl.multiple_of(n * tk, tk)
        cols = slice(hd * HEAD_DIM, (hd + 1) * HEAD_DIM)
        kb = k_ref[pl.ds(off, tk), cols]
        vb = v_ref[pl.ds(off, tk), cols]
        z = lax.dot_general(q_ref[:, cols], kb, _NT, preferred_element_type=F32) * scale
        sp = jnp.maximum(z, 0.0) + jnp.log(1.0 + jnp.exp(-jnp.abs(z)))
        if masked:
            strict = (off + col) < q_pos
            sp = jnp.where(strict, sp, 0.0)
        hi = sp.astype(BF16)
        lo = (sp - hi.astype(F32)).astype(BF16)
        between = (jnp.dot(hi, upper, preferred_element_type=F32)
                   + jnp.dot(lo, upper, preferred_element_type=F32))
        a = jnp.exp(z - sp - between - later)
        if masked:
            a = jnp.where(strict, a, 0.0)
        acc = acc + jnp.dot(a.astype(BF16), vb, preferred_element_type=F32)
        return later + jnp.sum(sp, axis=-1, keepdims=True), acc

    state = [(jnp.zeros((tq, 1), F32), jnp.zeros((tq, HEAD_DIM), F32)) for _ in range(n_heads)]
    for d in range(n_diag):
        state = [tile(n_past + n_diag - 1 - d, hd, *state[hd], True) for hd in range(n_heads)]

    def body(it, carry):
        return tuple(tile(n_past - 1 - it, hd, *carry[hd], False) for hd in range(n_heads))

    state = lax.fori_loop(0, n_past, body, tuple(state))
    for hd in range(n_heads):
        o_ref[:, hd * HEAD_DIM:(hd + 1) * HEAD_DIM] = state[hd][1].astype(o_ref.dtype)


def stick_attention(qkv, q_group, batch, seq, tq=512, tk=256, heads_per_step=2):
    t = qkv.shape[0]
    tq, tk = min(tq, seq), min(tk, seq)
    nq = seq // tq
    width = heads_per_step * HEAD_DIM
    per_group = MIX_WIDTH // width

    def col(group):
        return lambda b, h, i: (b, group * per_group + h)

    qspec = pl.BlockSpec((tq, width), lambda b, h, i: (b * nq + i, q_group * per_group + h))
    kspec = pl.BlockSpec((seq, width), col(q_group + 1))
    vspec = pl.BlockSpec((seq, width), col(q_group + 2))
    return pl.pallas_call(
        functools.partial(_stick_kernel, tq=tq, tk=tk),
        grid=(batch, N_HEADS // heads_per_step, nq),
        in_specs=[qspec, kspec, vspec],
        out_specs=pl.BlockSpec((tq, width), lambda b, h, i: (b * nq + i, h)),
        out_shape=jax.ShapeDtypeStruct((t, MIX_WIDTH), BF16),
        compiler_params=_params("parallel", "parallel", "arbitrary"),
        name="stick_attention",
    )(qkv, qkv, qkv)


def _merge_kernel(om_ref, od_ref, os_ref, w_ref, gm_ref, gd_ref, gs_ref, o_ref):
    acc = None
    for k, (g_ref, x_ref) in enumerate(((gm_ref, om_ref), (gd_ref, od_ref), (gs_ref, os_ref))):
        term = jax.nn.sigmoid(g_ref[...].astype(F32)) * jnp.dot(
            x_ref[...], w_ref[k].astype(BF16), preferred_element_type=F32)
        acc = term if acc is None else acc + term
    o_ref[...] = acc.astype(o_ref.dtype)


def merge_branches(o_m, o_d, o_s, w_branch, layer, proj, gate_col, tm=1024, tn=512):
    t, kdim = o_m.shape
    d = w_branch.shape[3]
    tm, tn = min(tm, t), min(tn, d)
    ospec = pl.BlockSpec((tm, kdim), lambda i, j: (i, 0))
    gspecs = [pl.BlockSpec((tm, tn), functools.partial(lambda i, j, c: (i, c + j), c=(gate_col + c * d) // tn))
              for c in range(3)]
    return pl.pallas_call(
        _merge_kernel,
        grid=(t // tm, d // tn),
        in_specs=[ospec, ospec, ospec,
                  pl.BlockSpec((None, 3, kdim, tn), lambda i, j: (layer, 0, 0, j))] + gspecs,
        out_specs=pl.BlockSpec((tm, tn), lambda i, j: (i, j)),
        out_shape=jax.ShapeDtypeStruct((t, d), BF16),
        compiler_params=_params("parallel", "parallel"),
        name="merge_branches",
    )(o_m, o_d, o_s, w_branch, proj, proj, proj)


def _extract_top(x, rounds, break_ties):
    rows, n = x.shape
    rid = lax.broadcasted_iota(jnp.int32, (rows, n), 0)
    kid = lax.broadcasted_iota(jnp.int32, (rounds, n), 0)

    def body(k, carry):
        x, rank, vals = carry
        best = jnp.max(x, axis=0, keepdims=True)
        hit = x == best
        if break_ties:
            hit = rid == jnp.min(jnp.where(hit, rid, rows), axis=0, keepdims=True)
        rank = jnp.where(hit, k, rank)
        x = jnp.where(hit, -jnp.inf, x)
        vals = jnp.where(kid == k, best, vals)
        return x, rank, vals

    init = (x, jnp.full((rows, n), rounds, jnp.int32), jnp.zeros((rounds, n), F32))
    _, rank, vals = lax.fori_loop(0, rounds, body, init)
    return vals, rank


def _peer_score_kernel(q_ref, keys_ref, cb_o, eb_o, na_o, wa_o):
    nk, topk = PEER_NKEYS, PEER_TOPK
    s = lax.dot_general(keys_ref[...], q_ref[...], _NT, precision=lax.Precision.HIGHEST,
                        preferred_element_type=F32)

    def taken_off(rank):
        taken = jnp.sum((rank < topk).astype(jnp.int32), axis=0, keepdims=True)
        return jnp.max(jnp.where(taken != topk, 1, 0))

    suspect = _peer_select(s, cb_o, eb_o, na_o, wa_o, False, taken_off)

    @pl.when(suspect > 0)
    def _():
        _peer_select(s, cb_o, eb_o, na_o, wa_o, True, taken_off)


def _peer_select(s, cb_o, eb_o, na_o, wa_o, break_ties, taken_off):
    nk, topk = PEER_NKEYS, PEER_TOPK
    sa, sb = s[:nk], s[nk:]
    va, rank_a = _extract_top(sa, topk, break_ties)
    vb, rank_b = _extract_top(sb, topk, break_ties)
    n = s.shape[1]
    row8 = lax.broadcasted_iota(jnp.int32, (8, n), 0)
    ninf = -jnp.inf
    a = [va[r:r + 1] for r in range(topk)]
    groups = [a[0] + vb]
    for r, lim in ((1, 8), (2, 5), (3, 4), (4, 3)):
        groups.append(jnp.where(row8 < lim, a[r] + vb[0:8], ninf))
    a_mix = jnp.where(row8 < 2, a[5], jnp.where(row8 < 4, a[6], jnp.where(
        row8 < 6, a[7], jnp.where(row8 == 6, a[8], a[9]))))
    b_mix = jnp.where((row8 < 6) & ((row8 & 1) == 1), vb[1:2], vb[0:1])
    groups.append(a_mix + b_mix)
    tail = jnp.full((8, n), ninf, F32)
    for k in range(6):
        tail = jnp.where(row8 == k, a[10 + k] + vb[0:1], tail)
    groups.append(tail)
    cand = jnp.concatenate(groups, axis=0)
    top_s, rank_c = _extract_top(cand, topk, break_ties)
    taken = (rank_c < topk).astype(jnp.int32)
    counts = [jnp.sum(taken[0:16], axis=0, keepdims=True)]
    counts += [jnp.sum(taken[8 + 8 * r:16 + 8 * r], axis=0, keepdims=True) for r in range(1, 5)]
    counts += [taken[48 + 2 * k:49 + 2 * k] + taken[49 + 2 * k:50 + 2 * k] for k in range(3)]
    counts += [taken[54 + k:55 + k] for k in range(8)]
    count = jnp.concatenate(counts, axis=0)
    z = jnp.sum(jnp.exp(top_s - top_s[0:1]), axis=0, keepdims=True)
    na = jnp.zeros(sa.shape, jnp.int32)
    for r in range(topk):
        na = jnp.where(rank_a == r, count[r:r + 1], na)
    cb_o[...] = rank_b.astype(F32)
    eb_o[...] = jnp.exp(sb - vb[0:1])
    na_o[...] = na.astype(F32)
    wa_o[...] = jnp.exp(sa - va[0:1]) / z
    return taken_off(rank_a) + taken_off(rank_b) + taken_off(rank_c)


def peer_scores(q, key_a, key_b, tn=256):
    t = q.shape[0]
    tn = min(tn, t)
    zeros = jnp.zeros_like(key_a)
    keys = jnp.concatenate([jnp.concatenate([key_a, zeros], axis=-1),
                            jnp.concatenate([zeros, key_b], axis=-1)], axis=1)
    out = jax.ShapeDtypeStruct((PEER_HEADS, PEER_NKEYS, t), F32)
    ospec = pl.BlockSpec((None, PEER_NKEYS, tn), lambda i, h: (h, 0, i))
    return pl.pallas_call(
        _peer_score_kernel,
        grid=(t // tn, PEER_HEADS),
        in_specs=[pl.BlockSpec((tn, LANES), lambda i, h: (i, h)),
                  pl.BlockSpec((None, 2 * PEER_NKEYS, LANES), lambda i, h: (h, 0, 0))],
        out_specs=[ospec] * 4,
        out_shape=[out] * 4,
        compiler_params=_params("parallel", "parallel"),
        name="peer_scores",
    )(q, keys)


def _peer_dense_kernel(xt_ref, u_ref, vt_ref, cb_ref, eb_ref, na_ref, wa_ref, o_ref,
                       coef_even, coef_odd, *, te):
    e = pl.program_id(1)
    last = pl.num_programs(1) - 2
    tn = o_ref.shape[1]
    half = tn // 2

    @pl.when(e == 0)
    def _():
        o_ref[...] = jnp.zeros_like(o_ref)
        coef_odd[...] = jnp.zeros_like(coef_odd)

    def gate_into_coef(hid, lane0, coef_out):
        chunk = 16
        for di in range(te // PEER_NKEYS):
            i = jnp.minimum(e, last) * (te // PEER_NKEYS) + di
            for sub in range(0, hid.shape[1], LANES):
                lanes = slice(lane0 + sub, lane0 + sub + LANES)
                n_rows = [na_ref[i, h:h + 1, lanes] for h in range(PEER_HEADS)]
                w_rows = [wa_ref[i, h:h + 1, lanes] for h in range(PEER_HEADS)]
                for r in range(0, PEER_NKEYS, chunk):
                    rs = slice(r, r + chunk)
                    w = None
                    for h in range(PEER_HEADS):
                        term = jnp.where(cb_ref[h, rs, lanes] < n_rows[h],
                                         eb_ref[h, rs, lanes] * w_rows[h], 0.0)
                        w = term if w is None else w + term
                    row0 = di * PEER_NKEYS + r
                    x = hid[row0:row0 + chunk, sub:sub + LANES]
                    act = 0.5 * x * (1.0 + lax.erf(x * (2.0 ** -0.5)))
                    coef_out[row0:row0 + chunk, lanes] = (act * w).astype(BF16)

    def step(coef_out, coef_in):
        lo, hi = slice(0, half), slice(half, tn)
        hid_lo = jnp.dot(u_ref[...], xt_ref[:, lo], preferred_element_type=F32)
        o_ref[:, lo] += jnp.dot(vt_ref[...], coef_in[:, lo], preferred_element_type=F32)
        hid_hi = jnp.dot(u_ref[...], xt_ref[:, hi], preferred_element_type=F32)
        gate_into_coef(hid_lo, 0, coef_out)
        o_ref[:, hi] += jnp.dot(vt_ref[...], coef_in[:, hi], preferred_element_type=F32)
        gate_into_coef(hid_hi, half, coef_out)

    @pl.when((e & 1) == 0)
    def _():
        step(coef_even, coef_odd)

    @pl.when((e & 1) == 1)
    def _():
        step(coef_odd, coef_even)


def peer_dense(xt, u, vt, cb, eb, na, wa, tn=512, te=512):
    d, t = xt.shape
    n_exp = u.shape[0]
    tn, te = min(tn, t), min(te, n_exp)
    n_e = n_exp // te
    once = pl.Buffered(1)
    sspec = pl.BlockSpec((PEER_HEADS, PEER_NKEYS, tn), lambda i, e: (0, 0, i), pipeline_mode=once)
    aspec = pl.BlockSpec((PEER_NKEYS, PEER_HEADS, tn), lambda i, e: (0, 0, i), pipeline_mode=once)
    na, wa = na.transpose(1, 0, 2), wa.transpose(1, 0, 2)
    return pl.pallas_call(
        functools.partial(_peer_dense_kernel, te=te),
        grid=(t // tn, n_e + 1),
        in_specs=[pl.BlockSpec((d, tn), lambda i, e: (0, i), pipeline_mode=once),
                  pl.BlockSpec((te, d), lambda i, e: (jnp.minimum(e, n_e - 1), 0)),
                  pl.BlockSpec((d, te), lambda i, e: (0, jnp.maximum(e - 1, 0))),
                  sspec, sspec, aspec, aspec],
        out_specs=pl.BlockSpec((d, tn), lambda i, e: (0, i)),
        out_shape=jax.ShapeDtypeStruct((d, t), F32),
        scratch_shapes=[pltpu.VMEM((te, tn), BF16), pltpu.VMEM((te, tn), BF16)],
        compiler_params=_params("parallel", "arbitrary"),
        name="peer_dense",
    )(xt, u, vt, cb, eb, na, wa)


def kernel(x, p, positions, norm_mix, w_in, qn_moba, kn_moba, qn_diff, kn_diff, lambda_q1, lambda_k1, lambda_q2, lambda_k2, subln_diff, w_branch, w_out, norm_ffn, peer_wq, peer_key_a, peer_key_b, peer_u, peer_v, norm_ple, w_ple_gate, w_ple_proj):
    b, s, d = x.shape
    t = b * s
    depth = w_in.shape[0]
    nb = s // MOBA_BLOCK
    gate_col = 9 * MIX_WIDTH
    h = x.reshape(t, d)
    p_flat = p.reshape(depth, t, -1)
    for i in range(depth):
        n1 = rmsnorm(h, norm_mix[i])
        proj = matmul(n1, w_in, i, out_dtype=BF16)
        qm, qmf, km, qd, kd, kmean = prep_qkv(
            proj, positions, qn_moba[i], kn_moba[i], qn_diff[i], kn_diff[i])
        kmean = kmean.reshape(b, nb, N_HEADS, HEAD_DIM).transpose(0, 2, 1, 3)
        kmean = jnp.pad(kmean, ((0, 0), (0, 0), (0, -nb % 8), (0, 0)))
        o_m = moba_attention(qm, qmf, kmean, km, proj, 2, b, s)
        lam_init = 0.8 - 0.6 * math.exp(-0.3 * i)
        o_d = diff_attention(qd, kd, proj, 5, lambda_q1[i], lambda_k1[i], lambda_q2[i], lambda_k2[i],
                             subln_diff[i], lam_init, b, s)
        o_s = stick_attention(proj, 6, b, s)
        merged = merge_branches(o_m, o_d, o_s, w_branch, i, proj, gate_col)
        h = matmul(merged, w_out, i, mode="residual", res=h)
        n2, n2_t = rmsnorm(h, norm_ffn[i], with_transpose=True)
        q = matmul(n2, peer_wq, i)
        cb, eb, na, wa = peer_scores(q, peer_key_a[i], peer_key_b[i])
        ffn_t = peer_dense(n2_t, to_bf16(peer_u, i), to_bf16(peer_v, i, transpose=True), cb, eb, na, wa)
        h, n3 = residual_rmsnorm(h, ffn_t, norm_ple[i])
        h = matmul(n3, w_ple_gate, i, mode="ple", res=h, p=p_flat, wp=w_ple_proj)
    return h.reshape(b, s, d)
```

```python
import functools
import math

import jax
import jax.numpy as jnp
from jax import lax
from jax.experimental import pallas as pl
from jax.experimental.pallas import tpu as pltpu

F32 = jnp.float32
BF16 = jnp.bfloat16

LANES = 128
HEAD_DIM = 128
N_HEADS = 8
MIX_WIDTH = N_HEADS * HEAD_DIM
DIFF_SUB = HEAD_DIM // 2
ROPE_THETA = 500000.0
ROT_HALF = HEAD_DIM // 8
ROT_HALF_DIFF = DIFF_SUB // 8
MOBA_BLOCK = 256
MOBA_TOPK = 3
PEER_HEADS = 8
PEER_NKEYS = 128
PEER_TOPK = 16
PEER_TILE = 512
NORM_EPS = 1e-6
NEG_INF = -1e30
V7X_VMEM_BYTES = 64 * 1024 * 1024
VMEM_LIMIT = V7X_VMEM_BYTES * 7 // 8

_NT = (((1,), (1,)), ((), ()))


def _params(*sem):
    return pltpu.CompilerParams(dimension_semantics=sem, vmem_limit_bytes=VMEM_LIMIT)


def _rmsnorm_kernel(x_ref, g_ref, o_ref, *maybe_ot_ref):
    x = x_ref[...]
    ms = jnp.mean(x * x, axis=-1, keepdims=True)
    y = x * lax.rsqrt(ms + NORM_EPS) * g_ref[...]
    o_ref[...] = y.astype(o_ref.dtype)
    for ot_ref in maybe_ot_ref:
        ot_ref[...] = y.T.astype(ot_ref.dtype)


def rmsnorm(x, gain, tm=256, with_transpose=False):
    t, d = x.shape
    tm = min(tm, t)
    out_specs = [pl.BlockSpec((tm, d), lambda i: (i, 0))]
    out_shape = [jax.ShapeDtypeStruct((t, d), BF16)]
    if with_transpose:
        out_specs.append(pl.BlockSpec((d, tm), lambda i: (0, i)))
        out_shape.append(jax.ShapeDtypeStruct((d, t), BF16))
    outs = pl.pallas_call(
        _rmsnorm_kernel,
        grid=(t // tm,),
        in_specs=[pl.BlockSpec((tm, d), lambda i: (i, 0)),
                  pl.BlockSpec((1, d), lambda i: (0, 0))],
        out_specs=out_specs,
        out_shape=out_shape,
        compiler_params=_params("parallel"),
        name="rmsnorm_t" if with_transpose else "rmsnorm",
    )(x, gain.reshape(1, d))
    return outs if with_transpose else outs[0]


def _residual_rmsnorm_kernel(x_ref, yt_ref, g_ref, h_ref, n_ref):
    h = x_ref[...] + yt_ref[...].T
    h_ref[...] = h
    ms = jnp.mean(h * h, axis=-1, keepdims=True)
    n_ref[...] = (h * lax.rsqrt(ms + NORM_EPS) * g_ref[...]).astype(n_ref.dtype)


def residual_rmsnorm(x, y_t, gain, tm=256):
    t, d = x.shape
    tm = min(tm, t)
    row = pl.BlockSpec((tm, d), lambda i: (i, 0))
    return pl.pallas_call(
        _residual_rmsnorm_kernel,
        grid=(t // tm,),
        in_specs=[row, pl.BlockSpec((d, tm), lambda i: (0, i)), pl.BlockSpec((1, d), lambda i: (0, 0))],
        out_specs=[row, row],
        out_shape=[jax.ShapeDtypeStruct((t, d), F32), jax.ShapeDtypeStruct((t, d), BF16)],
        compiler_params=_params("parallel"),
        name="residual_rmsnorm",
    )(x, y_t, gain.reshape(1, d))


def _relayout_kernel(x_ref, o_ref, *, transpose_tile):
    x = x_ref[...]
    if transpose_tile is None:
        o_ref[...] = x.astype(o_ref.dtype)
    else:
        xt = x.T.astype(o_ref.dtype)
        for a in range(o_ref.shape[0]):
            o_ref[a] = xt[:, a * transpose_tile:(a + 1) * transpose_tile]


def to_bf16(x, layer, transpose_tile=None, tb=1024):
    _, r, c = x.shape
    tr, tc = min(tb, r), min(tb, c)
    if transpose_tile is None:
        out_spec, out_dims = pl.BlockSpec((tr, tc), lambda i, j: (i, j)), (r, c)
    else:
        n = min(transpose_tile, tr)
        transpose_tile = n
        out_spec = pl.BlockSpec((tr // n, tc, n), lambda i, j: (i, j, 0))
        out_dims = (r // n, c, n)
    return pl.pallas_call(
        functools.partial(_relayout_kernel, transpose_tile=transpose_tile),
        grid=(r // tr, c // tc),
        in_specs=[pl.BlockSpec((None, tr, tc), lambda i, j: (layer, i, j))],
        out_specs=out_spec,
        out_shape=jax.ShapeDtypeStruct(out_dims, BF16),
        compiler_params=_params("parallel", "parallel"),
        name="to_bf16" if transpose_tile is None else "to_bf16_t",
    )(x)


def _mm_kernel(*refs, mode):
    if mode == "plain":
        a_ref, b_ref, o_ref = refs
    elif mode == "residual":
        a_ref, b_ref, r_ref, o_ref = refs
    else:
        a_ref, b_ref, r_ref, p_ref, wp_ref, o_ref = refs
    acc = jnp.dot(a_ref[...], b_ref[...].astype(BF16), preferred_element_type=F32)
    if mode == "plain":
        o_ref[...] = acc.astype(o_ref.dtype)
    elif mode == "residual":
        o_ref[...] = (r_ref[...] + acc).astype(o_ref.dtype)
    else:
        emb = jnp.dot(p_ref[...].astype(BF16), wp_ref[...].astype(BF16),
                      preferred_element_type=F32)
        o_ref[...] = (r_ref[...] + jax.nn.sigmoid(acc) * emb).astype(o_ref.dtype)


def matmul(a, w, layer, *, mode="plain", res=None, p=None, wp=None, out_dtype=F32, tm=2048, tn=None):
    m, kdim = a.shape
    n = w.shape[2]
    if tn is None:
        tn = 256 if mode == "ple" else 512
    tm, tn = min(tm, m), min(tn, n)
    in_specs = [pl.BlockSpec((tm, kdim), lambda i, j: (i, 0), pipeline_mode=pl.Buffered(1)),
                pl.BlockSpec((None, kdim, tn), lambda i, j: (layer, 0, j))]
    args = [a, w]
    if mode in ("residual", "ple"):
        in_specs.append(pl.BlockSpec((tm, tn), lambda i, j: (i, j)))
        args.append(res)
    if mode == "ple":
        pd = p.shape[2]
        in_specs += [pl.BlockSpec((None, tm, pd), lambda i, j: (layer, i, 0)),
                     pl.BlockSpec((None, pd, tn), lambda i, j: (layer, 0, j))]
        args += [p, wp]
    return pl.pallas_call(
        functools.partial(_mm_kernel, mode=mode),
        grid=(m // tm, n // tn),
        in_specs=in_specs,
        out_specs=pl.BlockSpec((tm, tn), lambda i, j: (i, j)),
        out_shape=jax.ShapeDtypeStruct((m, n), out_dtype),
        compiler_params=_params("parallel", "arbitrary"),
        name="matmul_" + mode,
    )(*args)


def _rope(x, cos, sin, half, period):
    lane = lax.broadcasted_iota(jnp.int32, x.shape, 1)
    first = (lane & (period - 1)) < half
    partner = jnp.where(first, pltpu.roll(x, LANES - half, 1), pltpu.roll(x, half, 1))
    return x * cos + partner * jnp.where(first, -sin, sin)


def _prep_kernel(pos_ref, fm_ref, fd_ref, qng_ref, kng_ref, qdg_ref, kdg_ref,
                 qm_i, km_i, qd_i, kd_i, qm_o, qmf_o, km_o, qd_o, kd_o, kmean_o):
    pos = pos_ref[...].astype(F32)
    ang_m = pos * fm_ref[...]
    ang_d = pos * fd_ref[...]
    cos_m, sin_m = jnp.cos(ang_m), jnp.sin(ang_m)
    cos_d, sin_d = jnp.cos(ang_d), jnp.sin(ang_d)
    lane = lax.broadcasted_iota(jnp.int32, ang_m.shape, 1)
    low = lane < DIFF_SUB

    def norm_full(x, g):
        ms = jnp.mean(x * x, axis=-1, keepdims=True)
        return x * lax.rsqrt(ms + NORM_EPS) * g

    def norm_halves(x, g):
        xx = x * x
        lo = jnp.sum(jnp.where(low, xx, 0.0), axis=-1, keepdims=True)
        hi = jnp.sum(jnp.where(low, 0.0, xx), axis=-1, keepdims=True)
        ms = jnp.where(low, lo, hi) * (1.0 / DIFF_SUB)
        return x * lax.rsqrt(ms + NORM_EPS) * g

    for h in range(N_HEADS):
        sl = slice(h * HEAD_DIM, (h + 1) * HEAD_DIM)
        qm = _rope(norm_full(qm_i[:, sl].astype(F32), qng_ref[...]), cos_m, sin_m, ROT_HALF, HEAD_DIM)
        km = _rope(norm_full(km_i[:, sl].astype(F32), kng_ref[...]), cos_m, sin_m, ROT_HALF, HEAD_DIM)
        qmf_o[:, sl] = qm
        qm_o[:, sl] = qm.astype(BF16)
        km_o[:, sl] = km.astype(BF16)
        kmean_o[0, h:h + 1, :] = jnp.mean(km, axis=0, keepdims=True)
        qd = _rope(norm_halves(qd_i[:, sl].astype(F32), qdg_ref[...]), cos_d, sin_d,
                   ROT_HALF_DIFF, DIFF_SUB)
        kd = _rope(norm_halves(kd_i[:, sl].astype(F32), kdg_ref[...]), cos_d, sin_d,
                   ROT_HALF_DIFF, DIFF_SUB)
        qd_o[:, sl] = qd.astype(BF16)
        kd_o[:, sl] = kd.astype(BF16)


def _rope_table(half, period):
    lane = jnp.arange(LANES)
    inv_freq = ROPE_THETA ** (-jnp.arange(half, dtype=F32) / half)
    tab = jnp.where((lane % period) < 2 * half, inv_freq[lane % half], 0.0)
    return tab.reshape(1, LANES).astype(F32)


def prep_qkv(proj, positions, qn_m, kn_m, qn_d, kn_d):
    t = proj.shape[0]
    tm = MOBA_BLOCK
    w = MIX_WIDTH
    row = lambda i: (i, 0)
    col_spec = [pl.BlockSpec((tm, w), functools.partial(lambda i, c: (i, c), c=c)) for c in (0, 1, 3, 4)]
    vec = pl.BlockSpec((1, LANES), lambda i: (0, 0))
    bf = jax.ShapeDtypeStruct((t, w), BF16)
    out_shape = [bf, jax.ShapeDtypeStruct((t, w), F32), bf, bf, bf,
                 jax.ShapeDtypeStruct((t // tm, N_HEADS, HEAD_DIM), F32)]
    out_specs = [pl.BlockSpec((tm, w), row)] * 5 + [
        pl.BlockSpec((1, N_HEADS, HEAD_DIM), lambda i: (i, 0, 0))]
    return pl.pallas_call(
        _prep_kernel,
        grid=(t // tm,),
        in_specs=[pl.BlockSpec((tm, 1), row), vec, vec, vec, vec, vec, vec] + col_spec,
        out_specs=out_specs,
        out_shape=out_shape,
        compiler_params=_params("parallel"),
        name="prep_qkv",
    )(positions.reshape(t, 1), _rope_table(ROT_HALF, HEAD_DIM), _rope_table(ROT_HALF_DIFF, DIFF_SUB),
      qn_m.reshape(1, LANES), kn_m.reshape(1, LANES),
      jnp.tile(qn_d, 2).reshape(1, LANES), jnp.tile(kn_d, 2).reshape(1, LANES),
      *([proj] * 4))


def _ones_column(rows, at=0):
    return (lax.broadcasted_iota(jnp.int32, (rows, LANES), 1) == at).astype(BF16)


def _value_with_ones(v):
    return jnp.concatenate([v, jnp.ones(v.shape, BF16)], axis=1)


def _moba_kernel(q_ref, qf_ref, kmean_ref, k_ref, v_ref, o_ref, knorm_ref, *, tk):
    blk = MOBA_BLOCK
    tq = q_ref.shape[0]
    qi = pl.program_id(2)
    scale = HEAD_DIM ** -0.5
    big = NEG_INF / scale
    nbp = kmean_ref.shape[0]

    @pl.when(qi == 0)
    def _():
        knorm_ref[...] = jnp.broadcast_to(_max_row_norm(k_ref, 0, HEAD_DIM), knorm_ref.shape)

    gate = lax.dot_general(kmean_ref[...], qf_ref[...], _NT, precision=lax.Precision.HIGHEST,
                           preferred_element_type=F32)
    rid = lax.broadcasted_iota(jnp.int32, gate.shape, 0)
    own = qi * (tq // blk) + lax.broadcasted_iota(jnp.int32, gate.shape, 1) // blk
    rank = jnp.zeros(gate.shape, jnp.int32)
    for m in range(nbp):
        g_m = gate[m:m + 1, :]
        beats = (g_m > gate) | ((g_m == gate) & (rid > m))
        rank = rank + jnp.where(beats & (m < own), 1, 0)
    allowed = ((rid < own) & (rank < MOBA_TOPK)) | (rid == own)
    pen_t = jnp.where(allowed, 0.0, big)
    pen_t = jnp.concatenate([pen_t, jnp.full((LANES - nbp, tq), big, F32)], axis=0)
    penalty = pen_t.T

    q = q_ref[...]
    lane = lax.broadcasted_iota(jnp.int32, (tq, LANES), 1)
    q_pos = qi * tq + lax.broadcasted_iota(jnp.int32, (tq, tk), 0)
    col = lax.broadcasted_iota(jnp.int32, (tq, tk), 1)
    krow = lax.broadcasted_iota(jnp.int32, (tk, LANES), 0)
    klane = lax.broadcasted_iota(jnp.int32, (tk, LANES), 1)
    n_tiles = (qi * tq) // tk + 1

    def scores(n, q_aug, shift_lane, causal):
        off = pl.multiple_of(n * tk, tk)
        kb = k_ref[pl.ds(off, tk), :]
        block_of_row = n * (tk // blk) + krow // blk
        marks = ((klane == block_of_row) | (klane == shift_lane)).astype(BF16)
        s = lax.dot_general(q_aug, jnp.concatenate([kb, marks], axis=1), _NT,
                            preferred_element_type=F32) * scale
        if causal:
            s = jnp.where(off + col <= q_pos, s, NEG_INF)
        return s, off

    q_max = jnp.concatenate([q, penalty.astype(BF16)], axis=1)

    def sweep_max(n, best):
        s, _ = scores(n, q_max, -1, True)
        return jnp.maximum(best, s)

    def exact_row_max():
        best = lax.fori_loop(0, n_tiles, sweep_max, jnp.full((tq, tk), NEG_INF, F32))
        return jnp.max(best, axis=-1, keepdims=True)

    qsq = q.astype(F32)
    bound = jnp.sqrt(jnp.sum(qsq * qsq, axis=-1, keepdims=True)) * knorm_ref[0:1, 0:1] * scale
    shift = lax.cond(2.0 * jnp.max(bound) < SHIFT_SLACK, lambda: bound, exact_row_max)
    shift_lane = LANES - 1
    q_sum = jnp.concatenate(
        [q, jnp.where(lane == shift_lane, -shift / scale, penalty).astype(BF16)], axis=1)

    def tile_sum(n, causal):
        s, off = scores(n, q_sum, shift_lane, causal)
        vb = v_ref[pl.ds(off, tk), :]
        return jnp.dot(jnp.exp(s).astype(BF16), _value_with_ones(vb), preferred_element_type=F32)

    acc = lax.fori_loop(0, n_tiles - 1, lambda n, acc: acc + tile_sum(n, False),
                        jnp.zeros((tq, 2 * HEAD_DIM), F32))
    acc = acc + tile_sum(n_tiles - 1, True)
    o_ref[...] = (acc[:, :HEAD_DIM] / acc[:, HEAD_DIM:]).astype(o_ref.dtype)


def moba_attention(q, qf, kmean, k, v, v_group, batch, seq, tq=512, tk=512):
    t = q.shape[0]
    tq, tk = min(tq, seq), min(tk, seq)
    assert tq % MOBA_BLOCK == 0 and tk % tq == 0
    nq = seq // tq
    qspec = pl.BlockSpec((tq, HEAD_DIM), lambda b, h, i: (b * nq + i, h))
    kspec = pl.BlockSpec((seq, HEAD_DIM), lambda b, h, i: (b, h))
    vspec = pl.BlockSpec((seq, HEAD_DIM), lambda b, h, i: (b, v_group * N_HEADS + h))
    return pl.pallas_call(
        functools.partial(_moba_kernel, tk=tk),
        grid=(batch, N_HEADS, nq),
        in_specs=[qspec, qspec,
                  pl.BlockSpec((None, None, kmean.shape[2], HEAD_DIM), lambda b, h, i: (b, h, 0, 0)),
                  kspec, vspec],
        out_specs=qspec,
        out_shape=jax.ShapeDtypeStruct((t, MIX_WIDTH), BF16),
        scratch_shapes=[pltpu.VMEM((8, LANES), F32)],
        compiler_params=_params("parallel", "parallel", "arbitrary"),
        name="moba_attention",
    )(q, qf, kmean, k, v)


SHIFT_SLACK = 80.0


def _max_row_norm(x_ref, lane_lo, lane_hi, chunk=512):
    n_rows = x_ref.shape[0]
    chunk = min(chunk, n_rows)
    lane = lax.broadcasted_iota(jnp.int32, (chunk, x_ref.shape[1]), 1)
    keep = (lane >= lane_lo) & (lane < lane_hi)

    def body(c, best):
        x = x_ref[pl.ds(pl.multiple_of(c * chunk, chunk), chunk), :].astype(F32)
        sq = jnp.sum(jnp.where(keep, x * x, 0.0), axis=-1, keepdims=True)
        return jnp.maximum(best, jnp.max(sq, axis=0, keepdims=True))

    return jnp.sqrt(lax.fori_loop(0, n_rows // chunk, body, jnp.zeros((1, 1), F32)))


def _diff_kernel(lq1_ref, lk1_ref, lq2_ref, lk2_ref, g_ref, q_ref, k_ref, v_ref, o_ref, knorm_ref,
                 *, tq, tk, lam_init):
    qi = pl.program_id(2)

    @pl.when(qi == 0)
    def _():
        knorm_ref[0] = jnp.broadcast_to(_max_row_norm(k_ref, 0, DIFF_SUB), knorm_ref.shape[1:])
        knorm_ref[1] = jnp.broadcast_to(_max_row_norm(k_ref, DIFF_SUB, HEAD_DIM), knorm_ref.shape[1:])

    q = q_ref[...] * (DIFF_SUB ** -0.5)
    lane = lax.broadcasted_iota(jnp.int32, q.shape, 1)
    zero = jnp.zeros_like(q)
    qq = jnp.concatenate([jnp.where(lane < DIFF_SUB, q, zero),
                          jnp.where(lane < DIFF_SUB, zero, q)], axis=0)
    rows = 2 * tq
    q_pos = qi * tq + (lax.broadcasted_iota(jnp.int32, (rows, tk), 0) & (tq - 1))
    col = lax.broadcasted_iota(jnp.int32, (rows, tk), 1)
    n_full = (qi * tq) // tk
    last = pl.multiple_of(n_full * tk, tk)
    causal = last + col <= q_pos

    def lane_tile_max(s):
        out = s[:, :LANES]
        for c in range(LANES, tk, LANES):
            out = jnp.maximum(out, s[:, c:c + LANES])
        return out

    def sweep_max(n, best):
        kb = k_ref[pl.ds(pl.multiple_of(n * tk, tk), tk), :]
        return jnp.maximum(best, lane_tile_max(lax.dot_general(qq, kb, _NT, preferred_element_type=F32)))

    def exact_row_max():
        best = lax.fori_loop(0, n_full, sweep_max, jnp.full((rows, LANES), NEG_INF, F32))
        s_last = lax.dot_general(qq, k_ref[pl.ds(last, tk), :], _NT, preferred_element_type=F32)
        best = jnp.maximum(best, lane_tile_max(jnp.where(causal, s_last, NEG_INF)))
        return jnp.max(best, axis=-1, keepdims=True)

    qf = qq.astype(F32)
    q_norm = jnp.sqrt(jnp.sum(qf * qf, axis=-1, keepdims=True))
    first_half = lax.broadcasted_iota(jnp.int32, (rows, 1), 0) < tq
    bound = q_norm * jnp.where(first_half, knorm_ref[0][0:1, 0:1], knorm_ref[1][0:1, 0:1])
    shift = lax.cond(2.0 * jnp.max(bound) < SHIFT_SLACK, lambda: bound, exact_row_max)

    lane_r = lax.broadcasted_iota(jnp.int32, (rows, LANES), 1)
    q_aug = jnp.concatenate([qq, jnp.where(lane_r == 0, -shift, 0.0).astype(BF16)], axis=1)
    ones_col = _ones_column(tk)

    def tile_sum(off, mask):
        k_aug = jnp.concatenate([k_ref[pl.ds(off, tk), :], ones_col], axis=1)
        v_aug = _value_with_ones(v_ref[pl.ds(off, tk), :])
        outs = []
        for half in range(2):
            s = lax.dot_general(q_aug[half * tq:(half + 1) * tq], k_aug, _NT,
                                preferred_element_type=F32)
            if mask is not None:
                s = jnp.where(mask[half * tq:(half + 1) * tq], s, NEG_INF)
            outs.append(jnp.dot(jnp.exp(s).astype(BF16), v_aug, preferred_element_type=F32))
        return jnp.concatenate(outs, axis=0)

    def sweep_sum(n, acc):
        return acc + tile_sum(pl.multiple_of(n * tk, tk), None)

    acc = lax.fori_loop(0, n_full, sweep_sum, jnp.zeros((rows, 2 * HEAD_DIM), F32))
    acc = acc + tile_sum(last, causal)

    lam = (jnp.exp(jnp.sum(lq1_ref[...] * lk1_ref[...], axis=-1, keepdims=True))
           - jnp.exp(jnp.sum(lq2_ref[...] * lk2_ref[...], axis=-1, keepdims=True)) + lam_init)
    out = acc[:, :HEAD_DIM] / acc[:, HEAD_DIM:]
    o = out[:tq] - lam * out[tq:]
    ms = jnp.mean(o * o, axis=-1, keepdims=True)
    o_ref[...] = (o * lax.rsqrt(ms + NORM_EPS) * g_ref[...] * (1.0 - lam_init)).astype(o_ref.dtype)


def diff_attention(q, k, v, v_group, lq1, lk1, lq2, lk2, subln, lam_init, batch, seq, tq=512, tk=512):
    t = q.shape[0]
    tq, tk = min(tq, seq), min(tk, seq)
    nq = seq // tq
    qspec = pl.BlockSpec((tq, HEAD_DIM), lambda b, h, i: (b * nq + i, h))
    kspec = pl.BlockSpec((seq, HEAD_DIM), lambda b, h, i: (b, h))
    vspec = pl.BlockSpec((seq, HEAD_DIM), lambda b, h, i: (b, v_group * N_HEADS + h))
    lspec = pl.BlockSpec((1, DIFF_SUB), lambda b, h, i: (0, 0))
    return pl.pallas_call(
        functools.partial(_diff_kernel, tq=tq, tk=tk, lam_init=lam_init),
        grid=(batch, N_HEADS, nq),
        in_specs=[lspec, lspec, lspec, lspec,
                  pl.BlockSpec((1, HEAD_DIM), lambda b, h, i: (0, 0)), qspec, kspec, vspec],
        out_specs=qspec,
        out_shape=jax.ShapeDtypeStruct((t, MIX_WIDTH), BF16),
        scratch_shapes=[pltpu.VMEM((2, 8, LANES), F32)],
        compiler_params=_params("parallel", "parallel", "arbitrary"),
        name="diff_attention",
    )(lq1.reshape(1, DIFF_SUB), lk1.reshape(1, DIFF_SUB), lq2.reshape(1, DIFF_SUB),
      lk2.reshape(1, DIFF_SUB), subln.reshape(1, HEAD_DIM), q, k, v)


def _stick_kernel(q_ref, k_ref, v_ref, o_ref, *, tq, tk):
    qi = pl.program_id(2)
    scale = HEAD_DIM ** -0.5
    n_heads = q_ref.shape[1] // HEAD_DIM
    upper = (lax.broadcasted_iota(jnp.int32, (tk, tk), 0)
             > lax.broadcasted_iota(jnp.int32, (tk, tk), 1)).astype(BF16)
    q_pos = qi * tq + lax.broadcasted_iota(jnp.int32, (tq, tk), 0)
    col = lax.broadcasted_iota(jnp.int32, (tq, tk), 1)
    n_diag = tq // tk
    n_past = qi * n_diag

    def tile(n, hd, later, acc, masked):
        off = pl.multiple_of(n * tk, tk)
        cols = slice(hd * HEAD_DIM, (hd + 1) * HEAD_DIM)
        kb = k_ref[pl.ds(off, tk), cols]
        vb = v_ref[pl.ds(off, tk), cols]
        z = lax.dot_general(q_ref[:, cols], kb, _NT, preferred_element_type=F32) * scale
        sp = jnp.maximum(z, 0.0) + jnp.log(1.0 + jnp.exp(-jnp.abs(z)))
        if masked:
            strict = (off + col) < q_pos
            sp = jnp.where(strict, sp, 0.0)
        hi = sp.astype(BF16)
        lo = (sp - hi.astype(F32)).astype(BF16)
        between = (jnp.dot(hi, upper, preferred_element_type=F32)
                   + jnp.dot(lo, upper, preferred_element_type=F32))
        a = jnp.exp(z - sp - between - later)
        if masked:
            a = jnp.where(strict, a, 0.0)
        acc = acc + jnp.dot(a.astype(BF16), vb, preferred_element_type=F32)
        return later + jnp.sum(sp, axis=-1, keepdims=True), acc

    state = [(jnp.zeros((tq, 1), F32), jnp.zeros((tq, HEAD_DIM), F32)) for _ in range(n_heads)]
    for d in range(n_diag):
        state = [tile(n_past + n_diag - 1 - d, hd, *state[hd], True) for hd in range(n_heads)]

    def body(it, carry):
        return tuple(tile(n_past - 1 - it, hd, *carry[hd], False) for hd in range(n_heads))

    state = lax.fori_loop(0, n_past, body, tuple(state))
    for hd in range(n_heads):
        o_ref[:, hd * HEAD_DIM:(hd + 1) * HEAD_DIM] = state[hd][1].astype(o_ref.dtype)


def stick_attention(qkv, q_group, batch, seq, tq=512, tk=256, heads_per_step=4):
    t = qkv.shape[0]
    tq, tk = min(tq, seq), min(tk, seq)
    nq = seq // tq
    width = heads_per_step * HEAD_DIM
    per_group = MIX_WIDTH // width

    def col(group):
        return lambda b, h, i: (b, group * per_group + h)

    qspec = pl.BlockSpec((tq, width), lambda b, h, i: (b * nq + i, q_group * per_group + h))
    kspec = pl.BlockSpec((seq, width), col(q_group + 1))
    vspec = pl.BlockSpec((seq, width), col(q_group + 2))
    return pl.pallas_call(
        functools.partial(_stick_kernel, tq=tq, tk=tk),
        grid=(batch, N_HEADS // heads_per_step, nq),
        in_specs=[qspec, kspec, vspec],
        out_specs=pl.BlockSpec((tq, width), lambda b, h, i: (b * nq + i, h)),
        out_shape=jax.ShapeDtypeStruct((t, MIX_WIDTH), BF16),
        compiler_params=_params("parallel", "parallel", "arbitrary"),
        name="stick_attention",
    )(qkv, qkv, qkv)


def _merge_kernel(om_ref, od_ref, os_ref, w_ref, gm_ref, gd_ref, gs_ref, o_ref):
    acc = None
    for k, (g_ref, x_ref) in enumerate(((gm_ref, om_ref), (gd_ref, od_ref), (gs_ref, os_ref))):
        term = jax.nn.sigmoid(g_ref[...].astype(F32)) * jnp.dot(
            x_ref[...], w_ref[k].astype(BF16), preferred_element_type=F32)
        acc = term if acc is None else acc + term
    o_ref[...] = acc.astype(o_ref.dtype)


def merge_branches(o_m, o_d, o_s, w_branch, layer, proj, gate_col, tm=1024, tn=512):
    t, kdim = o_m.shape
    d = w_branch.shape[3]
    tm, tn = min(tm, t), min(tn, d)
    ospec = pl.BlockSpec((tm, kdim), lambda i, j: (i, 0))
    gspecs = [pl.BlockSpec((tm, tn), functools.partial(lambda i, j, c: (i, c + j), c=(gate_col + c * d) // tn))
              for c in range(3)]
    return pl.pallas_call(
        _merge_kernel,
        grid=(t // tm, d // tn),
        in_specs=[ospec, ospec, ospec,
                  pl.BlockSpec((None, 3, kdim, tn), lambda i, j: (layer, 0, 0, j))] + gspecs,
        out_specs=pl.BlockSpec((tm, tn), lambda i, j: (i, j)),
        out_shape=jax.ShapeDtypeStruct((t, d), BF16),
        compiler_params=_params("parallel", "parallel"),
        name="merge_branches",
    )(o_m, o_d, o_s, w_branch, proj, proj, proj)


def _extract_top(x, rounds, break_ties):
    rows, n = x.shape
    rid = lax.broadcasted_iota(jnp.int32, (rows, n), 0)
    kid = lax.broadcasted_iota(jnp.int32, (rounds, n), 0)

    def body(k, carry):
        x, rank, vals = carry
        best = jnp.max(x, axis=0, keepdims=True)
        hit = x == best
        if break_ties:
            hit = rid == jnp.min(jnp.where(hit, rid, rows), axis=0, keepdims=True)
        rank = jnp.where(hit, k, rank)
        x = jnp.where(hit, -jnp.inf, x)
        vals = jnp.where(kid == k, best, vals)
        return x, rank, vals

    init = (x, jnp.full((rows, n), rounds, jnp.int32), jnp.zeros((rounds, n), F32))
    _, rank, vals = lax.fori_loop(0, rounds, body, init)
    return vals, rank


def _peer_score_kernel(q_ref, keys_ref, cb_o, eb_o, na_o, wa_o):
    nk, topk = PEER_NKEYS, PEER_TOPK
    s = lax.dot_general(keys_ref[...], q_ref[...], _NT, precision=lax.Precision.HIGHEST,
                        preferred_element_type=F32)

    def taken_off(rank):
        taken = jnp.sum((rank < topk).astype(jnp.int32), axis=0, keepdims=True)
        return jnp.max(jnp.where(taken != topk, 1, 0))

    suspect = _peer_select(s, cb_o, eb_o, na_o, wa_o, False, taken_off)

    @pl.when(suspect > 0)
    def _():
        _peer_select(s, cb_o, eb_o, na_o, wa_o, True, taken_off)


def _peer_select(s, cb_o, eb_o, na_o, wa_o, break_ties, taken_off):
    nk, topk = PEER_NKEYS, PEER_TOPK
    sa, sb = s[:nk], s[nk:]
    va, rank_a = _extract_top(sa, topk, break_ties)
    vb, rank_b = _extract_top(sb, topk, break_ties)
    n = s.shape[1]
    row8 = lax.broadcasted_iota(jnp.int32, (8, n), 0)
    ninf = -jnp.inf
    a = [va[r:r + 1] for r in range(topk)]
    groups = [a[0] + vb]
    for r, lim in ((1, 8), (2, 5), (3, 4), (4, 3)):
        groups.append(jnp.where(row8 < lim, a[r] + vb[0:8], ninf))
    a_mix = jnp.where(row8 < 2, a[5], jnp.where(row8 < 4, a[6], jnp.where(
        row8 < 6, a[7], jnp.where(row8 == 6, a[8], a[9]))))
    b_mix = jnp.where((row8 < 6) & ((row8 & 1) == 1), vb[1:2], vb[0:1])
    groups.append(a_mix + b_mix)
    tail = jnp.full((8, n), ninf, F32)
    for k in range(6):
        tail = jnp.where(row8 == k, a[10 + k] + vb[0:1], tail)
    groups.append(tail)
    cand = jnp.concatenate(groups, axis=0)
    top_s, rank_c = _extract_top(cand, topk, break_ties)
    taken = (rank_c < topk).astype(jnp.int32)
    counts = [jnp.sum(taken[0:16], axis=0, keepdims=True)]
    counts += [jnp.sum(taken[8 + 8 * r:16 + 8 * r], axis=0, keepdims=True) for r in range(1, 5)]
    counts += [taken[48 + 2 * k:49 + 2 * k] + taken[49 + 2 * k:50 + 2 * k] for k in range(3)]
    counts += [taken[54 + k:55 + k] for k in range(8)]
    count = jnp.concatenate(counts, axis=0)
    z = jnp.sum(jnp.exp(top_s - top_s[0:1]), axis=0, keepdims=True)
    na = jnp.zeros(sa.shape, jnp.int32)
    for r in range(topk):
        na = jnp.where(rank_a == r, count[r:r + 1], na)
    cb_o[...] = rank_b.astype(F32)
    eb_o[...] = jnp.exp(sb - vb[0:1])
    na_o[...] = na.astype(F32)
    wa_o[...] = jnp.exp(sa - va[0:1]) / z
    return taken_off(rank_a) + taken_off(rank_b) + taken_off(rank_c)


def peer_scores(q, key_a, key_b, tn=256):
    t = q.shape[0]
    tn = min(tn, t)
    zeros = jnp.zeros_like(key_a)
    keys = jnp.concatenate([jnp.concatenate([key_a, zeros], axis=-1),
                            jnp.concatenate([zeros, key_b], axis=-1)], axis=1)
    out = jax.ShapeDtypeStruct((PEER_HEADS, PEER_NKEYS, t), F32)
    ospec = pl.BlockSpec((None, PEER_NKEYS, tn), lambda i, h: (h, 0, i))
    return pl.pallas_call(
        _peer_score_kernel,
        grid=(t // tn, PEER_HEADS),
        in_specs=[pl.BlockSpec((tn, LANES), lambda i, h: (i, h)),
                  pl.BlockSpec((None, 2 * PEER_NKEYS, LANES), lambda i, h: (h, 0, 0))],
        out_specs=[ospec] * 4,
        out_shape=[out] * 4,
        compiler_params=_params("parallel", "parallel"),
        name="peer_scores",
    )(q, keys)


def _peer_dense_kernel(xt_ref, u_ref, vt_ref, cb_ref, eb_ref, na_ref, wa_ref, o_ref,
                       coef_even, coef_odd, *, te):
    e = pl.program_id(1)
    last = pl.num_programs(1) - 2
    tn = o_ref.shape[1]
    half = tn // 2

    @pl.when(e == 0)
    def _():
        o_ref[...] = jnp.zeros_like(o_ref)
        coef_odd[...] = jnp.zeros_like(coef_odd)

    def gate_into_coef(hid, lane0, coef_out):
        chunk = 16
        for di in range(te // PEER_NKEYS):
            i = jnp.minimum(e, last) * (te // PEER_NKEYS) + di
            for sub in range(0, hid.shape[1], LANES):
                lanes = slice(lane0 + sub, lane0 + sub + LANES)
                n_rows = [na_ref[i, h:h + 1, lanes] for h in range(PEER_HEADS)]
                w_rows = [wa_ref[i, h:h + 1, lanes] for h in range(PEER_HEADS)]
                for r in range(0, PEER_NKEYS, chunk):
                    rs = slice(r, r + chunk)
                    w = None
                    for h in range(PEER_HEADS):
                        term = jnp.where(cb_ref[h, rs, lanes] < n_rows[h],
                                         eb_ref[h, rs, lanes] * w_rows[h], 0.0)
                        w = term if w is None else w + term
                    row0 = di * PEER_NKEYS + r
                    x = hid[row0:row0 + chunk, sub:sub + LANES]
                    act = 0.5 * x * (1.0 + lax.erf(x * (2.0 ** -0.5)))
                    coef_out[row0:row0 + chunk, lanes] = (act * w).astype(BF16)

    def step(coef_out, coef_in):
        lo, hi = slice(0, half), slice(half, tn)
        hid_lo = jnp.dot(u_ref[...], xt_ref[:, lo], preferred_element_type=F32)
        o_ref[:, lo] += jnp.dot(vt_ref[...], coef_in[:, lo], preferred_element_type=F32)
        hid_hi = jnp.dot(u_ref[...], xt_ref[:, hi], preferred_element_type=F32)
        gate_into_coef(hid_lo, 0, coef_out)
        o_ref[:, hi] += jnp.dot(vt_ref[...], coef_in[:, hi], preferred_element_type=F32)
        gate_into_coef(hid_hi, half, coef_out)

    @pl.when((e & 1) == 0)
    def _():
        step(coef_even, coef_odd)

    @pl.when((e & 1) == 1)
    def _():
        step(coef_odd, coef_even)


def peer_dense(xt, u, vt, cb, eb, na, wa, tn=512, te=PEER_TILE):
    d, t = xt.shape
    n_exp = u.shape[0]
    tn = min(tn, t)
    n_e = n_exp // te
    assert vt.shape == (n_e, d, te)
    once = pl.Buffered(1)
    sspec = pl.BlockSpec((PEER_HEADS, PEER_NKEYS, tn), lambda i, e: (0, 0, i), pipeline_mode=once)
    aspec = pl.BlockSpec((PEER_NKEYS, PEER_HEADS, tn), lambda i, e: (0, 0, i), pipeline_mode=once)
    na, wa = na.transpose(1, 0, 2), wa.transpose(1, 0, 2)
    return pl.pallas_call(
        functools.partial(_peer_dense_kernel, te=te),
        grid=(t // tn, n_e + 1),
        in_specs=[pl.BlockSpec((d, tn), lambda i, e: (0, i), pipeline_mode=once),
                  pl.BlockSpec((te, d), lambda i, e: (jnp.minimum(e, n_e - 1), 0)),
                  pl.BlockSpec((None, d, te), lambda i, e: (jnp.maximum(e - 1, 0), 0, 0)),
                  sspec, sspec, aspec, aspec],
        out_specs=pl.BlockSpec((d, tn), lambda i, e: (0, i)),
        out_shape=jax.ShapeDtypeStruct((d, t), F32),
        scratch_shapes=[pltpu.VMEM((te, tn), BF16), pltpu.VMEM((te, tn), BF16)],
        compiler_params=_params("parallel", "arbitrary"),
        name="peer_dense",
    )(xt, u, vt, cb, eb, na, wa)


def kernel(x, p, positions, norm_mix, w_in, qn_moba, kn_moba, qn_diff, kn_diff, lambda_q1, lambda_k1, lambda_q2, lambda_k2, subln_diff, w_branch, w_out, norm_ffn, peer_wq, peer_key_a, peer_key_b, peer_u, peer_v, norm_ple, w_ple_gate, w_ple_proj):
    b, s, d = x.shape
    t = b * s
    depth = w_in.shape[0]
    nb = s // MOBA_BLOCK
    gate_col = 9 * MIX_WIDTH
    h = x.reshape(t, d)
    p_flat = p.reshape(depth, t, -1)
    for i in range(depth):
        n1 = rmsnorm(h, norm_mix[i])
        proj = matmul(n1, w_in, i, out_dtype=BF16)
        qm, qmf, km, qd, kd, kmean = prep_qkv(
            proj, positions, qn_moba[i], kn_moba[i], qn_diff[i], kn_diff[i])
        kmean = kmean.reshape(b, nb, N_HEADS, HEAD_DIM).transpose(0, 2, 1, 3)
        kmean = jnp.pad(kmean, ((0, 0), (0, 0), (0, -nb % 8), (0, 0)))
        o_m = moba_attention(qm, qmf, kmean, km, proj, 2, b, s)
        lam_init = 0.8 - 0.6 * math.exp(-0.3 * i)
        o_d = diff_attention(qd, kd, proj, 5, lambda_q1[i], lambda_k1[i], lambda_q2[i], lambda_k2[i],
                             subln_diff[i], lam_init, b, s)
        o_s = stick_attention(proj, 6, b, s)
        merged = merge_branches(o_m, o_d, o_s, w_branch, i, proj, gate_col)
        h = matmul(merged, w_out, i, mode="residual", res=h)
        n2, n2_t = rmsnorm(h, norm_ffn[i], with_transpose=True)
        q = matmul(n2, peer_wq, i)
        cb, eb, na, wa = peer_scores(q, peer_key_a[i], peer_key_b[i])
        ffn_t = peer_dense(n2_t, to_bf16(peer_u, i), to_bf16(peer_v, i, transpose_tile=PEER_TILE),
                           cb, eb, na, wa, te=PEER_TILE)
        h, n3 = residual_rmsnorm(h, ffn_t, norm_ple[i])
        h = matmul(n3, w_ple_gate, i, mode="ple", res=h, p=p_flat, wp=w_ple_proj)
    return h.reshape(b, s, d)
```

```python
import functools
import math

import jax
import jax.numpy as jnp
from jax import lax
from jax.experimental import pallas as pl
from jax.experimental.pallas import tpu as pltpu

F32 = jnp.float32
BF16 = jnp.bfloat16

LANES = 128
HEAD_DIM = 128
N_HEADS = 8
MIX_WIDTH = N_HEADS * HEAD_DIM
DIFF_SUB = HEAD_DIM // 2
ROPE_THETA = 500000.0
ROT_HALF = HEAD_DIM // 8
ROT_HALF_DIFF = DIFF_SUB // 8
MOBA_BLOCK = 256
MOBA_TOPK = 3
PEER_HEADS = 8
PEER_NKEYS = 128
PEER_TOPK = 16
PEER_TILE = 512
NORM_EPS = 1e-6
NEG_INF = -1e30
V7X_VMEM_BYTES = 64 * 1024 * 1024
VMEM_LIMIT = V7X_VMEM_BYTES * 7 // 8

_NT = (((1,), (1,)), ((), ()))


def _params(*sem):
    return pltpu.CompilerParams(dimension_semantics=sem, vmem_limit_bytes=VMEM_LIMIT)


def _rmsnorm_kernel(x_ref, g_ref, o_ref, *maybe_ot_ref):
    x = x_ref[...]
    ms = jnp.mean(x * x, axis=-1, keepdims=True)
    y = x * lax.rsqrt(ms + NORM_EPS) * g_ref[...]
    o_ref[...] = y.astype(o_ref.dtype)
    for ot_ref in maybe_ot_ref:
        ot_ref[...] = y.T.astype(ot_ref.dtype)


def rmsnorm(x, gain, tm=256, with_transpose=False):
    t, d = x.shape
    tm = min(tm, t)
    out_specs = [pl.BlockSpec((tm, d), lambda i: (i, 0))]
    out_shape = [jax.ShapeDtypeStruct((t, d), BF16)]
    if with_transpose:
        out_specs.append(pl.BlockSpec((d, tm), lambda i: (0, i)))
        out_shape.append(jax.ShapeDtypeStruct((d, t), BF16))
    outs = pl.pallas_call(
        _rmsnorm_kernel,
        grid=(t // tm,),
        in_specs=[pl.BlockSpec((tm, d), lambda i: (i, 0)),
                  pl.BlockSpec((1, d), lambda i: (0, 0))],
        out_specs=out_specs,
        out_shape=out_shape,
        compiler_params=_params("parallel"),
        name="rmsnorm_t" if with_transpose else "rmsnorm",
    )(x, gain.reshape(1, d))
    return outs if with_transpose else outs[0]


def _residual_rmsnorm_kernel(x_ref, yt_ref, g_ref, h_ref, n_ref):
    h = x_ref[...] + yt_ref[...].T
    h_ref[...] = h
    ms = jnp.mean(h * h, axis=-1, keepdims=True)
    n_ref[...] = (h * lax.rsqrt(ms + NORM_EPS) * g_ref[...]).astype(n_ref.dtype)


def residual_rmsnorm(x, y_t, gain, tm=256):
    t, d = x.shape
    tm = min(tm, t)
    row = pl.BlockSpec((tm, d), lambda i: (i, 0))
    return pl.pallas_call(
        _residual_rmsnorm_kernel,
        grid=(t // tm,),
        in_specs=[row, pl.BlockSpec((d, tm), lambda i: (0, i)), pl.BlockSpec((1, d), lambda i: (0, 0))],
        out_specs=[row, row],
        out_shape=[jax.ShapeDtypeStruct((t, d), F32), jax.ShapeDtypeStruct((t, d), BF16)],
        compiler_params=_params("parallel"),
        name="residual_rmsnorm",
    )(x, y_t, gain.reshape(1, d))


def _relayout_kernel(x_ref, o_ref, *, transpose_tile):
    x = x_ref[...]
    if transpose_tile is None:
        o_ref[...] = x.astype(o_ref.dtype)
    else:
        xt = x.T.astype(o_ref.dtype)
        for a in range(o_ref.shape[0]):
            o_ref[a] = xt[:, a * transpose_tile:(a + 1) * transpose_tile]


def to_bf16(x, layer, transpose_tile=None, tb=1024):
    _, r, c = x.shape
    tr, tc = min(tb, r), min(tb, c)
    if transpose_tile is None:
        out_spec, out_dims = pl.BlockSpec((tr, tc), lambda i, j: (i, j)), (r, c)
    else:
        n = min(transpose_tile, tr)
        transpose_tile = n
        out_spec = pl.BlockSpec((tr // n, tc, n), lambda i, j: (i, j, 0))
        out_dims = (r // n, c, n)
    return pl.pallas_call(
        functools.partial(_relayout_kernel, transpose_tile=transpose_tile),
        grid=(r // tr, c // tc),
        in_specs=[pl.BlockSpec((None, tr, tc), lambda i, j: (layer, i, j))],
        out_specs=out_spec,
        out_shape=jax.ShapeDtypeStruct(out_dims, BF16),
        compiler_params=_params("parallel", "parallel"),
        name="to_bf16" if transpose_tile is None else "to_bf16_t",
    )(x)


def _mm_kernel(*refs, mode):
    if mode == "plain":
        a_ref, b_ref, o_ref = refs
    elif mode == "residual":
        a_ref, b_ref, r_ref, o_ref = refs
    else:
        a_ref, b_ref, r_ref, p_ref, wp_ref, o_ref = refs
    acc = jnp.dot(a_ref[...], b_ref[...].astype(BF16), preferred_element_type=F32)
    if mode == "plain":
        o_ref[...] = acc.astype(o_ref.dtype)
    elif mode == "residual":
        o_ref[...] = (r_ref[...] + acc).astype(o_ref.dtype)
    else:
        emb = jnp.dot(p_ref[...].astype(BF16), wp_ref[...].astype(BF16),
                      preferred_element_type=F32)
        o_ref[...] = (r_ref[...] + jax.nn.sigmoid(acc) * emb).astype(o_ref.dtype)


def matmul(a, w, layer, *, mode="plain", res=None, p=None, wp=None, out_dtype=F32, tm=2048, tn=None):
    m, kdim = a.shape
    n = w.shape[2]
    if tn is None:
        tn = 256 if mode == "ple" else 512
    tm, tn = min(tm, m), min(tn, n)
    in_specs = [pl.BlockSpec((tm, kdim), lambda i, j: (i, 0), pipeline_mode=pl.Buffered(1)),
                pl.BlockSpec((None, kdim, tn), lambda i, j: (layer, 0, j))]
    args = [a, w]
    if mode in ("residual", "ple"):
        in_specs.append(pl.BlockSpec((tm, tn), lambda i, j: (i, j)))
        args.append(res)
    if mode == "ple":
        pd = p.shape[2]
        in_specs += [pl.BlockSpec((None, tm, pd), lambda i, j: (layer, i, 0)),
                     pl.BlockSpec((None, pd, tn), lambda i, j: (layer, 0, j))]
        args += [p, wp]
    return pl.pallas_call(
        functools.partial(_mm_kernel, mode=mode),
        grid=(m // tm, n // tn),
        in_specs=in_specs,
        out_specs=pl.BlockSpec((tm, tn), lambda i, j: (i, j)),
        out_shape=jax.ShapeDtypeStruct((m, n), out_dtype),
        compiler_params=_params("parallel", "arbitrary"),
        name="matmul_" + mode,
    )(*args)


def _rope(x, cos, sin, half, period):
    lane = lax.broadcasted_iota(jnp.int32, x.shape, 1)
    first = (lane & (period - 1)) < half
    partner = jnp.where(first, pltpu.roll(x, LANES - half, 1), pltpu.roll(x, half, 1))
    return x * cos + partner * jnp.where(first, -sin, sin)


def _prep_kernel(pos_ref, fm_ref, fd_ref, qng_ref, kng_ref, qdg_ref, kdg_ref,
                 qm_i, km_i, qd_i, kd_i, qm_o, qmf_o, km_o, qd_o, kd_o, kmean_o):
    pos = pos_ref[...].astype(F32)
    ang_m = pos * fm_ref[...]
    ang_d = pos * fd_ref[...]
    cos_m, sin_m = jnp.cos(ang_m), jnp.sin(ang_m)
    cos_d, sin_d = jnp.cos(ang_d), jnp.sin(ang_d)
    lane = lax.broadcasted_iota(jnp.int32, ang_m.shape, 1)
    low = lane < DIFF_SUB

    def norm_full(x, g):
        ms = jnp.mean(x * x, axis=-1, keepdims=True)
        return x * lax.rsqrt(ms + NORM_EPS) * g

    def norm_halves(x, g):
        xx = x * x
        lo = jnp.sum(jnp.where(low, xx, 0.0), axis=-1, keepdims=True)
        hi = jnp.sum(jnp.where(low, 0.0, xx), axis=-1, keepdims=True)
        ms = jnp.where(low, lo, hi) * (1.0 / DIFF_SUB)
        return x * lax.rsqrt(ms + NORM_EPS) * g

    for h in range(N_HEADS):
        sl = slice(h * HEAD_DIM, (h + 1) * HEAD_DIM)
        qm = _rope(norm_full(qm_i[:, sl].astype(F32), qng_ref[...]), cos_m, sin_m, ROT_HALF, HEAD_DIM)
        km = _rope(norm_full(km_i[:, sl].astype(F32), kng_ref[...]), cos_m, sin_m, ROT_HALF, HEAD_DIM)
        qmf_o[:, sl] = qm
        qm_o[:, sl] = qm.astype(BF16)
        km_o[:, sl] = km.astype(BF16)
        kmean_o[0, h:h + 1, :] = jnp.mean(km, axis=0, keepdims=True)
        qd = _rope(norm_halves(qd_i[:, sl].astype(F32), qdg_ref[...]), cos_d, sin_d,
                   ROT_HALF_DIFF, DIFF_SUB)
        kd = _rope(norm_halves(kd_i[:, sl].astype(F32), kdg_ref[...]), cos_d, sin_d,
                   ROT_HALF_DIFF, DIFF_SUB)
        qd_o[:, sl] = qd.astype(BF16)
        kd_o[:, sl] = kd.astype(BF16)


def _rope_table(half, period):
    lane = jnp.arange(LANES)
    inv_freq = ROPE_THETA ** (-jnp.arange(half, dtype=F32) / half)
    tab = jnp.where((lane % period) < 2 * half, inv_freq[lane % half], 0.0)
    return tab.reshape(1, LANES).astype(F32)


def prep_qkv(proj, positions, qn_m, kn_m, qn_d, kn_d):
    t = proj.shape[0]
    tm = MOBA_BLOCK
    w = MIX_WIDTH
    row = lambda i: (i, 0)
    col_spec = [pl.BlockSpec((tm, w), functools.partial(lambda i, c: (i, c), c=c)) for c in (0, 1, 3, 4)]
    vec = pl.BlockSpec((1, LANES), lambda i: (0, 0))
    bf = jax.ShapeDtypeStruct((t, w), BF16)
    out_shape = [bf, jax.ShapeDtypeStruct((t, w), F32), bf, bf, bf,
                 jax.ShapeDtypeStruct((t // tm, N_HEADS, HEAD_DIM), F32)]
    out_specs = [pl.BlockSpec((tm, w), row)] * 5 + [
        pl.BlockSpec((1, N_HEADS, HEAD_DIM), lambda i: (i, 0, 0))]
    return pl.pallas_call(
        _prep_kernel,
        grid=(t // tm,),
        in_specs=[pl.BlockSpec((tm, 1), row), vec, vec, vec, vec, vec, vec] + col_spec,
        out_specs=out_specs,
        out_shape=out_shape,
        compiler_params=_params("parallel"),
        name="prep_qkv",
    )(positions.reshape(t, 1), _rope_table(ROT_HALF, HEAD_DIM), _rope_table(ROT_HALF_DIFF, DIFF_SUB),
      qn_m.reshape(1, LANES), kn_m.reshape(1, LANES),
      jnp.tile(qn_d, 2).reshape(1, LANES), jnp.tile(kn_d, 2).reshape(1, LANES),
      *([proj] * 4))


def _ones_column(rows, at=0):
    return (lax.broadcasted_iota(jnp.int32, (rows, LANES), 1) == at).astype(BF16)


def _value_with_ones(v):
    return jnp.concatenate([v, jnp.ones(v.shape, BF16)], axis=1)


def _moba_kernel(q_ref, qf_ref, kmean_ref, k_ref, v_ref, o_ref, knorm_ref, *, tk):
    blk = MOBA_BLOCK
    tq = q_ref.shape[0]
    qi = pl.program_id(2)
    scale = HEAD_DIM ** -0.5
    big = NEG_INF / scale
    nbp = kmean_ref.shape[0]

    @pl.when(qi == 0)
    def _():
        knorm_ref[...] = jnp.broadcast_to(_max_row_norm(k_ref, 0, HEAD_DIM), knorm_ref.shape)

    gate = lax.dot_general(kmean_ref[...], qf_ref[...], _NT, precision=lax.Precision.HIGHEST,
                           preferred_element_type=F32)
    rid = lax.broadcasted_iota(jnp.int32, gate.shape, 0)
    own = qi * (tq // blk) + lax.broadcasted_iota(jnp.int32, gate.shape, 1) // blk
    rank = jnp.zeros(gate.shape, jnp.int32)
    for m in range(nbp):
        g_m = gate[m:m + 1, :]
        beats = (g_m > gate) | ((g_m == gate) & (rid > m))
        rank = rank + jnp.where(beats & (m < own), 1, 0)
    allowed = ((rid < own) & (rank < MOBA_TOPK)) | (rid == own)
    pen_t = jnp.where(allowed, 0.0, big)
    pen_t = jnp.concatenate([pen_t, jnp.full((LANES - nbp, tq), big, F32)], axis=0)
    penalty = pen_t.T

    q = q_ref[...]
    lane = lax.broadcasted_iota(jnp.int32, (tq, LANES), 1)
    q_pos = qi * tq + lax.broadcasted_iota(jnp.int32, (tq, tk), 0)
    col = lax.broadcasted_iota(jnp.int32, (tq, tk), 1)
    krow = lax.broadcasted_iota(jnp.int32, (tk, LANES), 0)
    klane = lax.broadcasted_iota(jnp.int32, (tk, LANES), 1)
    n_tiles = (qi * tq) // tk + 1

    def scores(n, q_aug, shift_lane, causal):
        off = pl.multiple_of(n * tk, tk)
        kb = k_ref[pl.ds(off, tk), :]
        block_of_row = n * (tk // blk) + krow // blk
        marks = ((klane == block_of_row) | (klane == shift_lane)).astype(BF16)
        s = lax.dot_general(q_aug, jnp.concatenate([kb, marks], axis=1), _NT,
                            preferred_element_type=F32) * scale
        if causal:
            s = jnp.where(off + col <= q_pos, s, NEG_INF)
        return s, off

    q_max = jnp.concatenate([q, penalty.astype(BF16)], axis=1)

    def sweep_max(n, best):
        s, _ = scores(n, q_max, -1, True)
        return jnp.maximum(best, s)

    def exact_row_max():
        best = lax.fori_loop(0, n_tiles, sweep_max, jnp.full((tq, tk), NEG_INF, F32))
        return jnp.max(best, axis=-1, keepdims=True)

    qsq = q.astype(F32)
    bound = jnp.sqrt(jnp.sum(qsq * qsq, axis=-1, keepdims=True)) * knorm_ref[0:1, 0:1] * scale
    shift = lax.cond(2.0 * jnp.max(bound) < SHIFT_SLACK, lambda: bound, exact_row_max)
    shift_lane = LANES - 1
    q_sum = jnp.concatenate(
        [q, jnp.where(lane == shift_lane, -shift / scale, penalty).astype(BF16)], axis=1)

    def tile_sum(n, causal):
        s, off = scores(n, q_sum, shift_lane, causal)
        vb = v_ref[pl.ds(off, tk), :]
        return jnp.dot(jnp.exp(s).astype(BF16), _value_with_ones(vb), preferred_element_type=F32)

    acc = lax.fori_loop(0, n_tiles - 1, lambda n, acc: acc + tile_sum(n, False),
                        jnp.zeros((tq, 2 * HEAD_DIM), F32))
    acc = acc + tile_sum(n_tiles - 1, True)
    o_ref[...] = (acc[:, :HEAD_DIM] / acc[:, HEAD_DIM:]).astype(o_ref.dtype)


def moba_attention(q, qf, kmean, k, v, v_group, batch, seq, tq=512, tk=512):
    t = q.shape[0]
    tq, tk = min(tq, seq), min(tk, seq)
    assert tq % MOBA_BLOCK == 0 and tk % tq == 0
    nq = seq // tq
    qspec = pl.BlockSpec((tq, HEAD_DIM), lambda b, h, i: (b * nq + i, h))
    kspec = pl.BlockSpec((seq, HEAD_DIM), lambda b, h, i: (b, h))
    vspec = pl.BlockSpec((seq, HEAD_DIM), lambda b, h, i: (b, v_group * N_HEADS + h))
    return pl.pallas_call(
        functools.partial(_moba_kernel, tk=tk),
        grid=(batch, N_HEADS, nq),
        in_specs=[qspec, qspec,
                  pl.BlockSpec((None, None, kmean.shape[2], HEAD_DIM), lambda b, h, i: (b, h, 0, 0)),
                  kspec, vspec],
        out_specs=qspec,
        out_shape=jax.ShapeDtypeStruct((t, MIX_WIDTH), BF16),
        scratch_shapes=[pltpu.VMEM((8, LANES), F32)],
        compiler_params=_params("parallel", "parallel", "arbitrary"),
        name="moba_attention",
    )(q, qf, kmean, k, v)


SHIFT_SLACK = 80.0


def _max_row_norm(x_ref, lane_lo, lane_hi, chunk=512):
    n_rows = x_ref.shape[0]
    chunk = min(chunk, n_rows)
    lane = lax.broadcasted_iota(jnp.int32, (chunk, x_ref.shape[1]), 1)
    keep = (lane >= lane_lo) & (lane < lane_hi)

    def body(c, best):
        x = x_ref[pl.ds(pl.multiple_of(c * chunk, chunk), chunk), :].astype(F32)
        sq = jnp.sum(jnp.where(keep, x * x, 0.0), axis=-1, keepdims=True)
        return jnp.maximum(best, jnp.max(sq, axis=0, keepdims=True))

    return jnp.sqrt(lax.fori_loop(0, n_rows // chunk, body, jnp.zeros((1, 1), F32)))


def _diff_kernel(lq1_ref, lk1_ref, lq2_ref, lk2_ref, g_ref, q_ref, k_ref, v_ref, o_ref, knorm_ref,
                 *, tq, tk, lam_init):
    qi = pl.program_id(2)

    @pl.when(qi == 0)
    def _():
        knorm_ref[0] = jnp.broadcast_to(_max_row_norm(k_ref, 0, DIFF_SUB), knorm_ref.shape[1:])
        knorm_ref[1] = jnp.broadcast_to(_max_row_norm(k_ref, DIFF_SUB, HEAD_DIM), knorm_ref.shape[1:])

    q = q_ref[...] * (DIFF_SUB ** -0.5)
    lane = lax.broadcasted_iota(jnp.int32, q.shape, 1)
    zero = jnp.zeros_like(q)
    qq = jnp.concatenate([jnp.where(lane < DIFF_SUB, q, zero),
                          jnp.where(lane < DIFF_SUB, zero, q)], axis=0)
    rows = 2 * tq
    q_pos = qi * tq + (lax.broadcasted_iota(jnp.int32, (rows, tk), 0) & (tq - 1))
    col = lax.broadcasted_iota(jnp.int32, (rows, tk), 1)
    n_full = (qi * tq) // tk
    last = pl.multiple_of(n_full * tk, tk)
    causal = last + col <= q_pos

    def lane_tile_max(s):
        out = s[:, :LANES]
        for c in range(LANES, tk, LANES):
            out = jnp.maximum(out, s[:, c:c + LANES])
        return out

    def sweep_max(n, best):
        kb = k_ref[pl.ds(pl.multiple_of(n * tk, tk), tk), :]
        return jnp.maximum(best, lane_tile_max(lax.dot_general(qq, kb, _NT, preferred_element_type=F32)))

    def exact_row_max():
        best = lax.fori_loop(0, n_full, sweep_max, jnp.full((rows, LANES), NEG_INF, F32))
        s_last = lax.dot_general(qq, k_ref[pl.ds(last, tk), :], _NT, preferred_element_type=F32)
        best = jnp.maximum(best, lane_tile_max(jnp.where(causal, s_last, NEG_INF)))
        return jnp.max(best, axis=-1, keepdims=True)

    qf = qq.astype(F32)
    q_norm = jnp.sqrt(jnp.sum(qf * qf, axis=-1, keepdims=True))
    first_half = lax.broadcasted_iota(jnp.int32, (rows, 1), 0) < tq
    bound = q_norm * jnp.where(first_half, knorm_ref[0][0:1, 0:1], knorm_ref[1][0:1, 0:1])
    shift = lax.cond(2.0 * jnp.max(bound) < SHIFT_SLACK, lambda: bound, exact_row_max)

    lane_r = lax.broadcasted_iota(jnp.int32, (rows, LANES), 1)
    q_aug = jnp.concatenate([qq, jnp.where(lane_r == 0, -shift, 0.0).astype(BF16)], axis=1)
    ones_col = _ones_column(tk)

    def tile_sum(off, mask):
        k_aug = jnp.concatenate([k_ref[pl.ds(off, tk), :], ones_col], axis=1)
        v_aug = _value_with_ones(v_ref[pl.ds(off, tk), :])
        outs = []
        for half in range(2):
            s = lax.dot_general(q_aug[half * tq:(half + 1) * tq], k_aug, _NT,
                                preferred_element_type=F32)
            if mask is not None:
                s = jnp.where(mask[half * tq:(half + 1) * tq], s, NEG_INF)
            outs.append(jnp.dot(jnp.exp(s).astype(BF16), v_aug, preferred_element_type=F32))
        return jnp.concatenate(outs, axis=0)

    def sweep_sum(n, acc):
        return acc + tile_sum(pl.multiple_of(n * tk, tk), None)

    acc = lax.fori_loop(0, n_full, sweep_sum, jnp.zeros((rows, 2 * HEAD_DIM), F32))
    acc = acc + tile_sum(last, causal)

    lam = (jnp.exp(jnp.sum(lq1_ref[...] * lk1_ref[...], axis=-1, keepdims=True))
           - jnp.exp(jnp.sum(lq2_ref[...] * lk2_ref[...], axis=-1, keepdims=True)) + lam_init)
    out = acc[:, :HEAD_DIM] / acc[:, HEAD_DIM:]
    o = out[:tq] - lam * out[tq:]
    ms = jnp.mean(o * o, axis=-1, keepdims=True)
    o_ref[...] = (o * lax.rsqrt(ms + NORM_EPS) * g_ref[...] * (1.0 - lam_init)).astype(o_ref.dtype)


def diff_attention(q, k, v, v_group, lq1, lk1, lq2, lk2, subln, lam_init, batch, seq, tq=512, tk=512):
    t = q.shape[0]
    tq, tk = min(tq, seq), min(tk, seq)
    nq = seq // tq
    qspec = pl.BlockSpec((tq, HEAD_DIM), lambda b, h, i: (b * nq + i, h))
    kspec = pl.BlockSpec((seq, HEAD_DIM), lambda b, h, i: (b, h))
    vspec = pl.BlockSpec((seq, HEAD_DIM), lambda b, h, i: (b, v_group * N_HEADS + h))
    lspec = pl.BlockSpec((1, DIFF_SUB), lambda b, h, i: (0, 0))
    return pl.pallas_call(
        functools.partial(_diff_kernel, tq=tq, tk=tk, lam_init=lam_init),
        grid=(batch, N_HEADS, nq),
        in_specs=[lspec, lspec, lspec, lspec,
                  pl.BlockSpec((1, HEAD_DIM), lambda b, h, i: (0, 0)), qspec, kspec, vspec],
        out_specs=qspec,
        out_shape=jax.ShapeDtypeStruct((t, MIX_WIDTH), BF16),
        scratch_shapes=[pltpu.VMEM((2, 8, LANES), F32)],
        compiler_params=_params("parallel", "parallel", "arbitrary"),
        name="diff_attention",
    )(lq1.reshape(1, DIFF_SUB), lk1.reshape(1, DIFF_SUB), lq2.reshape(1, DIFF_SUB),
      lk2.reshape(1, DIFF_SUB), subln.reshape(1, HEAD_DIM), q, k, v)


def _stick_kernel(q_ref, k_ref, v_ref, o_ref, *, tq, tk):
    qi = pl.program_id(2)
    scale = HEAD_DIM ** -0.5
    n_heads = q_ref.shape[1] // HEAD_DIM
    upper = (lax.broadcasted_iota(jnp.int32, (tk, tk), 0)
             > lax.broadcasted_iota(jnp.int32, (tk, tk), 1)).astype(BF16)
    q_pos = qi * tq + lax.broadcasted_iota(jnp.int32, (tq, tk), 0)
    col = lax.broadcasted_iota(jnp.int32, (tq, tk), 1)
    n_diag = tq // tk
    n_past = qi * n_diag

    def tile(n, hd, later, acc, masked):
        off = pl.multiple_of(n * tk, tk)
        cols = slice(hd * HEAD_DIM, (hd + 1) * HEAD_DIM)
        kb = k_ref[pl.ds(off, tk), cols]
        vb = v_ref[pl.ds(off, tk), cols]
        z = lax.dot_general(q_ref[:, cols], kb, _NT, preferred_element_type=F32) * scale
        sp = jnp.maximum(z, 0.0) + jnp.log(1.0 + jnp.exp(-jnp.abs(z)))
        if masked:
            strict = (off + col) < q_pos
            sp = jnp.where(strict, sp, 0.0)
        hi = sp.astype(BF16)
        lo = (sp - hi.astype(F32)).astype(BF16)
        between = (jnp.dot(hi, upper, preferred_element_type=F32)
                   + jnp.dot(lo, upper, preferred_element_type=F32))
        a = jnp.exp(z - sp - between - later)
        if masked:
            a = jnp.where(strict, a, 0.0)
        acc = acc + jnp.dot(a.astype(BF16), vb, preferred_element_type=F32)
        return later + jnp.sum(sp, axis=-1, keepdims=True), acc

    state = [(jnp.zeros((tq, 1), F32), jnp.zeros((tq, HEAD_DIM), F32)) for _ in range(n_heads)]
    for d in range(n_diag):
        state = [tile(n_past + n_diag - 1 - d, hd, *state[hd], True) for hd in range(n_heads)]

    def body(it, carry):
        return tuple(tile(n_past - 1 - it, hd, *carry[hd], False) for hd in range(n_heads))

    state = lax.fori_loop(0, n_past, body, tuple(state))
    for hd in range(n_heads):
        o_ref[:, hd * HEAD_DIM:(hd + 1) * HEAD_DIM] = state[hd][1].astype(o_ref.dtype)


def stick_attention(qkv, q_group, batch, seq, tq=1024, tk=256, heads_per_step=2):
    t = qkv.shape[0]
    tq, tk = min(tq, seq), min(tk, seq)
    nq = seq // tq
    width = heads_per_step * HEAD_DIM
    per_group = MIX_WIDTH // width

    def col(group):
        return lambda b, h, i: (b, group * per_group + h)

    qspec = pl.BlockSpec((tq, width), lambda b, h, i: (b * nq + i, q_group * per_group + h))
    kspec = pl.BlockSpec((seq, width), col(q_group + 1))
    vspec = pl.BlockSpec((seq, width), col(q_group + 2))
    return pl.pallas_call(
        functools.partial(_stick_kernel, tq=tq, tk=tk),
        grid=(batch, N_HEADS // heads_per_step, nq),
        in_specs=[qspec, kspec, vspec],
        out_specs=pl.BlockSpec((tq, width), lambda b, h, i: (b * nq + i, h)),
        out_shape=jax.ShapeDtypeStruct((t, MIX_WIDTH), BF16),
        compiler_params=_params("parallel", "parallel", "arbitrary"),
        name="stick_attention",
    )(qkv, qkv, qkv)


def _merge_kernel(om_ref, od_ref, os_ref, w_ref, gm_ref, gd_ref, gs_ref, o_ref):
    acc = None
    for k, (g_ref, x_ref) in enumerate(((gm_ref, om_ref), (gd_ref, od_ref), (gs_ref, os_ref))):
        term = jax.nn.sigmoid(g_ref[...].astype(F32)) * jnp.dot(
            x_ref[...], w_ref[k].astype(BF16), preferred_element_type=F32)
        acc = term if acc is None else acc + term
    o_ref[...] = acc.astype(o_ref.dtype)


def merge_branches(o_m, o_d, o_s, w_branch, layer, proj, gate_col, tm=1024, tn=512):
    t, kdim = o_m.shape
    d = w_branch.shape[3]
    tm, tn = min(tm, t), min(tn, d)
    ospec = pl.BlockSpec((tm, kdim), lambda i, j: (i, 0))
    gspecs = [pl.BlockSpec((tm, tn), functools.partial(lambda i, j, c: (i, c + j), c=(gate_col + c * d) // tn))
              for c in range(3)]
    return pl.pallas_call(
        _merge_kernel,
        grid=(t // tm, d // tn),
        in_specs=[ospec, ospec, ospec,
                  pl.BlockSpec((None, 3, kdim, tn), lambda i, j: (layer, 0, 0, j))] + gspecs,
        out_specs=pl.BlockSpec((tm, tn), lambda i, j: (i, j)),
        out_shape=jax.ShapeDtypeStruct((t, d), BF16),
        compiler_params=_params("parallel", "parallel"),
        name="merge_branches",
    )(o_m, o_d, o_s, w_branch, proj, proj, proj)


def _extract_top(x, rounds, break_ties):
    rows, n = x.shape
    rid = lax.broadcasted_iota(jnp.int32, (rows, n), 0)
    kid = lax.broadcasted_iota(jnp.int32, (rounds, n), 0)

    def body(k, carry):
        x, rank, vals = carry
        best = jnp.max(x, axis=0, keepdims=True)
        hit = x == best
        if break_ties:
            hit = rid == jnp.min(jnp.where(hit, rid, rows), axis=0, keepdims=True)
        rank = jnp.where(hit, k, rank)
        x = jnp.where(hit, -jnp.inf, x)
        vals = jnp.where(kid == k, best, vals)
        return x, rank, vals

    init = (x, jnp.full((rows, n), rounds, jnp.int32), jnp.zeros((rounds, n), F32))
    _, rank, vals = lax.fori_loop(0, rounds, body, init)
    return vals, rank


def _peer_score_kernel(q_ref, keys_ref, cb_o, eb_o, na_o, wa_o):
    nk, topk = PEER_NKEYS, PEER_TOPK
    s = lax.dot_general(keys_ref[...], q_ref[...], _NT, precision=lax.Precision.HIGHEST,
                        preferred_element_type=F32)

    def taken_off(rank):
        taken = jnp.sum((rank < topk).astype(jnp.int32), axis=0, keepdims=True)
        return jnp.max(jnp.where(taken != topk, 1, 0))

    suspect = _peer_select(s, cb_o, eb_o, na_o, wa_o, False, taken_off)

    @pl.when(suspect > 0)
    def _():
        _peer_select(s, cb_o, eb_o, na_o, wa_o, True, taken_off)


def _peer_select(s, cb_o, eb_o, na_o, wa_o, break_ties, taken_off):
    nk, topk = PEER_NKEYS, PEER_TOPK
    sa, sb = s[:nk], s[nk:]
    va, rank_a = _extract_top(sa, topk, break_ties)
    vb, rank_b = _extract_top(sb, topk, break_ties)
    n = s.shape[1]
    row8 = lax.broadcasted_iota(jnp.int32, (8, n), 0)
    ninf = -jnp.inf
    a = [va[r:r + 1] for r in range(topk)]
    groups = [a[0] + vb]
    for r, lim in ((1, 8), (2, 5), (3, 4), (4, 3)):
        groups.append(jnp.where(row8 < lim, a[r] + vb[0:8], ninf))
    a_mix = jnp.where(row8 < 2, a[5], jnp.where(row8 < 4, a[6], jnp.where(
        row8 < 6, a[7], jnp.where(row8 == 6, a[8], a[9]))))
    b_mix = jnp.where((row8 < 6) & ((row8 & 1) == 1), vb[1:2], vb[0:1])
    groups.append(a_mix + b_mix)
    tail = jnp.full((8, n), ninf, F32)
    for k in range(6):
        tail = jnp.where(row8 == k, a[10 + k] + vb[0:1], tail)
    groups.append(tail)
    cand = jnp.concatenate(groups, axis=0)
    top_s, rank_c = _extract_top(cand, topk, break_ties)
    taken = (rank_c < topk).astype(jnp.int32)
    counts = [jnp.sum(taken[0:16], axis=0, keepdims=True)]
    counts += [jnp.sum(taken[8 + 8 * r:16 + 8 * r], axis=0, keepdims=True) for r in range(1, 5)]
    counts += [taken[48 + 2 * k:49 + 2 * k] + taken[49 + 2 * k:50 + 2 * k] for k in range(3)]
    counts += [taken[54 + k:55 + k] for k in range(8)]
    count = jnp.concatenate(counts, axis=0)
    z = jnp.sum(jnp.exp(top_s - top_s[0:1]), axis=0, keepdims=True)
    na = jnp.zeros(sa.shape, jnp.int32)
    for r in range(topk):
        na = jnp.where(rank_a == r, count[r:r + 1], na)
    cb_o[...] = rank_b.astype(F32)
    eb_o[...] = jnp.exp(sb - vb[0:1])
    na_o[...] = na.astype(F32)
    wa_o[...] = jnp.exp(sa - va[0:1]) / z
    return taken_off(rank_a) + taken_off(rank_b) + taken_off(rank_c)


def peer_scores(q, key_a, key_b, tn=256):
    t = q.shape[0]
    tn = min(tn, t)
    zeros = jnp.zeros_like(key_a)
    keys = jnp.concatenate([jnp.concatenate([key_a, zeros], axis=-1),
                            jnp.concatenate([zeros, key_b], axis=-1)], axis=1)
    out = jax.ShapeDtypeStruct((PEER_HEADS, PEER_NKEYS, t), F32)
    ospec = pl.BlockSpec((None, PEER_NKEYS, tn), lambda i, h: (h, 0, i))
    return pl.pallas_call(
        _peer_score_kernel,
        grid=(t // tn, PEER_HEADS),
        in_specs=[pl.BlockSpec((tn, LANES), lambda i, h: (i, h)),
                  pl.BlockSpec((None, 2 * PEER_NKEYS, LANES), lambda i, h: (h, 0, 0))],
        out_specs=[ospec] * 4,
        out_shape=[out] * 4,
        compiler_params=_params("parallel", "parallel"),
        name="peer_scores",
    )(q, keys)


def _peer_dense_kernel(xt_ref, u_ref, vt_ref, cb_ref, eb_ref, na_ref, wa_ref, o_ref,
                       coef_even, coef_odd, *, te):
    e = pl.program_id(1)
    last = pl.num_programs(1) - 2
    tn = o_ref.shape[1]
    half = tn // 2

    @pl.when(e == 0)
    def _():
        o_ref[...] = jnp.zeros_like(o_ref)
        coef_odd[...] = jnp.zeros_like(coef_odd)

    def gate_into_coef(hid, lane0, coef_out):
        chunk = 16
        for di in range(te // PEER_NKEYS):
            i = jnp.minimum(e, last) * (te // PEER_NKEYS) + di
            for sub in range(0, hid.shape[1], LANES):
                lanes = slice(lane0 + sub, lane0 + sub + LANES)
                n_rows = [na_ref[i, h:h + 1, lanes] for h in range(PEER_HEADS)]
                w_rows = [wa_ref[i, h:h + 1, lanes] for h in range(PEER_HEADS)]
                for r in range(0, PEER_NKEYS, chunk):
                    rs = slice(r, r + chunk)
                    w = None
                    for h in range(PEER_HEADS):
                        term = jnp.where(cb_ref[h, rs, lanes] < n_rows[h],
                                         eb_ref[h, rs, lanes] * w_rows[h], 0.0)
                        w = term if w is None else w + term
                    row0 = di * PEER_NKEYS + r
                    x = hid[row0:row0 + chunk, sub:sub + LANES]
                    act = 0.5 * x * (1.0 + lax.erf(x * (2.0 ** -0.5)))
                    coef_out[row0:row0 + chunk, lanes] = (act * w).astype(BF16)

    def step(coef_out, coef_in):
        lo, hi = slice(0, half), slice(half, tn)
        hid_lo = jnp.dot(u_ref[...], xt_ref[:, lo], preferred_element_type=F32)
        o_ref[:, lo] += jnp.dot(vt_ref[...], coef_in[:, lo], preferred_element_type=F32)
        hid_hi = jnp.dot(u_ref[...], xt_ref[:, hi], preferred_element_type=F32)
        gate_into_coef(hid_lo, 0, coef_out)
        o_ref[:, hi] += jnp.dot(vt_ref[...], coef_in[:, hi], preferred_element_type=F32)
        gate_into_coef(hid_hi, half, coef_out)

    @pl.when((e & 1) == 0)
    def _():
        step(coef_even, coef_odd)

    @pl.when((e & 1) == 1)
    def _():
        step(coef_odd, coef_even)


def peer_dense(xt, u, vt, cb, eb, na, wa, tn=512, te=PEER_TILE):
    d, t = xt.shape
    n_exp = u.shape[0]
    tn = min(tn, t)
    n_e = n_exp // te
    assert vt.shape == (n_e, d, te)
    once = pl.Buffered(1)
    sspec = pl.BlockSpec((PEER_HEADS, PEER_NKEYS, tn), lambda i, e: (0, 0, i), pipeline_mode=once)
    aspec = pl.BlockSpec((PEER_NKEYS, PEER_HEADS, tn), lambda i, e: (0, 0, i), pipeline_mode=once)
    na, wa = na.transpose(1, 0, 2), wa.transpose(1, 0, 2)
    return pl.pallas_call(
        functools.partial(_peer_dense_kernel, te=te),
        grid=(t // tn, n_e + 1),
        in_specs=[pl.BlockSpec((d, tn), lambda i, e: (0, i), pipeline_mode=once),
                  pl.BlockSpec((te, d), lambda i, e: (jnp.minimum(e, n_e - 1), 0)),
                  pl.BlockSpec((None, d, te), lambda i, e: (jnp.maximum(e - 1, 0), 0, 0)),
                  sspec, sspec, aspec, aspec],
        out_specs=pl.BlockSpec((d, tn), lambda i, e: (0, i)),
        out_shape=jax.ShapeDtypeStruct((d, t), F32),
        scratch_shapes=[pltpu.VMEM((te, tn), BF16), pltpu.VMEM((te, tn), BF16)],
        compiler_params=_params("parallel", "arbitrary"),
        name="peer_dense",
    )(xt, u, vt, cb, eb, na, wa)


def kernel(x, p, positions, norm_mix, w_in, qn_moba, kn_moba, qn_diff, kn_diff, lambda_q1, lambda_k1, lambda_q2, lambda_k2, subln_diff, w_branch, w_out, norm_ffn, peer_wq, peer_key_a, peer_key_b, peer_u, peer_v, norm_ple, w_ple_gate, w_ple_proj):
    b, s, d = x.shape
    t = b * s
    depth = w_in.shape[0]
    nb = s // MOBA_BLOCK
    gate_col = 9 * MIX_WIDTH
    h = x.reshape(t, d)
    p_flat = p.reshape(depth, t, -1)
    for i in range(depth):
        n1 = rmsnorm(h, norm_mix[i])
        proj = matmul(n1, w_in, i, out_dtype=BF16)
        qm, qmf, km, qd, kd, kmean = prep_qkv(
            proj, positions, qn_moba[i], kn_moba[i], qn_diff[i], kn_diff[i])
        kmean = kmean.reshape(b, nb, N_HEADS, HEAD_DIM).transpose(0, 2, 1, 3)
        kmean = jnp.pad(kmean, ((0, 0), (0, 0), (0, -nb % 8), (0, 0)))
        o_m = moba_attention(qm, qmf, kmean, km, proj, 2, b, s)
        lam_init = 0.8 - 0.6 * math.exp(-0.3 * i)
        o_d = diff_attention(qd, kd, proj, 5, lambda_q1[i], lambda_k1[i], lambda_q2[i], lambda_k2[i],
                             subln_diff[i], lam_init, b, s)
        o_s = stick_attention(proj, 6, b, s)
        merged = merge_branches(o_m, o_d, o_s, w_branch, i, proj, gate_col)
        h = matmul(merged, w_out, i, mode="residual", res=h)
        n2, n2_t = rmsnorm(h, norm_ffn[i], with_transpose=True)
        q = matmul(n2, peer_wq, i)
        cb, eb, na, wa = peer_scores(q, peer_key_a[i], peer_key_b[i])
        ffn_t = peer_dense(n2_t, to_bf16(peer_u, i), to_bf16(peer_v, i, transpose_tile=PEER_TILE),
                           cb, eb, na, wa, te=PEER_TILE)
        h, n3 = residual_rmsnorm(h, ffn_t, norm_ple[i])
        h = matmul(n3, w_ple_gate, i, mode="ple", res=h, p=p_flat, wp=w_ple_proj)
    return h.reshape(b, s, d)
```

```python
import functools
import math

import jax
import jax.numpy as jnp
from jax import lax
from jax.experimental import pallas as pl
from jax.experimental.pallas import tpu as pltpu

F32 = jnp.float32
BF16 = jnp.bfloat16

LANES = 128
HEAD_DIM = 128
N_HEADS = 8
MIX_WIDTH = N_HEADS * HEAD_DIM
DIFF_SUB = HEAD_DIM // 2
ROPE_THETA = 500000.0
ROT_HALF = HEAD_DIM // 8
ROT_HALF_DIFF = DIFF_SUB // 8
MOBA_BLOCK = 256
MOBA_TOPK = 3
PEER_HEADS = 8
PEER_NKEYS = 128
PEER_TOPK = 16
PEER_TILE = 512
NORM_EPS = 1e-6
NEG_INF = -1e30
V7X_VMEM_BYTES = 64 * 1024 * 1024
VMEM_LIMIT = V7X_VMEM_BYTES * 7 // 8

_NT = (((1,), (1,)), ((), ()))


def _params(*sem):
    return pltpu.CompilerParams(dimension_semantics=sem, vmem_limit_bytes=VMEM_LIMIT)


def _rmsnorm_kernel(x_ref, g_ref, o_ref, *maybe_ot_ref):
    x = x_ref[...]
    ms = jnp.mean(x * x, axis=-1, keepdims=True)
    y = x * lax.rsqrt(ms + NORM_EPS) * g_ref[...]
    o_ref[...] = y.astype(o_ref.dtype)
    for ot_ref in maybe_ot_ref:
        ot_ref[...] = y.T.astype(ot_ref.dtype)


def rmsnorm(x, gain, tm=256, with_transpose=False):
    t, d = x.shape
    tm = min(tm, t)
    out_specs = [pl.BlockSpec((tm, d), lambda i: (i, 0))]
    out_shape = [jax.ShapeDtypeStruct((t, d), BF16)]
    if with_transpose:
        out_specs.append(pl.BlockSpec((d, tm), lambda i: (0, i)))
        out_shape.append(jax.ShapeDtypeStruct((d, t), BF16))
    outs = pl.pallas_call(
        _rmsnorm_kernel,
        grid=(t // tm,),
        in_specs=[pl.BlockSpec((tm, d), lambda i: (i, 0)),
                  pl.BlockSpec((1, d), lambda i: (0, 0))],
        out_specs=out_specs,
        out_shape=out_shape,
        compiler_params=_params("parallel"),
        name="rmsnorm_t" if with_transpose else "rmsnorm",
    )(x, gain.reshape(1, d))
    return outs if with_transpose else outs[0]


def _residual_rmsnorm_kernel(x_ref, yt_ref, g_ref, h_ref, n_ref):
    h = x_ref[...] + yt_ref[...].T
    h_ref[...] = h
    ms = jnp.mean(h * h, axis=-1, keepdims=True)
    n_ref[...] = (h * lax.rsqrt(ms + NORM_EPS) * g_ref[...]).astype(n_ref.dtype)


def residual_rmsnorm(x, y_t, gain, tm=256):
    t, d = x.shape
    tm = min(tm, t)
    row = pl.BlockSpec((tm, d), lambda i: (i, 0))
    return pl.pallas_call(
        _residual_rmsnorm_kernel,
        grid=(t // tm,),
        in_specs=[row, pl.BlockSpec((d, tm), lambda i: (0, i)), pl.BlockSpec((1, d), lambda i: (0, 0))],
        out_specs=[row, row],
        out_shape=[jax.ShapeDtypeStruct((t, d), F32), jax.ShapeDtypeStruct((t, d), BF16)],
        compiler_params=_params("parallel"),
        name="residual_rmsnorm",
    )(x, y_t, gain.reshape(1, d))


def _relayout_kernel(x_ref, o_ref, *, transpose_tile):
    x = x_ref[...]
    if transpose_tile is None:
        o_ref[...] = x.astype(o_ref.dtype)
    else:
        xt = x.T.astype(o_ref.dtype)
        for a in range(o_ref.shape[0]):
            o_ref[a] = xt[:, a * transpose_tile:(a + 1) * transpose_tile]


def to_bf16(x, layer, transpose_tile=None, tb=1024):
    _, r, c = x.shape
    tr, tc = min(tb, r), min(tb, c)
    if transpose_tile is None:
        out_spec, out_dims = pl.BlockSpec((tr, tc), lambda i, j: (i, j)), (r, c)
    else:
        n = min(transpose_tile, tr)
        transpose_tile = n
        out_spec = pl.BlockSpec((tr // n, tc, n), lambda i, j: (i, j, 0))
        out_dims = (r // n, c, n)
    return pl.pallas_call(
        functools.partial(_relayout_kernel, transpose_tile=transpose_tile),
        grid=(r // tr, c // tc),
        in_specs=[pl.BlockSpec((None, tr, tc), lambda i, j: (layer, i, j))],
        out_specs=out_spec,
        out_shape=jax.ShapeDtypeStruct(out_dims, BF16),
        compiler_params=_params("parallel", "parallel"),
        name="to_bf16" if transpose_tile is None else "to_bf16_t",
    )(x)


def _mm_kernel(*refs, mode):
    if mode == "plain":
        a_ref, b_ref, o_ref = refs
    elif mode == "residual":
        a_ref, b_ref, r_ref, o_ref = refs
    else:
        a_ref, b_ref, r_ref, p_ref, wp_ref, o_ref = refs
    acc = jnp.dot(a_ref[...], b_ref[...].astype(BF16), preferred_element_type=F32)
    if mode == "plain":
        o_ref[...] = acc.astype(o_ref.dtype)
    elif mode == "residual":
        o_ref[...] = (r_ref[...] + acc).astype(o_ref.dtype)
    else:
        emb = jnp.dot(p_ref[...].astype(BF16), wp_ref[...].astype(BF16),
                      preferred_element_type=F32)
        o_ref[...] = (r_ref[...] + jax.nn.sigmoid(acc) * emb).astype(o_ref.dtype)


def matmul(a, w, layer, *, mode="plain", res=None, p=None, wp=None, out_dtype=F32, tm=2048, tn=None):
    m, kdim = a.shape
    n = w.shape[2]
    if tn is None:
        tn = 256 if mode == "ple" else 512
    tm, tn = min(tm, m), min(tn, n)
    in_specs = [pl.BlockSpec((tm, kdim), lambda i, j: (i, 0), pipeline_mode=pl.Buffered(1)),
                pl.BlockSpec((None, kdim, tn), lambda i, j: (layer, 0, j))]
    args = [a, w]
    if mode in ("residual", "ple"):
        in_specs.append(pl.BlockSpec((tm, tn), lambda i, j: (i, j)))
        args.append(res)
    if mode == "ple":
        pd = p.shape[2]
        in_specs += [pl.BlockSpec((None, tm, pd), lambda i, j: (layer, i, 0)),
                     pl.BlockSpec((None, pd, tn), lambda i, j: (layer, 0, j))]
        args += [p, wp]
    return pl.pallas_call(
        functools.partial(_mm_kernel, mode=mode),
        grid=(m // tm, n // tn),
        in_specs=in_specs,
        out_specs=pl.BlockSpec((tm, tn), lambda i, j: (i, j)),
        out_shape=jax.ShapeDtypeStruct((m, n), out_dtype),
        compiler_params=_params("parallel", "arbitrary"),
        name="matmul_" + mode,
    )(*args)


def _rope(x, cos, sin, half, period):
    lane = lax.broadcasted_iota(jnp.int32, x.shape, 1)
    first = (lane & (period - 1)) < half
    partner = jnp.where(first, pltpu.roll(x, LANES - half, 1), pltpu.roll(x, half, 1))
    return x * cos + partner * jnp.where(first, -sin, sin)


def _prep_kernel(pos_ref, fm_ref, fd_ref, qng_ref, kng_ref, qdg_ref, kdg_ref,
                 qm_i, km_i, qd_i, kd_i, qm_o, qmf_o, km_o, qd_o, kd_o, kmean_o):
    pos = pos_ref[...].astype(F32)
    ang_m = pos * fm_ref[...]
    ang_d = pos * fd_ref[...]
    cos_m, sin_m = jnp.cos(ang_m), jnp.sin(ang_m)
    cos_d, sin_d = jnp.cos(ang_d), jnp.sin(ang_d)
    lane = lax.broadcasted_iota(jnp.int32, ang_m.shape, 1)
    low = lane < DIFF_SUB

    def norm_full(x, g):
        ms = jnp.mean(x * x, axis=-1, keepdims=True)
        return x * lax.rsqrt(ms + NORM_EPS) * g

    def norm_halves(x, g):
        xx = x * x
        lo = jnp.sum(jnp.where(low, xx, 0.0), axis=-1, keepdims=True)
        hi = jnp.sum(jnp.where(low, 0.0, xx), axis=-1, keepdims=True)
        ms = jnp.where(low, lo, hi) * (1.0 / DIFF_SUB)
        return x * lax.rsqrt(ms + NORM_EPS) * g

    for h in range(N_HEADS):
        sl = slice(h * HEAD_DIM, (h + 1) * HEAD_DIM)
        qm = _rope(norm_full(qm_i[:, sl].astype(F32), qng_ref[...]), cos_m, sin_m, ROT_HALF, HEAD_DIM)
        km = _rope(norm_full(km_i[:, sl].astype(F32), kng_ref[...]), cos_m, sin_m, ROT_HALF, HEAD_DIM)
        qmf_o[:, sl] = qm
        qm_o[:, sl] = qm.astype(BF16)
        km_o[:, sl] = km.astype(BF16)
        kmean_o[0, h:h + 1, :] = jnp.mean(km, axis=0, keepdims=True)
        qd = _rope(norm_halves(qd_i[:, sl].astype(F32), qdg_ref[...]), cos_d, sin_d,
                   ROT_HALF_DIFF, DIFF_SUB)
        kd = _rope(norm_halves(kd_i[:, sl].astype(F32), kdg_ref[...]), cos_d, sin_d,
                   ROT_HALF_DIFF, DIFF_SUB)
        qd_o[:, sl] = qd.astype(BF16)
        kd_o[:, sl] = kd.astype(BF16)


def _rope_table(half, period):
    lane = jnp.arange(LANES)
    inv_freq = ROPE_THETA ** (-jnp.arange(half, dtype=F32) / half)
    tab = jnp.where((lane % period) < 2 * half, inv_freq[lane % half], 0.0)
    return tab.reshape(1, LANES).astype(F32)


def prep_qkv(proj, positions, qn_m, kn_m, qn_d, kn_d):
    t = proj.shape[0]
    tm = MOBA_BLOCK
    w = MIX_WIDTH
    row = lambda i: (i, 0)
    col_spec = [pl.BlockSpec((tm, w), functools.partial(lambda i, c: (i, c), c=c)) for c in (0, 1, 3, 4)]
    vec = pl.BlockSpec((1, LANES), lambda i: (0, 0))
    bf = jax.ShapeDtypeStruct((t, w), BF16)
    out_shape = [bf, jax.ShapeDtypeStruct((t, w), F32), bf, bf, bf,
                 jax.ShapeDtypeStruct((t // tm, N_HEADS, HEAD_DIM), F32)]
    out_specs = [pl.BlockSpec((tm, w), row)] * 5 + [
        pl.BlockSpec((1, N_HEADS, HEAD_DIM), lambda i: (i, 0, 0))]
    return pl.pallas_call(
        _prep_kernel,
        grid=(t // tm,),
        in_specs=[pl.BlockSpec((tm, 1), row), vec, vec, vec, vec, vec, vec] + col_spec,
        out_specs=out_specs,
        out_shape=out_shape,
        compiler_params=_params("parallel"),
        name="prep_qkv",
    )(positions.reshape(t, 1), _rope_table(ROT_HALF, HEAD_DIM), _rope_table(ROT_HALF_DIFF, DIFF_SUB),
      qn_m.reshape(1, LANES), kn_m.reshape(1, LANES),
      jnp.tile(qn_d, 2).reshape(1, LANES), jnp.tile(kn_d, 2).reshape(1, LANES),
      *([proj] * 4))


def _ones_column(rows, at=0):
    return (lax.broadcasted_iota(jnp.int32, (rows, LANES), 1) == at).astype(BF16)


def _value_with_ones(v):
    return jnp.concatenate([v, jnp.ones(v.shape, BF16)], axis=1)


def _moba_kernel(q_ref, qf_ref, kmean_ref, k_ref, v_ref, o_ref, knorm_ref, *, tk):
    blk = MOBA_BLOCK
    tq = q_ref.shape[0]
    qi = pl.program_id(2)
    scale = HEAD_DIM ** -0.5
    big = NEG_INF / scale
    nbp = kmean_ref.shape[0]

    @pl.when(qi == 0)
    def _():
        knorm_ref[...] = jnp.broadcast_to(_max_row_norm(k_ref, 0, HEAD_DIM), knorm_ref.shape)

    gate = lax.dot_general(kmean_ref[...], qf_ref[...], _NT, precision=lax.Precision.HIGHEST,
                           preferred_element_type=F32)
    rid = lax.broadcasted_iota(jnp.int32, gate.shape, 0)
    own = qi * (tq // blk) + lax.broadcasted_iota(jnp.int32, gate.shape, 1) // blk
    rank = jnp.zeros(gate.shape, jnp.int32)
    for m in range(nbp):
        g_m = gate[m:m + 1, :]
        beats = (g_m > gate) | ((g_m == gate) & (rid > m))
        rank = rank + jnp.where(beats & (m < own), 1, 0)
    allowed = ((rid < own) & (rank < MOBA_TOPK)) | (rid == own)
    pen_t = jnp.where(allowed, 0.0, big)
    pen_t = jnp.concatenate([pen_t, jnp.full((LANES - nbp, tq), big, F32)], axis=0)
    penalty = pen_t.T

    q = q_ref[...]
    lane = lax.broadcasted_iota(jnp.int32, (tq, LANES), 1)
    q_pos = qi * tq + lax.broadcasted_iota(jnp.int32, (tq, tk), 0)
    col = lax.broadcasted_iota(jnp.int32, (tq, tk), 1)
    krow = lax.broadcasted_iota(jnp.int32, (tk, LANES), 0)
    klane = lax.broadcasted_iota(jnp.int32, (tk, LANES), 1)
    n_tiles = (qi * tq) // tk + 1

    def scores(n, q_aug, shift_lane, causal):
        off = pl.multiple_of(n * tk, tk)
        kb = k_ref[pl.ds(off, tk), :]
        block_of_row = n * (tk // blk) + krow // blk
        marks = ((klane == block_of_row) | (klane == shift_lane)).astype(BF16)
        s = lax.dot_general(q_aug, jnp.concatenate([kb, marks], axis=1), _NT,
                            preferred_element_type=F32) * scale
        if causal:
            s = jnp.where(off + col <= q_pos, s, NEG_INF)
        return s, off

    q_max = jnp.concatenate([q, penalty.astype(BF16)], axis=1)

    def sweep_max(n, best):
        s, _ = scores(n, q_max, -1, True)
        return jnp.maximum(best, s)

    def exact_row_max():
        best = lax.fori_loop(0, n_tiles, sweep_max, jnp.full((tq, tk), NEG_INF, F32))
        return jnp.max(best, axis=-1, keepdims=True)

    qsq = q.astype(F32)
    bound = jnp.sqrt(jnp.sum(qsq * qsq, axis=-1, keepdims=True)) * knorm_ref[0:1, 0:1] * scale
    shift = lax.cond(2.0 * jnp.max(bound) < SHIFT_SLACK, lambda: bound, exact_row_max)
    shift_lane = LANES - 1
    q_sum = jnp.concatenate(
        [q, jnp.where(lane == shift_lane, -shift / scale, penalty).astype(BF16)], axis=1)

    def tile_sum(n, causal):
        s, off = scores(n, q_sum, shift_lane, causal)
        vb = v_ref[pl.ds(off, tk), :]
        return jnp.dot(jnp.exp(s).astype(BF16), _value_with_ones(vb), preferred_element_type=F32)

    acc = lax.fori_loop(0, n_tiles - 1, lambda n, acc: acc + tile_sum(n, False),
                        jnp.zeros((tq, 2 * HEAD_DIM), F32))
    acc = acc + tile_sum(n_tiles - 1, True)
    o_ref[...] = (acc[:, :HEAD_DIM] / acc[:, HEAD_DIM:]).astype(o_ref.dtype)


def moba_attention(q, qf, kmean, k, v, v_group, batch, seq, tq=1024, tk=1024):
    t = q.shape[0]
    tq, tk = min(tq, seq), min(tk, seq)
    assert tq % MOBA_BLOCK == 0 and tk % tq == 0
    nq = seq // tq
    qspec = pl.BlockSpec((tq, HEAD_DIM), lambda b, h, i: (b * nq + i, h))
    kspec = pl.BlockSpec((seq, HEAD_DIM), lambda b, h, i: (b, h))
    vspec = pl.BlockSpec((seq, HEAD_DIM), lambda b, h, i: (b, v_group * N_HEADS + h))
    return pl.pallas_call(
        functools.partial(_moba_kernel, tk=tk),
        grid=(batch, N_HEADS, nq),
        in_specs=[qspec, qspec,
                  pl.BlockSpec((None, None, kmean.shape[2], HEAD_DIM), lambda b, h, i: (b, h, 0, 0)),
                  kspec, vspec],
        out_specs=qspec,
        out_shape=jax.ShapeDtypeStruct((t, MIX_WIDTH), BF16),
        scratch_shapes=[pltpu.VMEM((8, LANES), F32)],
        compiler_params=_params("parallel", "parallel", "arbitrary"),
        name="moba_attention",
    )(q, qf, kmean, k, v)


SHIFT_SLACK = 80.0


def _max_row_norm(x_ref, lane_lo, lane_hi, chunk=512):
    n_rows = x_ref.shape[0]
    chunk = min(chunk, n_rows)
    lane = lax.broadcasted_iota(jnp.int32, (chunk, x_ref.shape[1]), 1)
    keep = (lane >= lane_lo) & (lane < lane_hi)

    def body(c, best):
        x = x_ref[pl.ds(pl.multiple_of(c * chunk, chunk), chunk), :].astype(F32)
        sq = jnp.sum(jnp.where(keep, x * x, 0.0), axis=-1, keepdims=True)
        return jnp.maximum(best, jnp.max(sq, axis=0, keepdims=True))

    return jnp.sqrt(lax.fori_loop(0, n_rows // chunk, body, jnp.zeros((1, 1), F32)))


def _diff_kernel(lq1_ref, lk1_ref, lq2_ref, lk2_ref, g_ref, q_ref, k_ref, v_ref, o_ref, knorm_ref,
                 *, tq, tk, lam_init):
    qi = pl.program_id(2)

    @pl.when(qi == 0)
    def _():
        knorm_ref[0] = jnp.broadcast_to(_max_row_norm(k_ref, 0, DIFF_SUB), knorm_ref.shape[1:])
        knorm_ref[1] = jnp.broadcast_to(_max_row_norm(k_ref, DIFF_SUB, HEAD_DIM), knorm_ref.shape[1:])

    q = q_ref[...] * (DIFF_SUB ** -0.5)
    lane = lax.broadcasted_iota(jnp.int32, q.shape, 1)
    zero = jnp.zeros_like(q)
    qq = jnp.concatenate([jnp.where(lane < DIFF_SUB, q, zero),
                          jnp.where(lane < DIFF_SUB, zero, q)], axis=0)
    rows = 2 * tq
    q_pos = qi * tq + (lax.broadcasted_iota(jnp.int32, (rows, tk), 0) & (tq - 1))
    col = lax.broadcasted_iota(jnp.int32, (rows, tk), 1)
    n_full = (qi * tq) // tk
    last = pl.multiple_of(n_full * tk, tk)
    causal = last + col <= q_pos

    def lane_tile_max(s):
        out = s[:, :LANES]
        for c in range(LANES, tk, LANES):
            out = jnp.maximum(out, s[:, c:c + LANES])
        return out

    def sweep_max(n, best):
        kb = k_ref[pl.ds(pl.multiple_of(n * tk, tk), tk), :]
        return jnp.maximum(best, lane_tile_max(lax.dot_general(qq, kb, _NT, preferred_element_type=F32)))

    def exact_row_max():
        best = lax.fori_loop(0, n_full, sweep_max, jnp.full((rows, LANES), NEG_INF, F32))
        s_last = lax.dot_general(qq, k_ref[pl.ds(last, tk), :], _NT, preferred_element_type=F32)
        best = jnp.maximum(best, lane_tile_max(jnp.where(causal, s_last, NEG_INF)))
        return jnp.max(best, axis=-1, keepdims=True)

    qf = qq.astype(F32)
    q_norm = jnp.sqrt(jnp.sum(qf * qf, axis=-1, keepdims=True))
    first_half = lax.broadcasted_iota(jnp.int32, (rows, 1), 0) < tq
    bound = q_norm * jnp.where(first_half, knorm_ref[0][0:1, 0:1], knorm_ref[1][0:1, 0:1])
    shift = lax.cond(2.0 * jnp.max(bound) < SHIFT_SLACK, lambda: bound, exact_row_max)

    lane_r = lax.broadcasted_iota(jnp.int32, (rows, LANES), 1)
    q_aug = jnp.concatenate([qq, jnp.where(lane_r == 0, -shift, 0.0).astype(BF16)], axis=1)
    ones_col = _ones_column(tk)

    def tile_sum(off, mask):
        k_aug = jnp.concatenate([k_ref[pl.ds(off, tk), :], ones_col], axis=1)
        v_aug = _value_with_ones(v_ref[pl.ds(off, tk), :])
        outs = []
        for half in range(2):
            s = lax.dot_general(q_aug[half * tq:(half + 1) * tq], k_aug, _NT,
                                preferred_element_type=F32)
            if mask is not None:
                s = jnp.where(mask[half * tq:(half + 1) * tq], s, NEG_INF)
            outs.append(jnp.dot(jnp.exp(s).astype(BF16), v_aug, preferred_element_type=F32))
        return jnp.concatenate(outs, axis=0)

    def sweep_sum(n, acc):
        return acc + tile_sum(pl.multiple_of(n * tk, tk), None)

    acc = lax.fori_loop(0, n_full, sweep_sum, jnp.zeros((rows, 2 * HEAD_DIM), F32))
    acc = acc + tile_sum(last, causal)

    lam = (jnp.exp(jnp.sum(lq1_ref[...] * lk1_ref[...], axis=-1, keepdims=True))
           - jnp.exp(jnp.sum(lq2_ref[...] * lk2_ref[...], axis=-1, keepdims=True)) + lam_init)
    out = acc[:, :HEAD_DIM] / acc[:, HEAD_DIM:]
    o = out[:tq] - lam * out[tq:]
    ms = jnp.mean(o * o, axis=-1, keepdims=True)
    o_ref[...] = (o * lax.rsqrt(ms + NORM_EPS) * g_ref[...] * (1.0 - lam_init)).astype(o_ref.dtype)


def diff_attention(q, k, v, v_group, lq1, lk1, lq2, lk2, subln, lam_init, batch, seq, tq=1024, tk=1024):
    t = q.shape[0]
    tq, tk = min(tq, seq), min(tk, seq)
    nq = seq // tq
    qspec = pl.BlockSpec((tq, HEAD_DIM), lambda b, h, i: (b * nq + i, h))
    kspec = pl.BlockSpec((seq, HEAD_DIM), lambda b, h, i: (b, h))
    vspec = pl.BlockSpec((seq, HEAD_DIM), lambda b, h, i: (b, v_group * N_HEADS + h))
    lspec = pl.BlockSpec((1, DIFF_SUB), lambda b, h, i: (0, 0))
    return pl.pallas_call(
        functools.partial(_diff_kernel, tq=tq, tk=tk, lam_init=lam_init),
        grid=(batch, N_HEADS, nq),
        in_specs=[lspec, lspec, lspec, lspec,
                  pl.BlockSpec((1, HEAD_DIM), lambda b, h, i: (0, 0)), qspec, kspec, vspec],
        out_specs=qspec,
        out_shape=jax.ShapeDtypeStruct((t, MIX_WIDTH), BF16),
        scratch_shapes=[pltpu.VMEM((2, 8, LANES), F32)],
        compiler_params=_params("parallel", "parallel", "arbitrary"),
        name="diff_attention",
    )(lq1.reshape(1, DIFF_SUB), lk1.reshape(1, DIFF_SUB), lq2.reshape(1, DIFF_SUB),
      lk2.reshape(1, DIFF_SUB), subln.reshape(1, HEAD_DIM), q, k, v)


def _stick_kernel(q_ref, k_ref, v_ref, o_ref, *, tq, tk):
    qi = pl.program_id(2)
    scale = HEAD_DIM ** -0.5
    n_heads = q_ref.shape[1] // HEAD_DIM
    upper = (lax.broadcasted_iota(jnp.int32, (tk, tk), 0)
             > lax.broadcasted_iota(jnp.int32, (tk, tk), 1)).astype(BF16)
    q_pos = qi * tq + lax.broadcasted_iota(jnp.int32, (tq, tk), 0)
    col = lax.broadcasted_iota(jnp.int32, (tq, tk), 1)
    n_diag = tq // tk
    n_past = qi * n_diag

    def tile(n, hd, later, acc, masked):
        off = pl.multiple_of(n * tk, tk)
        cols = slice(hd * HEAD_DIM, (hd + 1) * HEAD_DIM)
        kb = k_ref[pl.ds(off, tk), cols]
        vb = v_ref[pl.ds(off, tk), cols]
        z = lax.dot_general(q_ref[:, cols], kb, _NT, preferred_element_type=F32) * scale
        sp = jnp.maximum(z, 0.0) + jnp.log(1.0 + jnp.exp(-jnp.abs(z)))
        if masked:
            strict = (off + col) < q_pos
            sp = jnp.where(strict, sp, 0.0)
        hi = sp.astype(BF16)
        lo = (sp - hi.astype(F32)).astype(BF16)
        between = (jnp.dot(hi, upper, preferred_element_type=F32)
                   + jnp.dot(lo, upper, preferred_element_type=F32))
        a = jnp.exp(z - sp - between - later)
        if masked:
            a = jnp.where(strict, a, 0.0)
        acc = acc + jnp.dot(a.astype(BF16), vb, preferred_element_type=F32)
        return later + jnp.sum(sp, axis=-1, keepdims=True), acc

    state = [(jnp.zeros((tq, 1), F32), jnp.zeros((tq, HEAD_DIM), F32)) for _ in range(n_heads)]
    for d in range(n_diag):
        state = [tile(n_past + n_diag - 1 - d, hd, *state[hd], True) for hd in range(n_heads)]

    def body(it, carry):
        return tuple(tile(n_past - 1 - it, hd, *carry[hd], False) for hd in range(n_heads))

    state = lax.fori_loop(0, n_past, body, tuple(state))
    for hd in range(n_heads):
        o_ref[:, hd * HEAD_DIM:(hd + 1) * HEAD_DIM] = state[hd][1].astype(o_ref.dtype)


def stick_attention(qkv, q_group, batch, seq, tq=1024, tk=256, heads_per_step=2):
    t = qkv.shape[0]
    tq, tk = min(tq, seq), min(tk, seq)
    nq = seq // tq
    width = heads_per_step * HEAD_DIM
    per_group = MIX_WIDTH // width

    def col(group):
        return lambda b, h, i: (b, group * per_group + h)

    qspec = pl.BlockSpec((tq, width), lambda b, h, i: (b * nq + i, q_group * per_group + h))
    kspec = pl.BlockSpec((seq, width), col(q_group + 1))
    vspec = pl.BlockSpec((seq, width), col(q_group + 2))
    return pl.pallas_call(
        functools.partial(_stick_kernel, tq=tq, tk=tk),
        grid=(batch, N_HEADS // heads_per_step, nq),
        in_specs=[qspec, kspec, vspec],
        out_specs=pl.BlockSpec((tq, width), lambda b, h, i: (b * nq + i, h)),
        out_shape=jax.ShapeDtypeStruct((t, MIX_WIDTH), BF16),
        compiler_params=_params("parallel", "parallel", "arbitrary"),
        name="stick_attention",
    )(qkv, qkv, qkv)


def _merge_kernel(om_ref, od_ref, os_ref, w_ref, gm_ref, gd_ref, gs_ref, o_ref):
    acc = None
    for k, (g_ref, x_ref) in enumerate(((gm_ref, om_ref), (gd_ref, od_ref), (gs_ref, os_ref))):
        term = jax.nn.sigmoid(g_ref[...].astype(F32)) * jnp.dot(
            x_ref[...], w_ref[k].astype(BF16), preferred_element_type=F32)
        acc = term if acc is None else acc + term
    o_ref[...] = acc.astype(o_ref.dtype)


def merge_branches(o_m, o_d, o_s, w_branch, layer, proj, gate_col, tm=1024, tn=512):
    t, kdim = o_m.shape
    d = w_branch.shape[3]
    tm, tn = min(tm, t), min(tn, d)
    ospec = pl.BlockSpec((tm, kdim), lambda i, j: (i, 0))
    gspecs = [pl.BlockSpec((tm, tn), functools.partial(lambda i, j, c: (i, c + j), c=(gate_col + c * d) // tn))
              for c in range(3)]
    return pl.pallas_call(
        _merge_kernel,
        grid=(t // tm, d // tn),
        in_specs=[ospec, ospec, ospec,
                  pl.BlockSpec((None, 3, kdim, tn), lambda i, j: (layer, 0, 0, j))] + gspecs,
        out_specs=pl.BlockSpec((tm, tn), lambda i, j: (i, j)),
        out_shape=jax.ShapeDtypeStruct((t, d), BF16),
        compiler_params=_params("parallel", "parallel"),
        name="merge_branches",
    )(o_m, o_d, o_s, w_branch, proj, proj, proj)


def _extract_top(x, rounds, break_ties):
    rows, n = x.shape
    rid = lax.broadcasted_iota(jnp.int32, (rows, n), 0)
    kid = lax.broadcasted_iota(jnp.int32, (rounds, n), 0)

    def body(k, carry):
        x, rank, vals = carry
        best = jnp.max(x, axis=0, keepdims=True)
        hit = x == best
        if break_ties:
            hit = rid == jnp.min(jnp.where(hit, rid, rows), axis=0, keepdims=True)
        rank = jnp.where(hit, k, rank)
        x = jnp.where(hit, -jnp.inf, x)
        vals = jnp.where(kid == k, best, vals)
        return x, rank, vals

    init = (x, jnp.full((rows, n), rounds, jnp.int32), jnp.zeros((rounds, n), F32))
    _, rank, vals = lax.fori_loop(0, rounds, body, init)
    return vals, rank


def _peer_score_kernel(q_ref, keys_ref, cb_o, eb_o, na_o, wa_o):
    nk, topk = PEER_NKEYS, PEER_TOPK
    s = lax.dot_general(keys_ref[...], q_ref[...], _NT, precision=lax.Precision.HIGHEST,
                        preferred_element_type=F32)

    def taken_off(rank):
        taken = jnp.sum((rank < topk).astype(jnp.int32), axis=0, keepdims=True)
        return jnp.max(jnp.where(taken != topk, 1, 0))

    suspect = _peer_select(s, cb_o, eb_o, na_o, wa_o, False, taken_off)

    @pl.when(suspect > 0)
    def _():
        _peer_select(s, cb_o, eb_o, na_o, wa_o, True, taken_off)


def _peer_select(s, cb_o, eb_o, na_o, wa_o, break_ties, taken_off):
    nk, topk = PEER_NKEYS, PEER_TOPK
    sa, sb = s[:nk], s[nk:]
    va, rank_a = _extract_top(sa, topk, break_ties)
    vb, rank_b = _extract_top(sb, topk, break_ties)
    n = s.shape[1]
    row8 = lax.broadcasted_iota(jnp.int32, (8, n), 0)
    ninf = -jnp.inf
    a = [va[r:r + 1] for r in range(topk)]
    groups = [a[0] + vb]
    for r, lim in ((1, 8), (2, 5), (3, 4), (4, 3)):
        groups.append(jnp.where(row8 < lim, a[r] + vb[0:8], ninf))
    a_mix = jnp.where(row8 < 2, a[5], jnp.where(row8 < 4, a[6], jnp.where(
        row8 < 6, a[7], jnp.where(row8 == 6, a[8], a[9]))))
    b_mix = jnp.where((row8 < 6) & ((row8 & 1) == 1), vb[1:2], vb[0:1])
    groups.append(a_mix + b_mix)
    tail = jnp.full((8, n), ninf, F32)
    for k in range(6):
        tail = jnp.where(row8 == k, a[10 + k] + vb[0:1], tail)
    groups.append(tail)
    cand = jnp.concatenate(groups, axis=0)
    top_s, rank_c = _extract_top(cand, topk, break_ties)
    taken = (rank_c < topk).astype(jnp.int32)
    counts = [jnp.sum(taken[0:16], axis=0, keepdims=True)]
    counts += [jnp.sum(taken[8 + 8 * r:16 + 8 * r], axis=0, keepdims=True) for r in range(1, 5)]
    counts += [taken[48 + 2 * k:49 + 2 * k] + taken[49 + 2 * k:50 + 2 * k] for k in range(3)]
    counts += [taken[54 + k:55 + k] for k in range(8)]
    count = jnp.concatenate(counts, axis=0)
    z = jnp.sum(jnp.exp(top_s - top_s[0:1]), axis=0, keepdims=True)
    na = jnp.zeros(sa.shape, jnp.int32)
    for r in range(topk):
        na = jnp.where(rank_a == r, count[r:r + 1], na)
    cb_o[...] = rank_b.astype(F32)
    eb_o[...] = jnp.exp(sb - vb[0:1])
    na_o[...] = na.astype(F32)
    wa_o[...] = jnp.exp(sa - va[0:1]) / z
    return taken_off(rank_a) + taken_off(rank_b) + taken_off(rank_c)


def peer_scores(q, key_a, key_b, tn=256):
    t = q.shape[0]
    tn = min(tn, t)
    zeros = jnp.zeros_like(key_a)
    keys = jnp.concatenate([jnp.concatenate([key_a, zeros], axis=-1),
                            jnp.concatenate([zeros, key_b], axis=-1)], axis=1)
    out = jax.ShapeDtypeStruct((PEER_HEADS, PEER_NKEYS, t), F32)
    ospec = pl.BlockSpec((None, PEER_NKEYS, tn), lambda i, h: (h, 0, i))
    return pl.pallas_call(
        _peer_score_kernel,
        grid=(t // tn, PEER_HEADS),
        in_specs=[pl.BlockSpec((tn, LANES), lambda i, h: (i, h)),
                  pl.BlockSpec((None, 2 * PEER_NKEYS, LANES), lambda i, h: (h, 0, 0))],
        out_specs=[ospec] * 4,
        out_shape=[out] * 4,
        compiler_params=_params("parallel", "parallel"),
        name="peer_scores",
    )(q, keys)


def _peer_dense_kernel(xt_ref, u_ref, vt_ref, cb_ref, eb_ref, na_ref, wa_ref, o_ref,
                       coef_even, coef_odd, *, te):
    e = pl.program_id(1)
    last = pl.num_programs(1) - 2
    tn = o_ref.shape[1]
    half = tn // 2

    @pl.when(e == 0)
    def _():
        o_ref[...] = jnp.zeros_like(o_ref)
        coef_odd[...] = jnp.zeros_like(coef_odd)

    def gate_into_coef(hid, lane0, coef_out):
        chunk = 16
        for di in range(te // PEER_NKEYS):
            i = jnp.minimum(e, last) * (te // PEER_NKEYS) + di
            for sub in range(0, hid.shape[1], LANES):
                lanes = slice(lane0 + sub, lane0 + sub + LANES)
                n_rows = [na_ref[i, h:h + 1, lanes] for h in range(PEER_HEADS)]
                w_rows = [wa_ref[i, h:h + 1, lanes] for h in range(PEER_HEADS)]
                for r in range(0, PEER_NKEYS, chunk):
                    rs = slice(r, r + chunk)
                    w = None
                    for h in range(PEER_HEADS):
                        term = jnp.where(cb_ref[h, rs, lanes] < n_rows[h],
                                         eb_ref[h, rs, lanes] * w_rows[h], 0.0)
                        w = term if w is None else w + term
                    row0 = di * PEER_NKEYS + r
                    x = hid[row0:row0 + chunk, sub:sub + LANES]
                    act = 0.5 * x * (1.0 + lax.erf(x * (2.0 ** -0.5)))
                    coef_out[row0:row0 + chunk, lanes] = (act * w).astype(BF16)

    def step(coef_out, coef_in):
        lo, hi = slice(0, half), slice(half, tn)
        hid_lo = jnp.dot(u_ref[...], xt_ref[:, lo], preferred_element_type=F32)
        o_ref[:, lo] += jnp.dot(vt_ref[...], coef_in[:, lo], preferred_element_type=F32)
        hid_hi = jnp.dot(u_ref[...], xt_ref[:, hi], preferred_element_type=F32)
        gate_into_coef(hid_lo, 0, coef_out)
        o_ref[:, hi] += jnp.dot(vt_ref[...], coef_in[:, hi], preferred_element_type=F32)
        gate_into_coef(hid_hi, half, coef_out)

    @pl.when((e & 1) == 0)
    def _():
        step(coef_even, coef_odd)

    @pl.when((e & 1) == 1)
    def _():
        step(coef_odd, coef_even)


def peer_dense(xt, u, vt, cb, eb, na, wa, tn=512, te=PEER_TILE):
    d, t = xt.shape
    n_exp = u.shape[0]
    tn = min(tn, t)
    n_e = n_exp // te
    assert vt.shape == (n_e, d, te)
    once = pl.Buffered(1)
    sspec = pl.BlockSpec((PEER_HEADS, PEER_NKEYS, tn), lambda i, e: (0, 0, i), pipeline_mode=once)
    aspec = pl.BlockSpec((PEER_NKEYS, PEER_HEADS, tn), lambda i, e: (0, 0, i), pipeline_mode=once)
    na, wa = na.transpose(1, 0, 2), wa.transpose(1, 0, 2)
    return pl.pallas_call(
        functools.partial(_peer_dense_kernel, te=te),
        grid=(t // tn, n_e + 1),
        in_specs=[pl.BlockSpec((d, tn), lambda i, e: (0, i), pipeline_mode=once),
                  pl.BlockSpec((te, d), lambda i, e: (jnp.minimum(e, n_e - 1), 0)),
                  pl.BlockSpec((None, d, te), lambda i, e: (jnp.maximum(e - 1, 0), 0, 0)),
                  sspec, sspec, aspec, aspec],
        out_specs=pl.BlockSpec((d, tn), lambda i, e: (0, i)),
        out_shape=jax.ShapeDtypeStruct((d, t), F32),
        scratch_shapes=[pltpu.VMEM((te, tn), BF16), pltpu.VMEM((te, tn), BF16)],
        compiler_params=_params("parallel", "arbitrary"),
        name="peer_dense",
    )(xt, u, vt, cb, eb, na, wa)


def kernel(x, p, positions, norm_mix, w_in, qn_moba, kn_moba, qn_diff, kn_diff, lambda_q1, lambda_k1, lambda_q2, lambda_k2, subln_diff, w_branch, w_out, norm_ffn, peer_wq, peer_key_a, peer_key_b, peer_u, peer_v, norm_ple, w_ple_gate, w_ple_proj):
    b, s, d = x.shape
    t = b * s
    depth = w_in.shape[0]
    nb = s // MOBA_BLOCK
    gate_col = 9 * MIX_WIDTH
    h = x.reshape(t, d)
    p_flat = p.reshape(depth, t, -1)
    for i in range(depth):
        n1 = rmsnorm(h, norm_mix[i])
        proj = matmul(n1, w_in, i, out_dtype=BF16)
        qm, qmf, km, qd, kd, kmean = prep_qkv(
            proj, positions, qn_moba[i], kn_moba[i], qn_diff[i], kn_diff[i])
        kmean = kmean.reshape(b, nb, N_HEADS, HEAD_DIM).transpose(0, 2, 1, 3)
        kmean = jnp.pad(kmean, ((0, 0), (0, 0), (0, -nb % 8), (0, 0)))
        o_m = moba_attention(qm, qmf, kmean, km, proj, 2, b, s)
        lam_init = 0.8 - 0.6 * math.exp(-0.3 * i)
        o_d = diff_attention(qd, kd, proj, 5, lambda_q1[i], lambda_k1[i], lambda_q2[i], lambda_k2[i],
                             subln_diff[i], lam_init, b, s)
        o_s = stick_attention(proj, 6, b, s)
        merged = merge_branches(o_m, o_d, o_s, w_branch, i, proj, gate_col)
        h = matmul(merged, w_out, i, mode="residual", res=h)
        n2, n2_t = rmsnorm(h, norm_ffn[i], with_transpose=True)
        q = matmul(n2, peer_wq, i)
        cb, eb, na, wa = peer_scores(q, peer_key_a[i], peer_key_b[i])
        ffn_t = peer_dense(n2_t, to_bf16(peer_u, i), to_bf16(peer_v, i, transpose_tile=PEER_TILE),
                           cb, eb, na, wa, te=PEER_TILE)
        h, n3 = residual_rmsnorm(h, ffn_t, norm_ple[i])
        h = matmul(n3, w_ple_gate, i, mode="ple", res=h, p=p_flat, wp=w_ple_proj)
    return h.reshape(b, s, d)
```

```python
import functools
import math

import jax
import jax.numpy as jnp
from jax import lax
from jax.experimental import pallas as pl
from jax.experimental.pallas import tpu as pltpu

F32 = jnp.float32
BF16 = jnp.bfloat16

LANES = 128
HEAD_DIM = 128
N_HEADS = 8
MIX_WIDTH = N_HEADS * HEAD_DIM
DIFF_SUB = HEAD_DIM // 2
ROPE_THETA = 500000.0
ROT_HALF = HEAD_DIM // 8
ROT_HALF_DIFF = DIFF_SUB // 8
MOBA_BLOCK = 256
MOBA_TOPK = 3
PEER_HEADS = 8
PEER_NKEYS = 128
PEER_TOPK = 16
PEER_TILE = 512
NORM_EPS = 1e-6
NEG_INF = -1e30
V7X_VMEM_BYTES = 64 * 1024 * 1024
VMEM_LIMIT = V7X_VMEM_BYTES * 7 // 8

_NT = (((1,), (1,)), ((), ()))


def _params(*sem):
    return pltpu.CompilerParams(dimension_semantics=sem, vmem_limit_bytes=VMEM_LIMIT)


def _rmsnorm_kernel(x_ref, g_ref, o_ref, *maybe_ot_ref):
    x = x_ref[...]
    ms = jnp.mean(x * x, axis=-1, keepdims=True)
    y = x * lax.rsqrt(ms + NORM_EPS) * g_ref[...]
    o_ref[...] = y.astype(o_ref.dtype)
    for ot_ref in maybe_ot_ref:
        ot_ref[...] = y.T.astype(ot_ref.dtype)


def rmsnorm(x, gain, tm=256, with_transpose=False):
    t, d = x.shape
    tm = min(tm, t)
    out_specs = [pl.BlockSpec((tm, d), lambda i: (i, 0))]
    out_shape = [jax.ShapeDtypeStruct((t, d), BF16)]
    if with_transpose:
        out_specs.append(pl.BlockSpec((d, tm), lambda i: (0, i)))
        out_shape.append(jax.ShapeDtypeStruct((d, t), BF16))
    outs = pl.pallas_call(
        _rmsnorm_kernel,
        grid=(t // tm,),
        in_specs=[pl.BlockSpec((tm, d), lambda i: (i, 0)),
                  pl.BlockSpec((1, d), lambda i: (0, 0))],
        out_specs=out_specs,
        out_shape=out_shape,
        compiler_params=_params("parallel"),
        name="rmsnorm_t" if with_transpose else "rmsnorm",
    )(x, gain.reshape(1, d))
    return outs if with_transpose else outs[0]


def _residual_rmsnorm_kernel(x_ref, yt_ref, g_ref, h_ref, n_ref):
    h = x_ref[...] + yt_ref[...].T
    h_ref[...] = h
    ms = jnp.mean(h * h, axis=-1, keepdims=True)
    n_ref[...] = (h * lax.rsqrt(ms + NORM_EPS) * g_ref[...]).astype(n_ref.dtype)


def residual_rmsnorm(x, y_t, gain, tm=256):
    t, d = x.shape
    tm = min(tm, t)
    row = pl.BlockSpec((tm, d), lambda i: (i, 0))
    return pl.pallas_call(
        _residual_rmsnorm_kernel,
        grid=(t // tm,),
        in_specs=[row, pl.BlockSpec((d, tm), lambda i: (0, i)), pl.BlockSpec((1, d), lambda i: (0, 0))],
        out_specs=[row, row],
        out_shape=[jax.ShapeDtypeStruct((t, d), F32), jax.ShapeDtypeStruct((t, d), BF16)],
        compiler_params=_params("parallel"),
        name="residual_rmsnorm",
    )(x, y_t, gain.reshape(1, d))


def _relayout_kernel(x_ref, o_ref, *, transpose_tile):
    x = x_ref[...]
    if transpose_tile is None:
        o_ref[...] = x.astype(o_ref.dtype)
    else:
        xt = x.T.astype(o_ref.dtype)
        for a in range(o_ref.shape[0]):
            o_ref[a] = xt[:, a * transpose_tile:(a + 1) * transpose_tile]


def to_bf16(x, layer, transpose_tile=None, tb=1024):
    _, r, c = x.shape
    tr, tc = min(tb, r), min(tb, c)
    if transpose_tile is None:
        out_spec, out_dims = pl.BlockSpec((tr, tc), lambda i, j: (i, j)), (r, c)
    else:
        n = min(transpose_tile, tr)
        transpose_tile = n
        out_spec = pl.BlockSpec((tr // n, tc, n), lambda i, j: (i, j, 0))
        out_dims = (r // n, c, n)
    return pl.pallas_call(
        functools.partial(_relayout_kernel, transpose_tile=transpose_tile),
        grid=(r // tr, c // tc),
        in_specs=[pl.BlockSpec((None, tr, tc), lambda i, j: (layer, i, j))],
        out_specs=out_spec,
        out_shape=jax.ShapeDtypeStruct(out_dims, BF16),
        compiler_params=_params("parallel", "parallel"),
        name="to_bf16" if transpose_tile is None else "to_bf16_t",
    )(x)


def _mm_kernel(*refs, mode):
    if mode == "plain":
        a_ref, b_ref, o_ref = refs
    elif mode == "residual":
        a_ref, b_ref, r_ref, o_ref = refs
    else:
        a_ref, b_ref, r_ref, p_ref, wp_ref, o_ref = refs
    acc = jnp.dot(a_ref[...], b_ref[...].astype(BF16), preferred_element_type=F32)
    if mode == "plain":
        o_ref[...] = acc.astype(o_ref.dtype)
    elif mode == "residual":
        o_ref[...] = (r_ref[...] + acc).astype(o_ref.dtype)
    else:
        emb = jnp.dot(p_ref[...].astype(BF16), wp_ref[...].astype(BF16),
                      preferred_element_type=F32)
        o_ref[...] = (r_ref[...] + jax.nn.sigmoid(acc) * emb).astype(o_ref.dtype)


def matmul(a, w, layer, *, mode="plain", res=None, p=None, wp=None, out_dtype=F32, tm=2048, tn=None):
    m, kdim = a.shape
    n = w.shape[2]
    if tn is None:
        tn = 256 if mode == "ple" else 512
    tm, tn = min(tm, m), min(tn, n)
    in_specs = [pl.BlockSpec((tm, kdim), lambda i, j: (i, 0), pipeline_mode=pl.Buffered(1)),
                pl.BlockSpec((None, kdim, tn), lambda i, j: (layer, 0, j))]
    args = [a, w]
    if mode in ("residual", "ple"):
        in_specs.append(pl.BlockSpec((tm, tn), lambda i, j: (i, j)))
        args.append(res)
    if mode == "ple":
        pd = p.shape[2]
        in_specs += [pl.BlockSpec((None, tm, pd), lambda i, j: (layer, i, 0)),
                     pl.BlockSpec((None, pd, tn), lambda i, j: (layer, 0, j))]
        args += [p, wp]
    return pl.pallas_call(
        functools.partial(_mm_kernel, mode=mode),
        grid=(m // tm, n // tn),
        in_specs=in_specs,
        out_specs=pl.BlockSpec((tm, tn), lambda i, j: (i, j)),
        out_shape=jax.ShapeDtypeStruct((m, n), out_dtype),
        compiler_params=_params("parallel", "arbitrary"),
        name="matmul_" + mode,
    )(*args)


def _rope(x, cos, sin, half, period):
    lane = lax.broadcasted_iota(jnp.int32, x.shape, 1)
    first = (lane & (period - 1)) < half
    partner = jnp.where(first, pltpu.roll(x, LANES - half, 1), pltpu.roll(x, half, 1))
    return x * cos + partner * jnp.where(first, -sin, sin)


def _prep_kernel(pos_ref, fm_ref, fd_ref, qng_ref, kng_ref, qdg_ref, kdg_ref,
                 qm_i, km_i, qd_i, kd_i, qm_o, qmf_o, km_o, qd_o, kd_o, kmean_o):
    pos = pos_ref[...].astype(F32)
    ang_m = pos * fm_ref[...]
    ang_d = pos * fd_ref[...]
    cos_m, sin_m = jnp.cos(ang_m), jnp.sin(ang_m)
    cos_d, sin_d = jnp.cos(ang_d), jnp.sin(ang_d)
    lane = lax.broadcasted_iota(jnp.int32, ang_m.shape, 1)
    low = lane < DIFF_SUB

    def norm_full(x, g):
        ms = jnp.mean(x * x, axis=-1, keepdims=True)
        return x * lax.rsqrt(ms + NORM_EPS) * g

    def norm_halves(x, g):
        xx = x * x
        lo = jnp.sum(jnp.where(low, xx, 0.0), axis=-1, keepdims=True)
        hi = jnp.sum(jnp.where(low, 0.0, xx), axis=-1, keepdims=True)
        ms = jnp.where(low, lo, hi) * (1.0 / DIFF_SUB)
        return x * lax.rsqrt(ms + NORM_EPS) * g

    for h in range(N_HEADS):
        sl = slice(h * HEAD_DIM, (h + 1) * HEAD_DIM)
        qm = _rope(norm_full(qm_i[:, sl].astype(F32), qng_ref[...]), cos_m, sin_m, ROT_HALF, HEAD_DIM)
        km = _rope(norm_full(km_i[:, sl].astype(F32), kng_ref[...]), cos_m, sin_m, ROT_HALF, HEAD_DIM)
        qmf_o[:, sl] = qm
        qm_o[:, sl] = qm.astype(BF16)
        km_o[:, sl] = km.astype(BF16)
        kmean_o[0, h:h + 1, :] = jnp.mean(km, axis=0, keepdims=True)
        qd = _rope(norm_halves(qd_i[:, sl].astype(F32), qdg_ref[...]), cos_d, sin_d,
                   ROT_HALF_DIFF, DIFF_SUB)
        kd = _rope(norm_halves(kd_i[:, sl].astype(F32), kdg_ref[...]), cos_d, sin_d,
                   ROT_HALF_DIFF, DIFF_SUB)
        qd_o[:, sl] = qd.astype(BF16)
        kd_o[:, sl] = kd.astype(BF16)


def _rope_table(half, period):
    lane = jnp.arange(LANES)
    inv_freq = ROPE_THETA ** (-jnp.arange(half, dtype=F32) / half)
    tab = jnp.where((lane % period) < 2 * half, inv_freq[lane % half], 0.0)
    return tab.reshape(1, LANES).astype(F32)


def prep_qkv(proj, positions, qn_m, kn_m, qn_d, kn_d):
    t = proj.shape[0]
    tm = MOBA_BLOCK
    w = MIX_WIDTH
    row = lambda i: (i, 0)
    col_spec = [pl.BlockSpec((tm, w), functools.partial(lambda i, c: (i, c), c=c)) for c in (0, 1, 3, 4)]
    vec = pl.BlockSpec((1, LANES), lambda i: (0, 0))
    bf = jax.ShapeDtypeStruct((t, w), BF16)
    out_shape = [bf, jax.ShapeDtypeStruct((t, w), F32), bf, bf, bf,
                 jax.ShapeDtypeStruct((t // tm, N_HEADS, HEAD_DIM), F32)]
    out_specs = [pl.BlockSpec((tm, w), row)] * 5 + [
        pl.BlockSpec((1, N_HEADS, HEAD_DIM), lambda i: (i, 0, 0))]
    return pl.pallas_call(
        _prep_kernel,
        grid=(t // tm,),
        in_specs=[pl.BlockSpec((tm, 1), row), vec, vec, vec, vec, vec, vec] + col_spec,
        out_specs=out_specs,
        out_shape=out_shape,
        compiler_params=_params("parallel"),
        name="prep_qkv",
    )(positions.reshape(t, 1), _rope_table(ROT_HALF, HEAD_DIM), _rope_table(ROT_HALF_DIFF, DIFF_SUB),
      qn_m.reshape(1, LANES), kn_m.reshape(1, LANES),
      jnp.tile(qn_d, 2).reshape(1, LANES), jnp.tile(kn_d, 2).reshape(1, LANES),
      *([proj] * 4))


def _ones_column(rows, at=0):
    return (lax.broadcasted_iota(jnp.int32, (rows, LANES), 1) == at).astype(BF16)


def _value_with_ones(v):
    return jnp.concatenate([v, jnp.ones(v.shape, BF16)], axis=1)


def _moba_kernel(q_ref, qf_ref, kmean_ref, k_ref, v_ref, o_ref, knorm_ref, *, tk):
    blk = MOBA_BLOCK
    tq = q_ref.shape[0]
    qi = pl.program_id(2)
    scale = HEAD_DIM ** -0.5
    big = NEG_INF / scale
    nbp = kmean_ref.shape[0]

    @pl.when(qi == 0)
    def _():
        knorm_ref[...] = jnp.broadcast_to(_max_row_norm(k_ref, 0, HEAD_DIM), knorm_ref.shape)

    gate = lax.dot_general(kmean_ref[...], qf_ref[...], _NT, precision=lax.Precision.HIGHEST,
                           preferred_element_type=F32)
    rid = lax.broadcasted_iota(jnp.int32, gate.shape, 0)
    own = qi * (tq // blk) + lax.broadcasted_iota(jnp.int32, gate.shape, 1) // blk
    rank = jnp.zeros(gate.shape, jnp.int32)
    for m in range(nbp):
        g_m = gate[m:m + 1, :]
        beats = (g_m > gate) | ((g_m == gate) & (rid > m))
        rank = rank + jnp.where(beats & (m < own), 1, 0)
    allowed = ((rid < own) & (rank < MOBA_TOPK)) | (rid == own)
    pen_t = jnp.where(allowed, 0.0, big)
    pen_t = jnp.concatenate([pen_t, jnp.full((LANES - nbp, tq), big, F32)], axis=0)
    penalty = pen_t.T

    q = q_ref[...]
    lane = lax.broadcasted_iota(jnp.int32, (tq, LANES), 1)
    q_pos = qi * tq + lax.broadcasted_iota(jnp.int32, (tq, tk), 0)
    col = lax.broadcasted_iota(jnp.int32, (tq, tk), 1)
    krow = lax.broadcasted_iota(jnp.int32, (tk, LANES), 0)
    klane = lax.broadcasted_iota(jnp.int32, (tk, LANES), 1)
    n_tiles = (qi * tq) // tk + 1

    def scores(n, q_aug, shift_lane, causal):
        off = pl.multiple_of(n * tk, tk)
        kb = k_ref[pl.ds(off, tk), :]
        block_of_row = n * (tk // blk) + krow // blk
        marks = ((klane == block_of_row) | (klane == shift_lane)).astype(BF16)
        s = lax.dot_general(q_aug, jnp.concatenate([kb, marks], axis=1), _NT,
                            preferred_element_type=F32) * scale
        if causal:
            s = jnp.where(off + col <= q_pos, s, NEG_INF)
        return s, off

    q_max = jnp.concatenate([q, penalty.astype(BF16)], axis=1)

    def sweep_max(n, best):
        s, _ = scores(n, q_max, -1, True)
        return jnp.maximum(best, s)

    def exact_row_max():
        best = lax.fori_loop(0, n_tiles, sweep_max, jnp.full((tq, tk), NEG_INF, F32))
        return jnp.max(best, axis=-1, keepdims=True)

    qsq = q.astype(F32)
    bound = jnp.sqrt(jnp.sum(qsq * qsq, axis=-1, keepdims=True)) * knorm_ref[0:1, 0:1] * scale
    shift = lax.cond(2.0 * jnp.max(bound) < SHIFT_SLACK, lambda: bound, exact_row_max)
    shift_lane = LANES - 1
    q_sum = jnp.concatenate(
        [q, jnp.where(lane == shift_lane, -shift / scale, penalty).astype(BF16)], axis=1)

    def tile_sum(n, causal):
        s, off = scores(n, q_sum, shift_lane, causal)
        vb = v_ref[pl.ds(off, tk), :]
        return jnp.dot(jnp.exp(s).astype(BF16), _value_with_ones(vb), preferred_element_type=F32)

    acc = lax.fori_loop(0, n_tiles - 1, lambda n, acc: acc + tile_sum(n, False),
                        jnp.zeros((tq, 2 * HEAD_DIM), F32))
    acc = acc + tile_sum(n_tiles - 1, True)
    o_ref[...] = (acc[:, :HEAD_DIM] / acc[:, HEAD_DIM:]).astype(o_ref.dtype)


def moba_attention(q, qf, kmean, k, v, v_group, batch, seq, tq=1024, tk=1024):
    t = q.shape[0]
    tq, tk = min(tq, seq), min(tk, seq)
    assert tq % MOBA_BLOCK == 0 and tk % tq == 0
    nq = seq // tq
    qspec = pl.BlockSpec((tq, HEAD_DIM), lambda b, h, i: (b * nq + i, h))
    kspec = pl.BlockSpec((seq, HEAD_DIM), lambda b, h, i: (b, h))
    vspec = pl.BlockSpec((seq, HEAD_DIM), lambda b, h, i: (b, v_group * N_HEADS + h))
    return pl.pallas_call(
        functools.partial(_moba_kernel, tk=tk),
        grid=(batch, N_HEADS, nq),
        in_specs=[qspec, qspec,
                  pl.BlockSpec((None, None, kmean.shape[2], HEAD_DIM), lambda b, h, i: (b, h, 0, 0)),
                  kspec, vspec],
        out_specs=qspec,
        out_shape=jax.ShapeDtypeStruct((t, MIX_WIDTH), BF16),
        scratch_shapes=[pltpu.VMEM((8, LANES), F32)],
        compiler_params=_params("parallel", "parallel", "arbitrary"),
        name="moba_attention",
    )(q, qf, kmean, k, v)


SHIFT_SLACK = 80.0


def _max_row_norm(x_ref, lane_lo, lane_hi, chunk=512):
    n_rows = x_ref.shape[0]
    chunk = min(chunk, n_rows)
    lane = lax.broadcasted_iota(jnp.int32, (chunk, x_ref.shape[1]), 1)
    keep = (lane >= lane_lo) & (lane < lane_hi)

    def body(c, best):
        x = x_ref[pl.ds(pl.multiple_of(c * chunk, chunk), chunk), :].astype(F32)
        sq = jnp.sum(jnp.where(keep, x * x, 0.0), axis=-1, keepdims=True)
        return jnp.maximum(best, jnp.max(sq, axis=0, keepdims=True))

    return jnp.sqrt(lax.fori_loop(0, n_rows // chunk, body, jnp.zeros((1, 1), F32)))


def _diff_kernel(lq1_ref, lk1_ref, lq2_ref, lk2_ref, g_ref, q_ref, k_ref, v_ref, o_ref, knorm_ref,
                 *, tq, tk, lam_init):
    qi = pl.program_id(2)

    @pl.when(qi == 0)
    def _():
        knorm_ref[0] = jnp.broadcast_to(_max_row_norm(k_ref, 0, DIFF_SUB), knorm_ref.shape[1:])
        knorm_ref[1] = jnp.broadcast_to(_max_row_norm(k_ref, DIFF_SUB, HEAD_DIM), knorm_ref.shape[1:])

    q = q_ref[...] * (DIFF_SUB ** -0.5)
    lane = lax.broadcasted_iota(jnp.int32, q.shape, 1)
    zero = jnp.zeros_like(q)
    qq = jnp.concatenate([jnp.where(lane < DIFF_SUB, q, zero),
                          jnp.where(lane < DIFF_SUB, zero, q)], axis=0)
    rows = 2 * tq
    q_pos = qi * tq + (lax.broadcasted_iota(jnp.int32, (rows, tk), 0) & (tq - 1))
    col = lax.broadcasted_iota(jnp.int32, (rows, tk), 1)
    n_full = (qi * tq) // tk
    last = pl.multiple_of(n_full * tk, tk)
    causal = last + col <= q_pos

    def lane_tile_max(s):
        out = s[:, :LANES]
        for c in range(LANES, tk, LANES):
            out = jnp.maximum(out, s[:, c:c + LANES])
        return out

    def sweep_max(n, best):
        kb = k_ref[pl.ds(pl.multiple_of(n * tk, tk), tk), :]
        return jnp.maximum(best, lane_tile_max(lax.dot_general(qq, kb, _NT, preferred_element_type=F32)))

    def exact_row_max():
        best = lax.fori_loop(0, n_full, sweep_max, jnp.full((rows, LANES), NEG_INF, F32))
        s_last = lax.dot_general(qq, k_ref[pl.ds(last, tk), :], _NT, preferred_element_type=F32)
        best = jnp.maximum(best, lane_tile_max(jnp.where(causal, s_last, NEG_INF)))
        return jnp.max(best, axis=-1, keepdims=True)

    qf = qq.astype(F32)
    q_norm = jnp.sqrt(jnp.sum(qf * qf, axis=-1, keepdims=True))
    first_half = lax.broadcasted_iota(jnp.int32, (rows, 1), 0) < tq
    bound = q_norm * jnp.where(first_half, knorm_ref[0][0:1, 0:1], knorm_ref[1][0:1, 0:1])
    shift = lax.cond(2.0 * jnp.max(bound) < SHIFT_SLACK, lambda: bound, exact_row_max)

    lane_r = lax.broadcasted_iota(jnp.int32, (rows, LANES), 1)
    q_aug = jnp.concatenate([qq, jnp.where(lane_r == 0, -shift, 0.0).astype(BF16)], axis=1)
    ones_col = _ones_column(tk)

    def tile_sum(off, mask):
        k_aug = jnp.concatenate([k_ref[pl.ds(off, tk), :], ones_col], axis=1)
        v_aug = _value_with_ones(v_ref[pl.ds(off, tk), :])
        outs = []
        for half in range(2):
            s = lax.dot_general(q_aug[half * tq:(half + 1) * tq], k_aug, _NT,
                                preferred_element_type=F32)
            if mask is not None:
                s = jnp.where(mask[half * tq:(half + 1) * tq], s, NEG_INF)
            outs.append(jnp.dot(jnp.exp(s).astype(BF16), v_aug, preferred_element_type=F32))
        return jnp.concatenate(outs, axis=0)

    def sweep_sum(n, acc):
        return acc + tile_sum(pl.multiple_of(n * tk, tk), None)

    acc = lax.fori_loop(0, n_full, sweep_sum, jnp.zeros((rows, 2 * HEAD_DIM), F32))
    acc = acc + tile_sum(last, causal)

    lam = (jnp.exp(jnp.sum(lq1_ref[...] * lk1_ref[...], axis=-1, keepdims=True))
           - jnp.exp(jnp.sum(lq2_ref[...] * lk2_ref[...], axis=-1, keepdims=True)) + lam_init)
    out = acc[:, :HEAD_DIM] / acc[:, HEAD_DIM:]
    o = out[:tq] - lam * out[tq:]
    ms = jnp.mean(o * o, axis=-1, keepdims=True)
    o_ref[...] = (o * lax.rsqrt(ms + NORM_EPS) * g_ref[...] * (1.0 - lam_init)).astype(o_ref.dtype)


def diff_attention(q, k, v, v_group, lq1, lk1, lq2, lk2, subln, lam_init, batch, seq, tq=1024, tk=1024):
    t = q.shape[0]
    tq, tk = min(tq, seq), min(tk, seq)
    nq = seq // tq
    qspec = pl.BlockSpec((tq, HEAD_DIM), lambda b, h, i: (b * nq + i, h))
    kspec = pl.BlockSpec((seq, HEAD_DIM), lambda b, h, i: (b, h))
    vspec = pl.BlockSpec((seq, HEAD_DIM), lambda b, h, i: (b, v_group * N_HEADS + h))
    lspec = pl.BlockSpec((1, DIFF_SUB), lambda b, h, i: (0, 0))
    return pl.pallas_call(
        functools.partial(_diff_kernel, tq=tq, tk=tk, lam_init=lam_init),
        grid=(batch, N_HEADS, nq),
        in_specs=[lspec, lspec, lspec, lspec,
                  pl.BlockSpec((1, HEAD_DIM), lambda b, h, i: (0, 0)), qspec, kspec, vspec],
        out_specs=qspec,
        out_shape=jax.ShapeDtypeStruct((t, MIX_WIDTH), BF16),
        scratch_shapes=[pltpu.VMEM((2, 8, LANES), F32)],
        compiler_params=_params("parallel", "parallel", "arbitrary"),
        name="diff_attention",
    )(lq1.reshape(1, DIFF_SUB), lk1.reshape(1, DIFF_SUB), lq2.reshape(1, DIFF_SUB),
      lk2.reshape(1, DIFF_SUB), subln.reshape(1, HEAD_DIM), q, k, v)


def _stick_kernel(q_ref, k_ref, v_ref, o_ref, *, tq, tk):
    qi = pl.program_id(2)
    scale = HEAD_DIM ** -0.5
    n_heads = q_ref.shape[1] // HEAD_DIM
    upper = (lax.broadcasted_iota(jnp.int32, (tk, tk), 0)
             > lax.broadcasted_iota(jnp.int32, (tk, tk), 1)).astype(BF16)
    q_pos = qi * tq + lax.broadcasted_iota(jnp.int32, (tq, tk), 0)
    col = lax.broadcasted_iota(jnp.int32, (tq, tk), 1)
    n_diag = tq // tk
    n_past = qi * n_diag

    def tile(n, hd, later, acc, masked):
        off = pl.multiple_of(n * tk, tk)
        cols = slice(hd * HEAD_DIM, (hd + 1) * HEAD_DIM)
        kb = k_ref[pl.ds(off, tk), cols]
        vb = v_ref[pl.ds(off, tk), cols]
        z = lax.dot_general(q_ref[:, cols], kb, _NT, preferred_element_type=F32) * scale
        sp = jnp.maximum(z, 0.0) + jnp.log(1.0 + jnp.exp(-jnp.abs(z)))
        if masked:
            strict = (off + col) < q_pos
            sp = jnp.where(strict, sp, 0.0)
        hi = sp.astype(BF16)
        lo = (sp - hi.astype(F32)).astype(BF16)
        between = (jnp.dot(hi, upper, preferred_element_type=F32)
                   + jnp.dot(lo, upper, preferred_element_type=F32))
        a = jnp.exp(z - sp - between - later)
        if masked:
            a = jnp.where(strict, a, 0.0)
        acc = acc + jnp.dot(a.astype(BF16), vb, preferred_element_type=F32)
        return later + jnp.sum(sp, axis=-1, keepdims=True), acc

    state = [(jnp.zeros((tq, 1), F32), jnp.zeros((tq, HEAD_DIM), F32)) for _ in range(n_heads)]
    for d in range(n_diag):
        state = [tile(n_past + n_diag - 1 - d, hd, *state[hd], True) for hd in range(n_heads)]

    def body(it, carry):
        return tuple(tile(n_past - 1 - it, hd, *carry[hd], False) for hd in range(n_heads))

    state = lax.fori_loop(0, n_past, body, tuple(state))
    for hd in range(n_heads):
        o_ref[:, hd * HEAD_DIM:(hd + 1) * HEAD_DIM] = state[hd][1].astype(o_ref.dtype)


def stick_attention(qkv, q_group, batch, seq, tq=1024, tk=256, heads_per_step=2):
    t = qkv.shape[0]
    tq, tk = min(tq, seq), min(tk, seq)
    nq = seq // tq
    width = heads_per_step * HEAD_DIM
    per_group = MIX_WIDTH // width

    def col(group):
        return lambda b, h, i: (b, group * per_group + h)

    qspec = pl.BlockSpec((tq, width), lambda b, h, i: (b * nq + i, q_group * per_group + h))
    kspec = pl.BlockSpec((seq, width), col(q_group + 1))
    vspec = pl.BlockSpec((seq, width), col(q_group + 2))
    return pl.pallas_call(
        functools.partial(_stick_kernel, tq=tq, tk=tk),
        grid=(batch, N_HEADS // heads_per_step, nq),
        in_specs=[qspec, kspec, vspec],
        out_specs=pl.BlockSpec((tq, width), lambda b, h, i: (b * nq + i, h)),
        out_shape=jax.ShapeDtypeStruct((t, MIX_WIDTH), BF16),
        compiler_params=_params("parallel", "parallel", "arbitrary"),
        name="stick_attention",
    )(qkv, qkv, qkv)


def _merge_kernel(om_ref, od_ref, os_ref, w_ref, gm_ref, gd_ref, gs_ref, o_ref):
    acc = None
    for k, (g_ref, x_ref) in enumerate(((gm_ref, om_ref), (gd_ref, od_ref), (gs_ref, os_ref))):
        term = jax.nn.sigmoid(g_ref[...].astype(F32)) * jnp.dot(
            x_ref[...], w_ref[k].astype(BF16), preferred_element_type=F32)
        acc = term if acc is None else acc + term
    o_ref[...] = acc.astype(o_ref.dtype)


def merge_branches(o_m, o_d, o_s, w_branch, layer, proj, gate_col, tm=1024, tn=512):
    t, kdim = o_m.shape
    d = w_branch.shape[3]
    tm, tn = min(tm, t), min(tn, d)
    ospec = pl.BlockSpec((tm, kdim), lambda i, j: (i, 0))
    gspecs = [pl.BlockSpec((tm, tn), functools.partial(lambda i, j, c: (i, c + j), c=(gate_col + c * d) // tn))
              for c in range(3)]
    return pl.pallas_call(
        _merge_kernel,
        grid=(t // tm, d // tn),
        in_specs=[ospec, ospec, ospec,
                  pl.BlockSpec((None, 3, kdim, tn), lambda i, j: (layer, 0, 0, j))] + gspecs,
        out_specs=pl.BlockSpec((tm, tn), lambda i, j: (i, j)),
        out_shape=jax.ShapeDtypeStruct((t, d), BF16),
        compiler_params=_params("parallel", "parallel"),
        name="merge_branches",
    )(o_m, o_d, o_s, w_branch, proj, proj, proj)


def _extract_top(x, rounds, break_ties):
    rows, n = x.shape
    rid = lax.broadcasted_iota(jnp.int32, (rows, n), 0)
    kid = lax.broadcasted_iota(jnp.int32, (rounds, n), 0)

    def body(k, carry):
        x, rank, vals = carry
        best = jnp.max(x, axis=0, keepdims=True)
        hit = x == best
        if break_ties:
            hit = rid == jnp.min(jnp.where(hit, rid, rows), axis=0, keepdims=True)
        rank = jnp.where(hit, k, rank)
        x = jnp.where(hit, -jnp.inf, x)
        vals = jnp.where(kid == k, best, vals)
        return x, rank, vals

    init = (x, jnp.full((rows, n), rounds, jnp.int32), jnp.zeros((rounds, n), F32))
    _, rank, vals = lax.fori_loop(0, rounds, body, init)
    return vals, rank


def _peer_score_kernel(q_ref, keys_ref, cb_o, eb_o, na_o, wa_o):
    nk, topk = PEER_NKEYS, PEER_TOPK
    s = lax.dot_general(keys_ref[...], q_ref[...], _NT, precision=lax.Precision.HIGHEST,
                        preferred_element_type=F32)

    def taken_off(rank):
        taken = jnp.sum((rank < topk).astype(jnp.int32), axis=0, keepdims=True)
        return jnp.max(jnp.where(taken != topk, 1, 0))

    suspect = _peer_select(s, cb_o, eb_o, na_o, wa_o, False, taken_off)

    @pl.when(suspect > 0)
    def _():
        _peer_select(s, cb_o, eb_o, na_o, wa_o, True, taken_off)


def _peer_select(s, cb_o, eb_o, na_o, wa_o, break_ties, taken_off):
    nk, topk = PEER_NKEYS, PEER_TOPK
    sa, sb = s[:nk], s[nk:]
    va, rank_a = _extract_top(sa, topk, break_ties)
    vb, rank_b = _extract_top(sb, topk, break_ties)
    n = s.shape[1]
    row8 = lax.broadcasted_iota(jnp.int32, (8, n), 0)
    ninf = -jnp.inf
    a = [va[r:r + 1] for r in range(topk)]
    groups = [a[0] + vb]
    for r, lim in ((1, 8), (2, 5), (3, 4), (4, 3)):
        groups.append(jnp.where(row8 < lim, a[r] + vb[0:8], ninf))
    a_mix = jnp.where(row8 < 2, a[5], jnp.where(row8 < 4, a[6], jnp.where(
        row8 < 6, a[7], jnp.where(row8 == 6, a[8], a[9]))))
    b_mix = jnp.where((row8 < 6) & ((row8 & 1) == 1), vb[1:2], vb[0:1])
    groups.append(a_mix + b_mix)
    tail = jnp.full((8, n), ninf, F32)
    for k in range(6):
        tail = jnp.where(row8 == k, a[10 + k] + vb[0:1], tail)
    groups.append(tail)
    cand = jnp.concatenate(groups, axis=0)
    top_s, rank_c = _extract_top(cand, topk, break_ties)
    taken = (rank_c < topk).astype(jnp.int32)
    counts = [jnp.sum(taken[0:16], axis=0, keepdims=True)]
    counts += [jnp.sum(taken[8 + 8 * r:16 + 8 * r], axis=0, keepdims=True) for r in range(1, 5)]
    counts += [taken[48 + 2 * k:49 + 2 * k] + taken[49 + 2 * k:50 + 2 * k] for k in range(3)]
    counts += [taken[54 + k:55 + k] for k in range(8)]
    count = jnp.concatenate(counts, axis=0)
    z = jnp.sum(jnp.exp(top_s - top_s[0:1]), axis=0, keepdims=True)
    na = jnp.zeros(sa.shape, jnp.int32)
    for r in range(topk):
        na = jnp.where(rank_a == r, count[r:r + 1], na)
    cb_o[...] = rank_b.astype(F32)
    eb_o[...] = jnp.exp(sb - vb[0:1])
    na_o[...] = na.astype(F32)
    wa_o[...] = jnp.exp(sa - va[0:1]) / z
    return taken_off(rank_a) + taken_off(rank_b) + taken_off(rank_c)


def peer_scores(q, key_a, key_b, tn=512):
    t = q.shape[0]
    tn = min(tn, t)
    zeros = jnp.zeros_like(key_a)
    keys = jnp.concatenate([jnp.concatenate([key_a, zeros], axis=-1),
                            jnp.concatenate([zeros, key_b], axis=-1)], axis=1)
    out = jax.ShapeDtypeStruct((PEER_HEADS, PEER_NKEYS, t), F32)
    ospec = pl.BlockSpec((None, PEER_NKEYS, tn), lambda i, h: (h, 0, i))
    return pl.pallas_call(
        _peer_score_kernel,
        grid=(t // tn, PEER_HEADS),
        in_specs=[pl.BlockSpec((tn, LANES), lambda i, h: (i, h)),
                  pl.BlockSpec((None, 2 * PEER_NKEYS, LANES), lambda i, h: (h, 0, 0))],
        out_specs=[ospec] * 4,
        out_shape=[out] * 4,
        compiler_params=_params("parallel", "parallel"),
        name="peer_scores",
    )(q, keys)


def _peer_dense_kernel(xt_ref, u_ref, vt_ref, cb_ref, eb_ref, na_ref, wa_ref, o_ref,
                       coef_even, coef_odd, *, te):
    e = pl.program_id(1)
    last = pl.num_programs(1) - 2
    tn = o_ref.shape[1]
    half = tn // 2

    @pl.when(e == 0)
    def _():
        o_ref[...] = jnp.zeros_like(o_ref)
        coef_odd[...] = jnp.zeros_like(coef_odd)

    def gate_into_coef(hid, lane0, coef_out):
        chunk = 16
        for di in range(te // PEER_NKEYS):
            i = jnp.minimum(e, last) * (te // PEER_NKEYS) + di
            for sub in range(0, hid.shape[1], LANES):
                lanes = slice(lane0 + sub, lane0 + sub + LANES)
                n_rows = [na_ref[i, h:h + 1, lanes] for h in range(PEER_HEADS)]
                w_rows = [wa_ref[i, h:h + 1, lanes] for h in range(PEER_HEADS)]
                for r in range(0, PEER_NKEYS, chunk):
                    rs = slice(r, r + chunk)
                    w = None
                    for h in range(PEER_HEADS):
                        term = jnp.where(cb_ref[h, rs, lanes] < n_rows[h],
                                         eb_ref[h, rs, lanes] * w_rows[h], 0.0)
                        w = term if w is None else w + term
                    row0 = di * PEER_NKEYS + r
                    x = hid[row0:row0 + chunk, sub:sub + LANES]
                    act = 0.5 * x * (1.0 + lax.erf(x * (2.0 ** -0.5)))
                    coef_out[row0:row0 + chunk, lanes] = (act * w).astype(BF16)

    def step(coef_out, coef_in):
        lo, hi = slice(0, half), slice(half, tn)
        hid_lo = jnp.dot(u_ref[...], xt_ref[:, lo], preferred_element_type=F32)
        o_ref[:, lo] += jnp.dot(vt_ref[...], coef_in[:, lo], preferred_element_type=F32)
        hid_hi = jnp.dot(u_ref[...], xt_ref[:, hi], preferred_element_type=F32)
        gate_into_coef(hid_lo, 0, coef_out)
        o_ref[:, hi] += jnp.dot(vt_ref[...], coef_in[:, hi], preferred_element_type=F32)
        gate_into_coef(hid_hi, half, coef_out)

    @pl.when((e & 1) == 0)
    def _():
        step(coef_even, coef_odd)

    @pl.when((e & 1) == 1)
    def _():
        step(coef_odd, coef_even)


def peer_dense(xt, u, vt, cb, eb, na, wa, tn=512, te=PEER_TILE):
    d, t = xt.shape
    n_exp = u.shape[0]
    tn = min(tn, t)
    n_e = n_exp // te
    assert vt.shape == (n_e, d, te)
    once = pl.Buffered(1)
    sspec = pl.BlockSpec((PEER_HEADS, PEER_NKEYS, tn), lambda i, e: (0, 0, i), pipeline_mode=once)
    aspec = pl.BlockSpec((PEER_NKEYS, PEER_HEADS, tn), lambda i, e: (0, 0, i), pipeline_mode=once)
    na, wa = na.transpose(1, 0, 2), wa.transpose(1, 0, 2)
    return pl.pallas_call(
        functools.partial(_peer_dense_kernel, te=te),
        grid=(t // tn, n_e + 1),
        in_specs=[pl.BlockSpec((d, tn), lambda i, e: (0, i), pipeline_mode=once),
                  pl.BlockSpec((te, d), lambda i, e: (jnp.minimum(e, n_e - 1), 0)),
                  pl.BlockSpec((None, d, te), lambda i, e: (jnp.maximum(e - 1, 0), 0, 0)),
                  sspec, sspec, aspec, aspec],
        out_specs=pl.BlockSpec((d, tn), lambda i, e: (0, i)),
        out_shape=jax.ShapeDtypeStruct((d, t), F32),
        scratch_shapes=[pltpu.VMEM((te, tn), BF16), pltpu.VMEM((te, tn), BF16)],
        compiler_params=_params("parallel", "arbitrary"),
        name="peer_dense",
    )(xt, u, vt, cb, eb, na, wa)


def kernel(x, p, positions, norm_mix, w_in, qn_moba, kn_moba, qn_diff, kn_diff, lambda_q1, lambda_k1, lambda_q2, lambda_k2, subln_diff, w_branch, w_out, norm_ffn, peer_wq, peer_key_a, peer_key_b, peer_u, peer_v, norm_ple, w_ple_gate, w_ple_proj):
    b, s, d = x.shape
    t = b * s
    depth = w_in.shape[0]
    nb = s // MOBA_BLOCK
    gate_col = 9 * MIX_WIDTH
    h = x.reshape(t, d)
    p_flat = p.reshape(depth, t, -1)
    for i in range(depth):
        n1 = rmsnorm(h, norm_mix[i])
        proj = matmul(n1, w_in, i, out_dtype=BF16)
        qm, qmf, km, qd, kd, kmean = prep_qkv(
            proj, positions, qn_moba[i], kn_moba[i], qn_diff[i], kn_diff[i])
        kmean = kmean.reshape(b, nb, N_HEADS, HEAD_DIM).transpose(0, 2, 1, 3)
        kmean = jnp.pad(kmean, ((0, 0), (0, 0), (0, -nb % 8), (0, 0)))
        o_m = moba_attention(qm, qmf, kmean, km, proj, 2, b, s)
        lam_init = 0.8 - 0.6 * math.exp(-0.3 * i)
        o_d = diff_attention(qd, kd, proj, 5, lambda_q1[i], lambda_k1[i], lambda_q2[i], lambda_k2[i],
                             subln_diff[i], lam_init, b, s)
        o_s = stick_attention(proj, 6, b, s)
        merged = merge_branches(o_m, o_d, o_s, w_branch, i, proj, gate_col)
        h = matmul(merged, w_out, i, mode="residual", res=h)
        n2, n2_t = rmsnorm(h, norm_ffn[i], with_transpose=True)
        q = matmul(n2, peer_wq, i)
        cb, eb, na, wa = peer_scores(q, peer_key_a[i], peer_key_b[i])
        ffn_t = peer_dense(n2_t, to_bf16(peer_u, i), to_bf16(peer_v, i, transpose_tile=PEER_TILE),
                           cb, eb, na, wa, te=PEER_TILE)
        h, n3 = residual_rmsnorm(h, ffn_t, norm_ple[i])
        h = matmul(n3, w_ple_gate, i, mode="ple", res=h, p=p_flat, wp=w_ple_proj)
    return h.reshape(b, s, d)
```

```python
import functools
import math

import jax
import jax.numpy as jnp
from jax import lax
from jax.experimental import pallas as pl
from jax.experimental.pallas import tpu as pltpu

F32 = jnp.float32
BF16 = jnp.bfloat16

LANES = 128
HEAD_DIM = 128
N_HEADS = 8
MIX_WIDTH = N_HEADS * HEAD_DIM
DIFF_SUB = HEAD_DIM // 2
ROPE_THETA = 500000.0
ROT_HALF = HEAD_DIM // 8
ROT_HALF_DIFF = DIFF_SUB // 8
MOBA_BLOCK = 256
MOBA_TOPK = 3
PEER_HEADS = 8
PEER_NKEYS = 128
PEER_TOPK = 16
PEER_TILE = 512
NORM_EPS = 1e-6
NEG_INF = -1e30
V7X_VMEM_BYTES = 64 * 1024 * 1024
VMEM_LIMIT = V7X_VMEM_BYTES * 7 // 8

_NT = (((1,), (1,)), ((), ()))


def _params(*sem):
    return pltpu.CompilerParams(dimension_semantics=sem, vmem_limit_bytes=VMEM_LIMIT)


def _rmsnorm_kernel(x_ref, g_ref, o_ref, *maybe_ot_ref):
    x = x_ref[...]
    ms = jnp.mean(x * x, axis=-1, keepdims=True)
    y = x * lax.rsqrt(ms + NORM_EPS) * g_ref[...]
    o_ref[...] = y.astype(o_ref.dtype)
    for ot_ref in maybe_ot_ref:
        ot_ref[...] = y.T.astype(ot_ref.dtype)


def rmsnorm(x, gain, tm=256, with_transpose=False):
    t, d = x.shape
    tm = min(tm, t)
    out_specs = [pl.BlockSpec((tm, d), lambda i: (i, 0))]
    out_shape = [jax.ShapeDtypeStruct((t, d), BF16)]
    if with_transpose:
        out_specs.append(pl.BlockSpec((d, tm), lambda i: (0, i)))
        out_shape.append(jax.ShapeDtypeStruct((d, t), BF16))
    outs = pl.pallas_call(
        _rmsnorm_kernel,
        grid=(t // tm,),
        in_specs=[pl.BlockSpec((tm, d), lambda i: (i, 0)),
                  pl.BlockSpec((1, d), lambda i: (0, 0))],
        out_specs=out_specs,
        out_shape=out_shape,
        compiler_params=_params("parallel"),
        name="rmsnorm_t" if with_transpose else "rmsnorm",
    )(x, gain.reshape(1, d))
    return outs if with_transpose else outs[0]


def _residual_rmsnorm_kernel(x_ref, yt_ref, g_ref, h_ref, n_ref):
    h = x_ref[...] + yt_ref[...].T
    h_ref[...] = h
    ms = jnp.mean(h * h, axis=-1, keepdims=True)
    n_ref[...] = (h * lax.rsqrt(ms + NORM_EPS) * g_ref[...]).astype(n_ref.dtype)


def residual_rmsnorm(x, y_t, gain, tm=256):
    t, d = x.shape
    tm = min(tm, t)
    row = pl.BlockSpec((tm, d), lambda i: (i, 0))
    return pl.pallas_call(
        _residual_rmsnorm_kernel,
        grid=(t // tm,),
        in_specs=[row, pl.BlockSpec((d, tm), lambda i: (0, i)), pl.BlockSpec((1, d), lambda i: (0, 0))],
        out_specs=[row, row],
        out_shape=[jax.ShapeDtypeStruct((t, d), F32), jax.ShapeDtypeStruct((t, d), BF16)],
        compiler_params=_params("parallel"),
        name="residual_rmsnorm",
    )(x, y_t, gain.reshape(1, d))


def _relayout_kernel(x_ref, o_ref, *, transpose_tile):
    x = x_ref[...]
    if transpose_tile is None:
        o_ref[...] = x.astype(o_ref.dtype)
    else:
        xt = x.T.astype(o_ref.dtype)
        for a in range(o_ref.shape[0]):
            o_ref[a] = xt[:, a * transpose_tile:(a + 1) * transpose_tile]


def to_bf16(x, layer, transpose_tile=None, tb=1024):
    _, r, c = x.shape
    tr, tc = min(tb, r), min(tb, c)
    if transpose_tile is None:
        out_spec, out_dims = pl.BlockSpec((tr, tc), lambda i, j: (i, j)), (r, c)
    else:
        n = min(transpose_tile, tr)
        transpose_tile = n
        out_spec = pl.BlockSpec((tr // n, tc, n), lambda i, j: (i, j, 0))
        out_dims = (r // n, c, n)
    return pl.pallas_call(
        functools.partial(_relayout_kernel, transpose_tile=transpose_tile),
        grid=(r // tr, c // tc),
        in_specs=[pl.BlockSpec((None, tr, tc), lambda i, j: (layer, i, j))],
        out_specs=out_spec,
        out_shape=jax.ShapeDtypeStruct(out_dims, BF16),
        compiler_params=_params("parallel", "parallel"),
        name="to_bf16" if transpose_tile is None else "to_bf16_t",
    )(x)


def _mm_kernel(*refs, mode):
    if mode == "plain":
        a_ref, b_ref, o_ref = refs
    elif mode == "residual":
        a_ref, b_ref, r_ref, o_ref = refs
    else:
        a_ref, b_ref, r_ref, p_ref, wp_ref, o_ref = refs
    acc = jnp.dot(a_ref[...], b_ref[...].astype(BF16), preferred_element_type=F32)
    if mode == "plain":
        o_ref[...] = acc.astype(o_ref.dtype)
    elif mode == "residual":
        o_ref[...] = (r_ref[...] + acc).astype(o_ref.dtype)
    else:
        emb = jnp.dot(p_ref[...].astype(BF16), wp_ref[...].astype(BF16),
                      preferred_element_type=F32)
        o_ref[...] = (r_ref[...] + jax.nn.sigmoid(acc) * emb).astype(o_ref.dtype)


def matmul(a, w, layer, *, mode="plain", res=None, p=None, wp=None, out_dtype=F32, tm=2048, tn=None):
    m, kdim = a.shape
    n = w.shape[2]
    if tn is None:
        tn = 256 if mode == "ple" else 512
    tm, tn = min(tm, m), min(tn, n)
    in_specs = [pl.BlockSpec((tm, kdim), lambda i, j: (i, 0), pipeline_mode=pl.Buffered(1)),
                pl.BlockSpec((None, kdim, tn), lambda i, j: (layer, 0, j))]
    args = [a, w]
    if mode in ("residual", "ple"):
        in_specs.append(pl.BlockSpec((tm, tn), lambda i, j: (i, j)))
        args.append(res)
    if mode == "ple":
        pd = p.shape[2]
        in_specs += [pl.BlockSpec((None, tm, pd), lambda i, j: (layer, i, 0)),
                     pl.BlockSpec((None, pd, tn), lambda i, j: (layer, 0, j))]
        args += [p, wp]
    return pl.pallas_call(
        functools.partial(_mm_kernel, mode=mode),
        grid=(m // tm, n // tn),
        in_specs=in_specs,
        out_specs=pl.BlockSpec((tm, tn), lambda i, j: (i, j)),
        out_shape=jax.ShapeDtypeStruct((m, n), out_dtype),
        compiler_params=_params("parallel", "arbitrary"),
        name="matmul_" + mode,
    )(*args)


def _rope(x, cos, sin, half, period):
    lane = lax.broadcasted_iota(jnp.int32, x.shape, 1)
    first = (lane & (period - 1)) < half
    partner = jnp.where(first, pltpu.roll(x, LANES - half, 1), pltpu.roll(x, half, 1))
    return x * cos + partner * jnp.where(first, -sin, sin)


def _prep_kernel(pos_ref, fm_ref, fd_ref, qng_ref, kng_ref, qdg_ref, kdg_ref,
                 qm_i, km_i, qd_i, kd_i, qm_o, qmf_o, km_o, qd_o, kd_o, kmean_o):
    pos = pos_ref[...].astype(F32)
    ang_m = pos * fm_ref[...]
    ang_d = pos * fd_ref[...]
    cos_m, sin_m = jnp.cos(ang_m), jnp.sin(ang_m)
    cos_d, sin_d = jnp.cos(ang_d), jnp.sin(ang_d)
    lane = lax.broadcasted_iota(jnp.int32, ang_m.shape, 1)
    low = lane < DIFF_SUB

    def norm_full(x, g):
        ms = jnp.mean(x * x, axis=-1, keepdims=True)
        return x * lax.rsqrt(ms + NORM_EPS) * g

    def norm_halves(x, g):
        xx = x * x
        lo = jnp.sum(jnp.where(low, xx, 0.0), axis=-1, keepdims=True)
        hi = jnp.sum(jnp.where(low, 0.0, xx), axis=-1, keepdims=True)
        ms = jnp.where(low, lo, hi) * (1.0 / DIFF_SUB)
        return x * lax.rsqrt(ms + NORM_EPS) * g

    for h in range(N_HEADS):
        sl = slice(h * HEAD_DIM, (h + 1) * HEAD_DIM)
        qm = _rope(norm_full(qm_i[:, sl].astype(F32), qng_ref[...]), cos_m, sin_m, ROT_HALF, HEAD_DIM)
        km = _rope(norm_full(km_i[:, sl].astype(F32), kng_ref[...]), cos_m, sin_m, ROT_HALF, HEAD_DIM)
        qmf_o[:, sl] = qm
        qm_o[:, sl] = qm.astype(BF16)
        km_o[:, sl] = km.astype(BF16)
        kmean_o[0, h:h + 1, :] = jnp.mean(km, axis=0, keepdims=True)
        qd = _rope(norm_halves(qd_i[:, sl].astype(F32), qdg_ref[...]), cos_d, sin_d,
                   ROT_HALF_DIFF, DIFF_SUB)
        kd = _rope(norm_halves(kd_i[:, sl].astype(F32), kdg_ref[...]), cos_d, sin_d,
                   ROT_HALF_DIFF, DIFF_SUB)
        qd_o[:, sl] = qd.astype(BF16)
        kd_o[:, sl] = kd.astype(BF16)


def _rope_table(half, period):
    lane = jnp.arange(LANES)
    inv_freq = ROPE_THETA ** (-jnp.arange(half, dtype=F32) / half)
    tab = jnp.where((lane % period) < 2 * half, inv_freq[lane % half], 0.0)
    return tab.reshape(1, LANES).astype(F32)


def prep_qkv(proj, positions, qn_m, kn_m, qn_d, kn_d):
    t = proj.shape[0]
    tm = MOBA_BLOCK
    w = MIX_WIDTH
    row = lambda i: (i, 0)
    col_spec = [pl.BlockSpec((tm, w), functools.partial(lambda i, c: (i, c), c=c)) for c in (0, 1, 3, 4)]
    vec = pl.BlockSpec((1, LANES), lambda i: (0, 0))
    bf = jax.ShapeDtypeStruct((t, w), BF16)
    out_shape = [bf, jax.ShapeDtypeStruct((t, w), F32), bf, bf, bf,
                 jax.ShapeDtypeStruct((t // tm, N_HEADS, HEAD_DIM), F32)]
    out_specs = [pl.BlockSpec((tm, w), row)] * 5 + [
        pl.BlockSpec((1, N_HEADS, HEAD_DIM), lambda i: (i, 0, 0))]
    return pl.pallas_call(
        _prep_kernel,
        grid=(t // tm,),
        in_specs=[pl.BlockSpec((tm, 1), row), vec, vec, vec, vec, vec, vec] + col_spec,
        out_specs=out_specs,
        out_shape=out_shape,
        compiler_params=_params("parallel"),
        name="prep_qkv",
    )(positions.reshape(t, 1), _rope_table(ROT_HALF, HEAD_DIM), _rope_table(ROT_HALF_DIFF, DIFF_SUB),
      qn_m.reshape(1, LANES), kn_m.reshape(1, LANES),
      jnp.tile(qn_d, 2).reshape(1, LANES), jnp.tile(kn_d, 2).reshape(1, LANES),
      *([proj] * 4))


def _ones_column(rows, at=0):
    return (lax.broadcasted_iota(jnp.int32, (rows, LANES), 1) == at).astype(BF16)


def _value_with_ones(v):
    return jnp.concatenate([v, jnp.ones(v.shape, BF16)], axis=1)


def _moba_kernel(q_ref, qf_ref, kmean_ref, k_ref, v_ref, o_ref, knorm_ref, *, tk):
    blk = MOBA_BLOCK
    tq = q_ref.shape[0]
    qi = pl.program_id(2)
    scale = HEAD_DIM ** -0.5
    big = NEG_INF / scale
    nbp = kmean_ref.shape[0]

    @pl.when(qi == 0)
    def _():
        knorm_ref[...] = jnp.broadcast_to(_max_row_norm(k_ref, 0, HEAD_DIM), knorm_ref.shape)

    gate = lax.dot_general(kmean_ref[...], qf_ref[...], _NT, precision=lax.Precision.HIGHEST,
                           preferred_element_type=F32)
    rid = lax.broadcasted_iota(jnp.int32, gate.shape, 0)
    own = qi * (tq // blk) + lax.broadcasted_iota(jnp.int32, gate.shape, 1) // blk
    rank = jnp.zeros(gate.shape, jnp.int32)
    for m in range(nbp):
        g_m = gate[m:m + 1, :]
        beats = (g_m > gate) | ((g_m == gate) & (rid > m))
        rank = rank + jnp.where(beats & (m < own), 1, 0)
    allowed = ((rid < own) & (rank < MOBA_TOPK)) | (rid == own)
    pen_t = jnp.where(allowed, 0.0, big)
    pen_t = jnp.concatenate([pen_t, jnp.full((LANES - nbp, tq), big, F32)], axis=0)
    penalty = pen_t.T

    q = q_ref[...]
    lane = lax.broadcasted_iota(jnp.int32, (tq, LANES), 1)
    q_pos = qi * tq + lax.broadcasted_iota(jnp.int32, (tq, tk), 0)
    col = lax.broadcasted_iota(jnp.int32, (tq, tk), 1)
    krow = lax.broadcasted_iota(jnp.int32, (tk, LANES), 0)
    klane = lax.broadcasted_iota(jnp.int32, (tk, LANES), 1)
    n_tiles = (qi * tq) // tk + 1

    def scores(n, q_aug, shift_lane, causal):
        off = pl.multiple_of(n * tk, tk)
        kb = k_ref[pl.ds(off, tk), :]
        block_of_row = n * (tk // blk) + krow // blk
        marks = ((klane == block_of_row) | (klane == shift_lane)).astype(BF16)
        s = lax.dot_general(q_aug, jnp.concatenate([kb, marks], axis=1), _NT,
                            preferred_element_type=F32) * scale
        if causal:
            s = jnp.where(off + col <= q_pos, s, NEG_INF)
        return s, off

    q_max = jnp.concatenate([q, penalty.astype(BF16)], axis=1)

    def sweep_max(n, best):
        s, _ = scores(n, q_max, -1, True)
        return jnp.maximum(best, s)

    def exact_row_max():
        best = lax.fori_loop(0, n_tiles, sweep_max, jnp.full((tq, tk), NEG_INF, F32))
        return jnp.max(best, axis=-1, keepdims=True)

    qsq = q.astype(F32)
    bound = jnp.sqrt(jnp.sum(qsq * qsq, axis=-1, keepdims=True)) * knorm_ref[0:1, 0:1] * scale
    shift = lax.cond(2.0 * jnp.max(bound) < SHIFT_SLACK, lambda: bound, exact_row_max)
    shift_lane = LANES - 1
    q_sum = jnp.concatenate(
        [q, jnp.where(lane == shift_lane, -shift / scale, penalty).astype(BF16)], axis=1)

    def tile_sum(n, causal):
        s, off = scores(n, q_sum, shift_lane, causal)
        vb = v_ref[pl.ds(off, tk), :]
        return jnp.dot(jnp.exp(s).astype(BF16), _value_with_ones(vb), preferred_element_type=F32)

    acc = lax.fori_loop(0, n_tiles - 1, lambda n, acc: acc + tile_sum(n, False),
                        jnp.zeros((tq, 2 * HEAD_DIM), F32))
    acc = acc + tile_sum(n_tiles - 1, True)
    o_ref[...] = (acc[:, :HEAD_DIM] / acc[:, HEAD_DIM:]).astype(o_ref.dtype)


def moba_attention(q, qf, kmean, k, v, v_group, batch, seq, tq=1024, tk=1024):
    t = q.shape[0]
    tq, tk = min(tq, seq), min(tk, seq)
    assert tq % MOBA_BLOCK == 0 and tk % tq == 0
    nq = seq // tq
    qspec = pl.BlockSpec((tq, HEAD_DIM), lambda b, h, i: (b * nq + i, h))
    kspec = pl.BlockSpec((seq, HEAD_DIM), lambda b, h, i: (b, h))
    vspec = pl.BlockSpec((seq, HEAD_DIM), lambda b, h, i: (b, v_group * N_HEADS + h))
    return pl.pallas_call(
        functools.partial(_moba_kernel, tk=tk),
        grid=(batch, N_HEADS, nq),
        in_specs=[qspec, qspec,
                  pl.BlockSpec((None, None, kmean.shape[2], HEAD_DIM), lambda b, h, i: (b, h, 0, 0)),
                  kspec, vspec],
        out_specs=qspec,
        out_shape=jax.ShapeDtypeStruct((t, MIX_WIDTH), BF16),
        scratch_shapes=[pltpu.VMEM((8, LANES), F32)],
        compiler_params=_params("parallel", "parallel", "arbitrary"),
        name="moba_attention",
    )(q, qf, kmean, k, v)


SHIFT_SLACK = 80.0


def _max_row_norm(x_ref, lane_lo, lane_hi, chunk=512):
    n_rows = x_ref.shape[0]
    chunk = min(chunk, n_rows)
    lane = lax.broadcasted_iota(jnp.int32, (chunk, x_ref.shape[1]), 1)
    keep = (lane >= lane_lo) & (lane < lane_hi)

    def body(c, best):
        x = x_ref[pl.ds(pl.multiple_of(c * chunk, chunk), chunk), :].astype(F32)
        sq = jnp.sum(jnp.where(keep, x * x, 0.0), axis=-1, keepdims=True)
        return jnp.maximum(best, jnp.max(sq, axis=0, keepdims=True))

    return jnp.sqrt(lax.fori_loop(0, n_rows // chunk, body, jnp.zeros((1, 1), F32)))


def _diff_kernel(lq1_ref, lk1_ref, lq2_ref, lk2_ref, g_ref, q_ref, k_ref, v_ref, o_ref, knorm_ref,
                 *, tq, tk, lam_init):
    qi = pl.program_id(2)

    @pl.when(qi == 0)
    def _():
        knorm_ref[0] = jnp.broadcast_to(_max_row_norm(k_ref, 0, DIFF_SUB), knorm_ref.shape[1:])
        knorm_ref[1] = jnp.broadcast_to(_max_row_norm(k_ref, DIFF_SUB, HEAD_DIM), knorm_ref.shape[1:])

    q = q_ref[...] * (DIFF_SUB ** -0.5)
    lane = lax.broadcasted_iota(jnp.int32, q.shape, 1)
    zero = jnp.zeros_like(q)
    qq = jnp.concatenate([jnp.where(lane < DIFF_SUB, q, zero),
                          jnp.where(lane < DIFF_SUB, zero, q)], axis=0)
    rows = 2 * tq
    q_pos = qi * tq + (lax.broadcasted_iota(jnp.int32, (rows, tk), 0) & (tq - 1))
    col = lax.broadcasted_iota(jnp.int32, (rows, tk), 1)
    n_full = (qi * tq) // tk
    last = pl.multiple_of(n_full * tk, tk)
    causal = last + col <= q_pos

    def lane_tile_max(s):
        out = s[:, :LANES]
        for c in range(LANES, tk, LANES):
            out = jnp.maximum(out, s[:, c:c + LANES])
        return out

    def sweep_max(n, best):
        kb = k_ref[pl.ds(pl.multiple_of(n * tk, tk), tk), :]
        return jnp.maximum(best, lane_tile_max(lax.dot_general(qq, kb, _NT, preferred_element_type=F32)))

    def exact_row_max():
        best = lax.fori_loop(0, n_full, sweep_max, jnp.full((rows, LANES), NEG_INF, F32))
        s_last = lax.dot_general(qq, k_ref[pl.ds(last, tk), :], _NT, preferred_element_type=F32)
        best = jnp.maximum(best, lane_tile_max(jnp.where(causal, s_last, NEG_INF)))
        return jnp.max(best, axis=-1, keepdims=True)

    qf = qq.astype(F32)
    q_norm = jnp.sqrt(jnp.sum(qf * qf, axis=-1, keepdims=True))
    first_half = lax.broadcasted_iota(jnp.int32, (rows, 1), 0) < tq
    bound = q_norm * jnp.where(first_half, knorm_ref[0][0:1, 0:1], knorm_ref[1][0:1, 0:1])
    shift = lax.cond(2.0 * jnp.max(bound) < SHIFT_SLACK, lambda: bound, exact_row_max)

    lane_r = lax.broadcasted_iota(jnp.int32, (rows, LANES), 1)
    q_aug = jnp.concatenate([qq, jnp.where(lane_r == 0, -shift, 0.0).astype(BF16)], axis=1)
    ones_col = _ones_column(tk)

    def tile_sum(off, mask):
        k_aug = jnp.concatenate([k_ref[pl.ds(off, tk), :], ones_col], axis=1)
        v_aug = _value_with_ones(v_ref[pl.ds(off, tk), :])
        outs = []
        for half in range(2):
            s = lax.dot_general(q_aug[half * tq:(half + 1) * tq], k_aug, _NT,
                                preferred_element_type=F32)
            if mask is not None:
                s = jnp.where(mask[half * tq:(half + 1) * tq], s, NEG_INF)
            outs.append(jnp.dot(jnp.exp(s).astype(BF16), v_aug, preferred_element_type=F32))
        return jnp.concatenate(outs, axis=0)

    def sweep_sum(n, acc):
        return acc + tile_sum(pl.multiple_of(n * tk, tk), None)

    acc = lax.fori_loop(0, n_full, sweep_sum, jnp.zeros((rows, 2 * HEAD_DIM), F32))
    acc = acc + tile_sum(last, causal)

    lam = (jnp.exp(jnp.sum(lq1_ref[...] * lk1_ref[...], axis=-1, keepdims=True))
           - jnp.exp(jnp.sum(lq2_ref[...] * lk2_ref[...], axis=-1, keepdims=True)) + lam_init)
    out = acc[:, :HEAD_DIM] / acc[:, HEAD_DIM:]
    o = out[:tq] - lam * out[tq:]
    ms = jnp.mean(o * o, axis=-1, keepdims=True)
    o_ref[...] = (o * lax.rsqrt(ms + NORM_EPS) * g_ref[...] * (1.0 - lam_init)).astype(o_ref.dtype)


def diff_attention(q, k, v, v_group, lq1, lk1, lq2, lk2, subln, lam_init, batch, seq, tq=1024, tk=1024):
    t = q.shape[0]
    tq, tk = min(tq, seq), min(tk, seq)
    nq = seq // tq
    qspec = pl.BlockSpec((tq, HEAD_DIM), lambda b, h, i: (b * nq + i, h))
    kspec = pl.BlockSpec((seq, HEAD_DIM), lambda b, h, i: (b, h))
    vspec = pl.BlockSpec((seq, HEAD_DIM), lambda b, h, i: (b, v_group * N_HEADS + h))
    lspec = pl.BlockSpec((1, DIFF_SUB), lambda b, h, i: (0, 0))
    return pl.pallas_call(
        functools.partial(_diff_kernel, tq=tq, tk=tk, lam_init=lam_init),
        grid=(batch, N_HEADS, nq),
        in_specs=[lspec, lspec, lspec, lspec,
                  pl.BlockSpec((1, HEAD_DIM), lambda b, h, i: (0, 0)), qspec, kspec, vspec],
        out_specs=qspec,
        out_shape=jax.ShapeDtypeStruct((t, MIX_WIDTH), BF16),
        scratch_shapes=[pltpu.VMEM((2, 8, LANES), F32)],
        compiler_params=_params("parallel", "parallel", "arbitrary"),
        name="diff_attention",
    )(lq1.reshape(1, DIFF_SUB), lk1.reshape(1, DIFF_SUB), lq2.reshape(1, DIFF_SUB),
      lk2.reshape(1, DIFF_SUB), subln.reshape(1, HEAD_DIM), q, k, v)


def _stick_kernel(q_ref, k_ref, v_ref, o_ref, *, tq, tk):
    qi = pl.program_id(2)
    scale = HEAD_DIM ** -0.5
    n_heads = q_ref.shape[1] // HEAD_DIM
    upper = (lax.broadcasted_iota(jnp.int32, (tk, tk), 0)
             > lax.broadcasted_iota(jnp.int32, (tk, tk), 1)).astype(BF16)
    q_pos = qi * tq + lax.broadcasted_iota(jnp.int32, (tq, tk), 0)
    col = lax.broadcasted_iota(jnp.int32, (tq, tk), 1)
    n_diag = tq // tk
    n_past = qi * n_diag

    def tile(n, hd, later, acc, masked, row0=0):
        off = pl.multiple_of(n * tk, tk)
        cols = slice(hd * HEAD_DIM, (hd + 1) * HEAD_DIM)
        kb = k_ref[pl.ds(off, tk), cols]
        vb = v_ref[pl.ds(off, tk), cols]
        z = lax.dot_general(q_ref[row0:, cols], kb, _NT, preferred_element_type=F32) * scale
        sp = jnp.maximum(z, 0.0) + jnp.log(1.0 + jnp.exp(-jnp.abs(z)))
        if masked:
            rows = (tq - row0, tk)
            strict = (off + lax.broadcasted_iota(jnp.int32, rows, 1)
                      < qi * tq + row0 + lax.broadcasted_iota(jnp.int32, rows, 0))
            sp = jnp.where(strict, sp, 0.0)
        hi = sp.astype(BF16)
        lo = (sp - hi.astype(F32)).astype(BF16)
        between = (jnp.dot(hi, upper, preferred_element_type=F32)
                   + jnp.dot(lo, upper, preferred_element_type=F32))
        later_rows = jnp.concatenate([later[row0:]] * (tk // LANES), axis=1)
        a = jnp.exp(z - sp - between - later_rows)
        if masked:
            a = jnp.where(strict, a, 0.0)
        pv = jnp.dot(a.astype(BF16), vb, preferred_element_type=F32)
        add = jnp.broadcast_to(jnp.sum(sp, axis=-1, keepdims=True), (tq - row0, LANES))
        if row0:
            pv = jnp.concatenate([jnp.zeros((row0, HEAD_DIM), F32), pv], axis=0)
            add = jnp.concatenate([jnp.zeros((row0, LANES), F32), add], axis=0)
        return later + add, acc + pv

    state = [(jnp.zeros((tq, LANES), F32), jnp.zeros((tq, HEAD_DIM), F32)) for _ in range(n_heads)]
    for d in range(n_diag):
        j = n_diag - 1 - d
        state = [tile(n_past + j, hd, *state[hd], True, row0=j * tk) for hd in range(n_heads)]

    def body(it, carry):
        return tuple(tile(n_past - 1 - it, hd, *carry[hd], False) for hd in range(n_heads))

    state = lax.fori_loop(0, n_past, body, tuple(state))
    for hd in range(n_heads):
        o_ref[:, hd * HEAD_DIM:(hd + 1) * HEAD_DIM] = state[hd][1].astype(o_ref.dtype)


def stick_attention(qkv, q_group, batch, seq, tq=1024, tk=256, heads_per_step=2):
    t = qkv.shape[0]
    tq, tk = min(tq, seq), min(tk, seq)
    nq = seq // tq
    width = heads_per_step * HEAD_DIM
    per_group = MIX_WIDTH // width

    def col(group):
        return lambda b, h, i: (b, group * per_group + h)

    qspec = pl.BlockSpec((tq, width), lambda b, h, i: (b * nq + i, q_group * per_group + h))
    kspec = pl.BlockSpec((seq, width), col(q_group + 1))
    vspec = pl.BlockSpec((seq, width), col(q_group + 2))
    return pl.pallas_call(
        functools.partial(_stick_kernel, tq=tq, tk=tk),
        grid=(batch, N_HEADS // heads_per_step, nq),
        in_specs=[qspec, kspec, vspec],
        out_specs=pl.BlockSpec((tq, width), lambda b, h, i: (b * nq + i, h)),
        out_shape=jax.ShapeDtypeStruct((t, MIX_WIDTH), BF16),
        compiler_params=_params("parallel", "parallel", "arbitrary"),
        name="stick_attention",
    )(qkv, qkv, qkv)


def _merge_kernel(om_ref, od_ref, os_ref, w_ref, gm_ref, gd_ref, gs_ref, o_ref):
    acc = None
    for k, (g_ref, x_ref) in enumerate(((gm_ref, om_ref), (gd_ref, od_ref), (gs_ref, os_ref))):
        term = jax.nn.sigmoid(g_ref[...].astype(F32)) * jnp.dot(
            x_ref[...], w_ref[k].astype(BF16), preferred_element_type=F32)
        acc = term if acc is None else acc + term
    o_ref[...] = acc.astype(o_ref.dtype)


def merge_branches(o_m, o_d, o_s, w_branch, layer, proj, gate_col, tm=1024, tn=512):
    t, kdim = o_m.shape
    d = w_branch.shape[3]
    tm, tn = min(tm, t), min(tn, d)
    ospec = pl.BlockSpec((tm, kdim), lambda i, j: (i, 0))
    gspecs = [pl.BlockSpec((tm, tn), functools.partial(lambda i, j, c: (i, c + j), c=(gate_col + c * d) // tn))
              for c in range(3)]
    return pl.pallas_call(
        _merge_kernel,
        grid=(t // tm, d // tn),
        in_specs=[ospec, ospec, ospec,
                  pl.BlockSpec((None, 3, kdim, tn), lambda i, j: (layer, 0, 0, j))] + gspecs,
        out_specs=pl.BlockSpec((tm, tn), lambda i, j: (i, j)),
        out_shape=jax.ShapeDtypeStruct((t, d), BF16),
        compiler_params=_params("parallel", "parallel"),
        name="merge_branches",
    )(o_m, o_d, o_s, w_branch, proj, proj, proj)


def _extract_top(x, rounds, break_ties):
    rows, n = x.shape
    rid = lax.broadcasted_iota(jnp.int32, (rows, n), 0)
    kid = lax.broadcasted_iota(jnp.int32, (rounds, n), 0)

    def body(k, carry):
        x, rank, vals = carry
        best = jnp.max(x, axis=0, keepdims=True)
        hit = x == best
        if break_ties:
            hit = rid == jnp.min(jnp.where(hit, rid, rows), axis=0, keepdims=True)
        rank = jnp.where(hit, k, rank)
        x = jnp.where(hit, -jnp.inf, x)
        vals = jnp.where(kid == k, best, vals)
        return x, rank, vals

    init = (x, jnp.full((rows, n), rounds, jnp.int32), jnp.zeros((rounds, n), F32))
    _, rank, vals = lax.fori_loop(0, rounds, body, init)
    return vals, rank


def _peer_score_kernel(q_ref, keys_ref, cb_o, eb_o, na_o, wa_o):
    nk, topk = PEER_NKEYS, PEER_TOPK
    s = lax.dot_general(keys_ref[...], q_ref[...], _NT, precision=lax.Precision.HIGHEST,
                        preferred_element_type=F32)

    def taken_off(rank):
        taken = jnp.sum((rank < topk).astype(jnp.int32), axis=0, keepdims=True)
        return jnp.max(jnp.where(taken != topk, 1, 0))

    suspect = _peer_select(s, cb_o, eb_o, na_o, wa_o, False, taken_off)

    @pl.when(suspect > 0)
    def _():
        _peer_select(s, cb_o, eb_o, na_o, wa_o, True, taken_off)


def _peer_select(s, cb_o, eb_o, na_o, wa_o, break_ties, taken_off):
    nk, topk = PEER_NKEYS, PEER_TOPK
    sa, sb = s[:nk], s[nk:]
    va, rank_a = _extract_top(sa, topk, break_ties)
    vb, rank_b = _extract_top(sb, topk, break_ties)
    n = s.shape[1]
    row8 = lax.broadcasted_iota(jnp.int32, (8, n), 0)
    ninf = -jnp.inf
    a = [va[r:r + 1] for r in range(topk)]
    groups = [a[0] + vb]
    for r, lim in ((1, 8), (2, 5), (3, 4), (4, 3)):
        groups.append(jnp.where(row8 < lim, a[r] + vb[0:8], ninf))
    a_mix = jnp.where(row8 < 2, a[5], jnp.where(row8 < 4, a[6], jnp.where(
        row8 < 6, a[7], jnp.where(row8 == 6, a[8], a[9]))))
    b_mix = jnp.where((row8 < 6) & ((row8 & 1) == 1), vb[1:2], vb[0:1])
    groups.append(a_mix + b_mix)
    tail = jnp.full((8, n), ninf, F32)
    for k in range(6):
        tail = jnp.where(row8 == k, a[10 + k] + vb[0:1], tail)
    groups.append(tail)
    cand = jnp.concatenate(groups, axis=0)
    top_s, rank_c = _extract_top(cand, topk, break_ties)
    taken = (rank_c < topk).astype(jnp.int32)
    counts = [jnp.sum(taken[0:16], axis=0, keepdims=True)]
    counts += [jnp.sum(taken[8 + 8 * r:16 + 8 * r], axis=0, keepdims=True) for r in range(1, 5)]
    counts += [taken[48 + 2 * k:49 + 2 * k] + taken[49 + 2 * k:50 + 2 * k] for k in range(3)]
    counts += [taken[54 + k:55 + k] for k in range(8)]
    count = jnp.concatenate(counts, axis=0)
    z = jnp.sum(jnp.exp(top_s - top_s[0:1]), axis=0, keepdims=True)
    na = jnp.zeros(sa.shape, jnp.int32)
    for r in range(topk):
        na = jnp.where(rank_a == r, count[r:r + 1], na)
    cb_o[...] = rank_b.astype(F32)
    eb_o[...] = jnp.exp(sb - vb[0:1])
    na_o[...] = na.astype(F32)
    wa_o[...] = jnp.exp(sa - va[0:1]) / z
    return taken_off(rank_a) + taken_off(rank_b) + taken_off(rank_c)


def peer_scores(q, key_a, key_b, tn=256):
    t = q.shape[0]
    tn = min(tn, t)
    zeros = jnp.zeros_like(key_a)
    keys = jnp.concatenate([jnp.concatenate([key_a, zeros], axis=-1),
                            jnp.concatenate([zeros, key_b], axis=-1)], axis=1)
    out = jax.ShapeDtypeStruct((PEER_HEADS, PEER_NKEYS, t), F32)
    ospec = pl.BlockSpec((None, PEER_NKEYS, tn), lambda i, h: (h, 0, i))
    return pl.pallas_call(
        _peer_score_kernel,
        grid=(t // tn, PEER_HEADS),
        in_specs=[pl.BlockSpec((tn, LANES), lambda i, h: (i, h)),
                  pl.BlockSpec((None, 2 * PEER_NKEYS, LANES), lambda i, h: (h, 0, 0))],
        out_specs=[ospec] * 4,
        out_shape=[out] * 4,
        compiler_params=_params("parallel", "parallel"),
        name="peer_scores",
    )(q, keys)


def _peer_dense_kernel(xt_ref, u_ref, vt_ref, cb_ref, eb_ref, na_ref, wa_ref, o_ref,
                       coef_even, coef_odd, *, te):
    e = pl.program_id(1)
    last = pl.num_programs(1) - 2
    tn = o_ref.shape[1]
    half = tn // 2

    @pl.when(e == 0)
    def _():
        o_ref[...] = jnp.zeros_like(o_ref)
        coef_odd[...] = jnp.zeros_like(coef_odd)

    def gate_into_coef(hid, lane0, coef_out):
        chunk = 16
        for di in range(te // PEER_NKEYS):
            i = jnp.minimum(e, last) * (te // PEER_NKEYS) + di
            for sub in range(0, hid.shape[1], LANES):
                lanes = slice(lane0 + sub, lane0 + sub + LANES)
                n_rows = [na_ref[i, h:h + 1, lanes] for h in range(PEER_HEADS)]
                w_rows = [wa_ref[i, h:h + 1, lanes] for h in range(PEER_HEADS)]
                for r in range(0, PEER_NKEYS, chunk):
                    rs = slice(r, r + chunk)
                    w = None
                    for h in range(PEER_HEADS):
                        term = jnp.where(cb_ref[h, rs, lanes] < n_rows[h],
                                         eb_ref[h, rs, lanes] * w_rows[h], 0.0)
                        w = term if w is None else w + term
                    row0 = di * PEER_NKEYS + r
                    x = hid[row0:row0 + chunk, sub:sub + LANES]
                    act = 0.5 * x * (1.0 + lax.erf(x * (2.0 ** -0.5)))
                    coef_out[row0:row0 + chunk, lanes] = (act * w).astype(BF16)

    def step(coef_out, coef_in):
        lo, hi = slice(0, half), slice(half, tn)
        hid_lo = jnp.dot(u_ref[...], xt_ref[:, lo], preferred_element_type=F32)
        o_ref[:, lo] += jnp.dot(vt_ref[...], coef_in[:, lo], preferred_element_type=F32)
        hid_hi = jnp.dot(u_ref[...], xt_ref[:, hi], preferred_element_type=F32)
        gate_into_coef(hid_lo, 0, coef_out)
        o_ref[:, hi] += jnp.dot(vt_ref[...], coef_in[:, hi], preferred_element_type=F32)
        gate_into_coef(hid_hi, half, coef_out)

    @pl.when((e & 1) == 0)
    def _():
        step(coef_even, coef_odd)

    @pl.when((e & 1) == 1)
    def _():
        step(coef_odd, coef_even)


def peer_dense(xt, u, vt, cb, eb, na, wa, tn=512, te=PEER_TILE):
    d, t = xt.shape
    n_exp = u.shape[0]
    tn = min(tn, t)
    n_e = n_exp // te
    assert vt.shape == (n_e, d, te)
    once = pl.Buffered(1)
    sspec = pl.BlockSpec((PEER_HEADS, PEER_NKEYS, tn), lambda i, e: (0, 0, i), pipeline_mode=once)
    aspec = pl.BlockSpec((PEER_NKEYS, PEER_HEADS, tn), lambda i, e: (0, 0, i), pipeline_mode=once)
    na, wa = na.transpose(1, 0, 2), wa.transpose(1, 0, 2)
    return pl.pallas_call(
        functools.partial(_peer_dense_kernel, te=te),
        grid=(t // tn, n_e + 1),
        in_specs=[pl.BlockSpec((d, tn), lambda i, e: (0, i), pipeline_mode=once),
                  pl.BlockSpec((te, d), lambda i, e: (jnp.minimum(e, n_e - 1), 0)),
                  pl.BlockSpec((None, d, te), lambda i, e: (jnp.maximum(e - 1, 0), 0, 0)),
                  sspec, sspec, aspec, aspec],
        out_specs=pl.BlockSpec((d, tn), lambda i, e: (0, i)),
        out_shape=jax.ShapeDtypeStruct((d, t), F32),
        scratch_shapes=[pltpu.VMEM((te, tn), BF16), pltpu.VMEM((te, tn), BF16)],
        compiler_params=_params("parallel", "arbitrary"),
        name="peer_dense",
    )(xt, u, vt, cb, eb, na, wa)


def kernel(x, p, positions, norm_mix, w_in, qn_moba, kn_moba, qn_diff, kn_diff, lambda_q1, lambda_k1, lambda_q2, lambda_k2, subln_diff, w_branch, w_out, norm_ffn, peer_wq, peer_key_a, peer_key_b, peer_u, peer_v, norm_ple, w_ple_gate, w_ple_proj):
    b, s, d = x.shape
    t = b * s
    depth = w_in.shape[0]
    nb = s // MOBA_BLOCK
    gate_col = 9 * MIX_WIDTH
    h = x.reshape(t, d)
    p_flat = p.reshape(depth, t, -1)
    for i in range(depth):
        n1 = rmsnorm(h, norm_mix[i])
        proj = matmul(n1, w_in, i, out_dtype=BF16)
        qm, qmf, km, qd, kd, kmean = prep_qkv(
            proj, positions, qn_moba[i], kn_moba[i], qn_diff[i], kn_diff[i])
        kmean = kmean.reshape(b, nb, N_HEADS, HEAD_DIM).transpose(0, 2, 1, 3)
        kmean = jnp.pad(kmean, ((0, 0), (0, 0), (0, -nb % 8), (0, 0)))
        o_m = moba_attention(qm, qmf, kmean, km, proj, 2, b, s)
        lam_init = 0.8 - 0.6 * math.exp(-0.3 * i)
        o_d = diff_attention(qd, kd, proj, 5, lambda_q1[i], lambda_k1[i], lambda_q2[i], lambda_k2[i],
                             subln_diff[i], lam_init, b, s)
        o_s = stick_attention(proj, 6, b, s)
        merged = merge_branches(o_m, o_d, o_s, w_branch, i, proj, gate_col)
        h = matmul(merged, w_out, i, mode="residual", res=h)
        n2, n2_t = rmsnorm(h, norm_ffn[i], with_transpose=True)
        q = matmul(n2, peer_wq, i)
        cb, eb, na, wa = peer_scores(q, peer_key_a[i], peer_key_b[i])
        ffn_t = peer_dense(n2_t, to_bf16(peer_u, i), to_bf16(peer_v, i, transpose_tile=PEER_TILE),
                           cb, eb, na, wa, te=PEER_TILE)
        h, n3 = residual_rmsnorm(h, ffn_t, norm_ple[i])
        h = matmul(n3, w_ple_gate, i, mode="ple", res=h, p=p_flat, wp=w_ple_proj)
    return h.reshape(b, s, d)
```
